```python
import math
import jax, jax.numpy as jnp
from jax import lax
import numpy as np

D_MODEL = 1024
BATCH = 8
SEQ = 2048
DEPTH = 2

GRID_W = 64
Q_BLOCK = 128
HEAD_DIM = 64
ROPE_THETA = 10000.0
EPS = 1e-6
NEG_INF = -1e30

A_HEADS = 8
A_KV_HEADS = 2

B_HEADS = 8
B_BRANCHES = ((128, 1), (512, 4), (2048, 16))
NUM_BUCKETS = 32
REL_MAX_DISTANCE = 1024

C_HEADS = 16
C_Q_RANK = 256
C_KV_RANK = 128
C_NOPE_DIM = 64
C_ROPE_DIM = 32
C_V_DIM = 64

D_FF = 4 * D_MODEL

A_Q_W = A_HEADS * HEAD_DIM
A_KV_W = A_KV_HEADS * HEAD_DIM
B_W = B_HEADS * HEAD_DIM
AB_IN_W = A_Q_W + 2 * A_KV_W + 3 * B_W
AB_OUT_IN = A_Q_W + B_W
C_DOWN_W = C_Q_RANK + C_KV_RANK + C_ROPE_DIM
C_QK_DIM = C_NOPE_DIM + C_ROPE_DIM
C_OUT_IN = C_HEADS * C_V_DIM
N_EVEN = (DEPTH + 1) // 2
N_ODD = DEPTH // 2

kernel_name = "hybrid_gqa_dilated_mla_encoder"


def rms_norm(x, g):
    xf = x.astype(jnp.float32)
    y = xf * lax.rsqrt(jnp.mean(xf * xf, axis=-1, keepdims=True) + EPS)
    return (y * g.astype(jnp.float32)).astype(x.dtype)


def rope_angles(pos, dim):
    inv_freq = ROPE_THETA ** (-jnp.arange(0, dim, 2, dtype=jnp.float32) / dim)
    return pos.astype(jnp.float32)[:, None] * inv_freq[None, :]


def apply_rope(x, cos, sin):
    xf = x.astype(jnp.float32)
    half = x.shape[-1] // 2
    x1, x2 = xf[..., :half], xf[..., half:]
    out = jnp.concatenate([x1 * cos - x2 * sin, x2 * cos + x1 * sin], axis=-1)
    return out.astype(x.dtype)


def unblock(o):
    nb, b, q = o.shape[:3]
    return jnp.moveaxis(o, 0, 1).reshape((b, nb * q) + o.shape[3:])


def t5_bucket(rel):
    nb = NUM_BUCKETS // 2
    max_exact = nb // 2
    base = jnp.where(rel > 0, nb, 0)
    n = jnp.abs(rel)
    nf = jnp.maximum(n, 1).astype(jnp.float32)
    large = max_exact + (jnp.log(nf / max_exact) / math.log(REL_MAX_DISTANCE / max_exact)
                         * (nb - max_exact)).astype(jnp.int32)
    large = jnp.minimum(large, nb - 1)
    return base + jnp.where(n < max_exact, n, large)


def gqa_attention(q, k, v):
    b, s = q.shape[:2]
    rep = A_HEADS // A_KV_HEADS
    q = q.reshape(b, s, A_KV_HEADS, rep, HEAD_DIM)
    scale = HEAD_DIM ** -0.5

    def block(i):
        qb = lax.dynamic_slice_in_dim(q, i * Q_BLOCK, Q_BLOCK, axis=1)
        logits = jnp.einsum('bqgrd,bkgd->bgrqk', qb, k).astype(jnp.float32) * scale
        p = jax.nn.softmax(logits, axis=-1).astype(v.dtype)
        return jnp.einsum('bgrqk,bkgd->bqgrd', p, v)

    o = unblock(lax.map(block, jnp.arange(s // Q_BLOCK)))
    return o.reshape(b, s, A_HEADS * HEAD_DIM)


def dilated_attention(q, k, v, rel_bias):
    b, s = q.shape[:2]
    scale = HEAD_DIM ** -0.5
    pad = max((w // (2 * d)) * d for w, d in B_BRANCHES)
    kp = jnp.pad(k, ((0, 0), (pad, pad), (0, 0), (0, 0)))
    vp = jnp.pad(v, ((0, 0), (pad, pad), (0, 0), (0, 0)))
    offsets, biases = [], []
    for w, d in B_BRANCHES:
        n_side = w // (2 * d)
        off = jnp.arange(-n_side, n_side + 1, dtype=jnp.int32) * d
        offsets.append(off)
        biases.append(rel_bias[t5_bucket(off)].astype(jnp.float32).T)

    def block(i):
        t = i * Q_BLOCK + jnp.arange(Q_BLOCK, dtype=jnp.int32)
        qb = lax.dynamic_slice_in_dim(q, i * Q_BLOCK, Q_BLOCK, axis=1)
        outs, lses = [], []
        for off, bias in zip(offsets, biases):
            pos = t[:, None] + off[None, :]
            valid = (pos >= 0) & (pos < s)
            kg = jnp.take(kp, pos + pad, axis=1)
            vg = jnp.take(vp, pos + pad, axis=1)
            logits = jnp.einsum('bqhd,bqkhd->bhqk', qb, kg).astype(jnp.float32) * scale
            logits = jnp.where(valid[None, None], logits + bias[None, :, None, :], NEG_INF)
            m = jnp.max(logits, axis=-1, keepdims=True)
            p = jnp.exp(logits - m)
            l = jnp.sum(p, axis=-1, keepdims=True)
            o = jnp.einsum('bhqk,bqkhd->bqhd', (p / l).astype(vg.dtype), vg)
            outs.append(o.astype(jnp.float32))
            lses.append((m + jnp.log(l))[..., 0])
        wts = jax.nn.softmax(jnp.stack(lses, axis=0), axis=0)
        wts = jnp.transpose(wts, (0, 1, 3, 2))[..., None]
        return jnp.sum(jnp.stack(outs, axis=0) * wts, axis=0).astype(q.dtype)

    o = unblock(lax.map(block, jnp.arange(s // Q_BLOCK)))
    return o.reshape(b, s, B_HEADS * HEAD_DIM)


def mla_attention(xn, w_down, q_norm_g, kv_norm_g, w_uq, w_ukv, cos, sin):
    b, s = xn.shape[:2]
    h = xn @ w_down
    c_q, c_kv, k_r = jnp.split(h, [C_Q_RANK, C_Q_RANK + C_KV_RANK], axis=-1)
    q = (rms_norm(c_q, q_norm_g) @ w_uq).reshape(b, s, C_HEADS, C_QK_DIM)
    q_n, q_r = q[..., :C_NOPE_DIM], q[..., C_NOPE_DIM:]
    q_r = apply_rope(q_r, cos[:, None, :], sin[:, None, :])
    k_r = apply_rope(k_r, cos, sin)
    kv = (rms_norm(c_kv, kv_norm_g) @ w_ukv).reshape(b, s, C_HEADS, C_NOPE_DIM + C_V_DIM)
    k_n, v = kv[..., :C_NOPE_DIM], kv[..., C_NOPE_DIM:]
    scale = C_QK_DIM ** -0.5

    def block(i):
        qnb = lax.dynamic_slice_in_dim(q_n, i * Q_BLOCK, Q_BLOCK, axis=1)
        qrb = lax.dynamic_slice_in_dim(q_r, i * Q_BLOCK, Q_BLOCK, axis=1)
        logits = (jnp.einsum('bqhd,bkhd->bhqk', qnb, k_n)
                  + jnp.einsum('bqhr,bkr->bhqk', qrb, k_r)).astype(jnp.float32) * scale
        p = jax.nn.softmax(logits, axis=-1).astype(v.dtype)
        return jnp.einsum('bhqk,bkhd->bqhd', p, v)

    o = unblock(lax.map(block, jnp.arange(s // Q_BLOCK)))
    return o.reshape(b, s, C_OUT_IN)


def setup_inputs(seed: int = 0) -> dict:
    key = jax.random.key(seed)
    ks = jax.random.split(key, 20)
    f32 = jnp.float32

    def w(k, shape, fan_in):
        return jax.random.normal(k, shape, f32) * (fan_in ** -0.5)

    def g(k, shape):
        return 1.0 + 0.02 * jax.random.normal(k, shape, f32)

    return {
        "x": jax.random.normal(ks[0], (BATCH, SEQ, D_MODEL), f32),
        "norm_mix_g": g(ks[1], (DEPTH, D_MODEL)),
        "norm_mlp_g": g(ks[2], (DEPTH, D_MODEL)),
        "ab_w_in": w(ks[3], (N_EVEN, D_MODEL, AB_IN_W), D_MODEL),
        "a_q_norm_g": g(ks[4], (N_EVEN, HEAD_DIM)),
        "a_k_norm_g": g(ks[5], (N_EVEN, HEAD_DIM)),
        "ab_w_out": w(ks[6], (N_EVEN, AB_OUT_IN, D_MODEL), AB_OUT_IN),
        "rel_bias": 0.2 * jax.random.normal(ks[7], (NUM_BUCKETS, B_HEADS), f32),
        "c_w_down": w(ks[8], (N_ODD, D_MODEL, C_DOWN_W), D_MODEL),
        "c_q_norm_g": g(ks[9], (N_ODD, C_Q_RANK)),
        "c_kv_norm_g": g(ks[10], (N_ODD, C_KV_RANK)),
        "c_w_uq": w(ks[11], (N_ODD, C_Q_RANK, C_HEADS * C_QK_DIM), C_Q_RANK),
        "c_w_ukv": w(ks[12], (N_ODD, C_KV_RANK, C_HEADS * (C_NOPE_DIM + C_V_DIM)), C_KV_RANK),
        "c_w_out": w(ks[13], (N_ODD, C_OUT_IN, D_MODEL), C_OUT_IN),
        "mlp_w1": w(ks[14], (DEPTH, D_MODEL, D_FF), D_MODEL),
        "mlp_w2": w(ks[15], (DEPTH, D_FF, D_MODEL), D_FF),
        "final_norm_g": g(ks[16], (D_MODEL,)),
    }


def reference(x, norm_mix_g, norm_mlp_g, ab_w_in, a_q_norm_g, a_k_norm_g, ab_w_out,
              rel_bias, c_w_down, c_q_norm_g, c_kv_norm_g, c_w_uq, c_w_ukv, c_w_out,
              mlp_w1, mlp_w2, final_norm_g):
    b, s, _ = x.shape
    rows = s // GRID_W
    row = jnp.repeat(jnp.arange(rows, dtype=jnp.int32), GRID_W)
    col = jnp.tile(jnp.arange(GRID_W, dtype=jnp.int32), rows)
    ang_ax = jnp.concatenate([rope_angles(row, HEAD_DIM // 2),
                              rope_angles(col, HEAD_DIM // 2)], axis=-1)
    cos_ax, sin_ax = jnp.cos(ang_ax)[:, None, :], jnp.sin(ang_ax)[:, None, :]
    ang_c = rope_angles(jnp.arange(s, dtype=jnp.int32), C_ROPE_DIM)
    cos_c, sin_c = jnp.cos(ang_c), jnp.sin(ang_c)
    split_ab = np.cumsum([A_Q_W, A_KV_W, A_KV_W, B_W, B_W]).tolist()

    h = x
    for layer in range(DEPTH):
        j = layer // 2
        hn = rms_norm(h, norm_mix_g[layer])
        if layer % 2 == 0:
            proj = hn @ ab_w_in[j]
            qa, ka, va, qb, kb, vb = jnp.split(proj, split_ab, axis=-1)
            qa = rms_norm(qa.reshape(b, s, A_HEADS, HEAD_DIM), a_q_norm_g[j])
            ka = rms_norm(ka.reshape(b, s, A_KV_HEADS, HEAD_DIM), a_k_norm_g[j])
            qa = apply_rope(qa, cos_ax, sin_ax)
            ka = apply_rope(ka, cos_ax, sin_ax)
            va = va.reshape(b, s, A_KV_HEADS, HEAD_DIM)
            o_a = gqa_attention(qa, ka, va)
            o_b = dilated_attention(qb.reshape(b, s, B_HEADS, HEAD_DIM),
                                    kb.reshape(b, s, B_HEADS, HEAD_DIM),
                                    vb.reshape(b, s, B_HEADS, HEAD_DIM), rel_bias)
            mix = jnp.concatenate([o_a, o_b], axis=-1) @ ab_w_out[j]
        else:
            o_c = mla_attention(hn, c_w_down[j], c_q_norm_g[j], c_kv_norm_g[j],
                                c_w_uq[j], c_w_ukv[j], cos_c, sin_c)
            mix = o_c @ c_w_out[j]
        h = h + mix
        hn = rms_norm(h, norm_mlp_g[layer])
        h = h + jnp.square(jax.nn.relu(hn @ mlp_w1[layer])) @ mlp_w2[layer]
    return rms_norm(h, final_norm_g)
```

```python
import functools
import math

import jax
import jax.numpy as jnp
import numpy as np
from jax import lax
from jax.experimental import pallas as pl
from jax.experimental.pallas import tpu as pltpu

F32 = jnp.float32
BF16 = jnp.bfloat16

D_MODEL = 1024
GRID_W = 64
HEAD_DIM = 64
ROPE_THETA = 10000.0
EPS = 1e-6
NEG_INF = -1e30

A_HEADS = 8
A_KV_HEADS = 2
B_HEADS = 8
B_BRANCHES = ((128, 1), (512, 4), (2048, 16))
NUM_BUCKETS = 32
REL_MAX_DISTANCE = 1024

C_HEADS = 16
C_Q_RANK = 256
C_KV_RANK = 128
C_NOPE_DIM = 64
C_ROPE_DIM = 32
C_V_DIM = 64
C_QK_DIM = C_NOPE_DIM + C_ROPE_DIM
D_FF = 4 * D_MODEL

A_Q_W = A_HEADS * HEAD_DIM
A_KV_W = A_KV_HEADS * HEAD_DIM
B_W = B_HEADS * HEAD_DIM
AB_IN_W = A_Q_W + 2 * A_KV_W + 3 * B_W

LANES = 128
VMEM_LIMIT_BYTES = 56 * 1024 * 1024

TOKEN_TILE = 512
ATTN_Q_TILE = 256
DIL_Q_BLOCK = 128
FF_CHUNK = 1024

COL_QA = 0
COL_KA = A_Q_W
COL_VA = COL_KA + A_KV_W
COL_QB = COL_VA + A_KV_W
COL_KB = COL_QB + B_W
COL_VB = COL_KB + B_W


def _rms(x):
    return x * lax.rsqrt(jnp.mean(x * x, axis=-1, keepdims=True) + EPS)


def _dot(a, b):
    return jnp.dot(a, b, preferred_element_type=F32)


def _dot_nt(a, b):
    return lax.dot_general(a, b, (((1,), (1,)), ((), ())), preferred_element_type=F32)


def _softmax_pv(s, v):
    m = jnp.max(s, axis=-1, keepdims=True)
    p = jnp.exp(s - m)
    l = jnp.sum(p, axis=-1, keepdims=True)
    return _dot(p.astype(BF16), v) / l


def _proj0_kernel(x_ref, g_ref, w_ref, gq_ref, gk_ref, c_ref, sa_ref, sb_ref, o_ref):
    tm = x_ref.shape[0]
    xn = (_rms(x_ref[...]) * g_ref[...]).astype(BF16)
    low = lax.broadcasted_iota(jnp.int32, (tm, LANES), 1) < HEAD_DIM
    cos, sin_a, sin_b = c_ref[...], sa_ref[...], sb_ref[...]

    def norm_rope(y, gain):
        sq = y * y
        s_all = jnp.sum(sq, axis=-1, keepdims=True)
        s_low = jnp.sum(jnp.where(low, sq, 0.0), axis=-1, keepdims=True)
        ms = jnp.where(low, s_low, s_all - s_low) * (1.0 / HEAD_DIM)
        y = y * lax.rsqrt(ms + EPS) * gain
        return (y * cos + pltpu.roll(y, LANES - HEAD_DIM // 2, 1) * sin_a
                + pltpu.roll(y, HEAD_DIM // 2, 1) * sin_b)

    scale = HEAD_DIM ** -0.5
    for c0 in range(COL_QA, COL_KA, LANES):
        y = _dot(xn, w_ref[:, c0:c0 + LANES])
        o_ref[:, c0:c0 + LANES] = (norm_rope(y, gq_ref[...]) * scale).astype(BF16)
    y = _dot(xn, w_ref[:, COL_KA:COL_VA])
    o_ref[:, COL_KA:COL_VA] = norm_rope(y, gk_ref[...]).astype(BF16)
    o_ref[:, COL_VA:COL_QB] = _dot(xn, w_ref[:, COL_VA:COL_QB]).astype(BF16)
    o_ref[:, COL_QB:COL_KB] = (_dot(xn, w_ref[:, COL_QB:COL_KB]) * scale).astype(BF16)
    o_ref[:, COL_KB:] = _dot(xn, w_ref[:, COL_KB:]).astype(BF16)


def _gqa_kernel(q_ref, k_ref, v_ref, o_ref):
    rep = A_HEADS // A_KV_HEADS
    outs = []
    for g in range(A_KV_HEADS):
        k = k_ref[:, g * HEAD_DIM:(g + 1) * HEAD_DIM]
        v = v_ref[:, g * HEAD_DIM:(g + 1) * HEAD_DIM]
        for r in range(rep):
            h = g * rep + r
            q = q_ref[:, h * HEAD_DIM:(h + 1) * HEAD_DIM]
            outs.append(_softmax_pv(_dot_nt(q, k), v))
    o_ref[...] = jnp.concatenate(outs, axis=-1).astype(o_ref.dtype)


def _dilated_kernel(q_ref, k_ref, v_ref, e_ref, o_ref):
    s_len = q_ref.shape[0]
    n_blk = s_len // DIL_Q_BLOCK
    reach = max((w // (2 * d)) * d for w, d in B_BRANCHES) // DIL_Q_BLOCK
    for hh in range(LANES // HEAD_DIM):
        lanes = slice(hh * HEAD_DIM, (hh + 1) * HEAD_DIM)
        for i in range(n_blk):
            k_lo = max(0, i - reach) * DIL_Q_BLOCK
            k_hi = min(n_blk, i + reach + 1) * DIL_Q_BLOCK
            off = (n_blk - 1 - i) * DIL_Q_BLOCK
            rows = slice(i * DIL_Q_BLOCK, (i + 1) * DIL_Q_BLOCK)
            s = _dot_nt(q_ref[rows, lanes], k_ref[k_lo:k_hi, lanes])
            s = s + e_ref[hh, :, off + k_lo:off + k_hi]
            o = _softmax_pv(s, v_ref[k_lo:k_hi, lanes])
            o_ref[rows, lanes] = o.astype(o_ref.dtype)


def _mlp_kernel(n_mix, final_norm, *refs):
    h_ref = refs[0]
    mix_refs = refs[1:1 + n_mix]
    wo_ref, g_ref, w1_ref, w2_ref = refs[1 + n_mix:5 + n_mix]
    gf_ref = refs[5 + n_mix] if final_norm else None
    out_ref = refs[-1]

    mix = jnp.concatenate([m_ref[...] for m_ref in mix_refs], axis=-1)
    h = h_ref[...] + _dot(mix, wo_ref[...])
    hn = (_rms(h) * g_ref[...]).astype(BF16)
    for c0 in range(0, D_FF, FF_CHUNK):
        a = jnp.maximum(_dot(hn, w1_ref[:, c0:c0 + FF_CHUNK]), 0.0)
        h = h + _dot((a * a).astype(BF16), w2_ref[c0:c0 + FF_CHUNK, :])
    if final_norm:
        h = _rms(h) * gf_ref[...]
    out_ref[...] = h


def _proj1_kernel(h_ref, g_ref, wd_ref, gq_ref, gkv_ref, wuq_ref, wuk_ref, wuv_ref,
                  cq_ref, saq_ref, sbq_ref, ck_ref, sak_ref, sbk_ref,
                  q_ref, k_ref, v_ref):
    xn = (_rms(h_ref[...]) * g_ref[...]).astype(BF16)
    hd = _dot(xn, wd_ref[...])
    c_q = (_rms(hd[:, :C_Q_RANK]) * gq_ref[...]).astype(BF16)
    c_kv = (_rms(hd[:, C_Q_RANK:C_Q_RANK + C_KV_RANK]) * gkv_ref[...]).astype(BF16)
    half = C_ROPE_DIM // 2

    def rope(y, cos, sin_a, sin_b):
        return y * cos + pltpu.roll(y, half, 1) * sin_a + pltpu.roll(y, LANES - half, 1) * sin_b

    cq, saq, sbq = cq_ref[...], saq_ref[...], sbq_ref[...]
    k_rope = rope(hd[:, C_Q_RANK + C_KV_RANK:], ck_ref[...], sak_ref[...], sbk_ref[...])
    for hh in range(C_HEADS):
        lanes = slice(hh * LANES, (hh + 1) * LANES)
        q_ref[:, lanes] = rope(_dot(c_q, wuq_ref[:, lanes]), cq, saq, sbq).astype(BF16)
        k_ref[:, lanes] = (_dot(c_kv, wuk_ref[:, lanes]) + k_rope).astype(BF16)
    v_ref[...] = _dot(c_kv, wuv_ref[...]).astype(BF16)


def _mla_kernel(q_ref, k_ref, v_ref, o_ref):
    s_len = q_ref.shape[0]
    for i in range(s_len // ATTN_Q_TILE):
        rows = slice(i * ATTN_Q_TILE, (i + 1) * ATTN_Q_TILE)
        outs = []
        for hh in range(2):
            lanes = slice(hh * LANES, (hh + 1) * LANES)
            s = _dot_nt(q_ref[rows, lanes], k_ref[:, lanes])
            outs.append(_softmax_pv(s, v_ref[:, hh * C_V_DIM:(hh + 1) * C_V_DIM]))
        o_ref[rows, :] = jnp.concatenate(outs, axis=-1).astype(o_ref.dtype)


def _rope_angles(pos, dim):
    inv_freq = ROPE_THETA ** (-np.arange(0, dim, 2, dtype=np.float64) / dim)
    return pos.astype(np.float64)[:, None] * inv_freq[None, :]


def _axial_tables(s_len):
    t = np.arange(s_len)
    ang = np.concatenate([_rope_angles(t // GRID_W, HEAD_DIM // 2),
                          _rope_angles(t % GRID_W, HEAD_DIM // 2)], axis=-1)
    cos, sin, zero = np.cos(ang), np.sin(ang), np.zeros_like(ang)
    c = np.concatenate([cos, cos, cos, cos], axis=-1)
    sa = np.concatenate([-sin, zero, -sin, zero], axis=-1)
    sb = np.concatenate([zero, sin, zero, sin], axis=-1)
    return [jnp.asarray(a, F32) for a in (c, sa, sb)]


def _mla_tables(s_len, scale):
    ang = _rope_angles(np.arange(s_len), C_ROPE_DIM)
    cos, sin = np.cos(ang), np.sin(ang)
    half = C_ROPE_DIM // 2
    ones = np.ones((s_len, C_NOPE_DIM))
    z_nope = np.zeros((s_len, C_NOPE_DIM))
    z_half = np.zeros((s_len, half))
    z_tail = np.zeros((s_len, LANES - C_QK_DIM))
    c = np.concatenate([ones, cos, cos, z_tail], axis=-1)
    sa = np.concatenate([z_nope, z_half, sin, z_tail], axis=-1)
    sb = np.concatenate([z_nope, -sin, z_half, z_tail], axis=-1)
    return [jnp.asarray(a * scale, F32) for a in (c, sa, sb)]


def _t5_bucket(rel):
    nb = NUM_BUCKETS // 2
    max_exact = nb // 2
    base = np.where(rel > 0, nb, 0)
    n = np.abs(rel)
    nf = np.maximum(n, 1).astype(np.float32)
    large = max_exact + (np.log(nf / np.float32(max_exact)) / np.float32(math.log(REL_MAX_DISTANCE / max_exact))
                         * np.float32(nb - max_exact)).astype(np.int32)
    large = np.minimum(large, nb - 1)
    return base + np.where(n < max_exact, n, large)


def _dilated_structure(s_len):
    n_blk = s_len // DIL_Q_BLOCK
    width = (2 * n_blk - 1) * DIL_Q_BLOCK
    delta = (np.arange(width)[None, :] - (n_blk - 1) * DIL_Q_BLOCK
             - np.arange(DIL_Q_BLOCK)[:, None])
    mult = np.zeros(delta.shape, np.int32)
    for w, d in B_BRANCHES:
        n_side = w // (2 * d)
        mult += ((delta % d == 0) & (np.abs(delta) <= n_side * d)).astype(np.int32)
    log_mult = np.where(mult > 0, np.log(np.maximum(mult, 1).astype(np.float64)), NEG_INF)
    bucket = np.where(mult > 0, _t5_bucket(delta), 0)
    return bucket.astype(np.int32), log_mult.astype(np.float32)


def _params(*sem):
    return pltpu.CompilerParams(dimension_semantics=sem, vmem_limit_bytes=VMEM_LIMIT_BYTES)


def _resident(shape):
    return pl.BlockSpec(shape, lambda *_: (0,) * len(shape), pipeline_mode=pl.Buffered(1))


def _row2(v):
    return v.reshape(1, -1).astype(F32)


def _mlp_call(h, mixes, wo, g, w1, w2, gf, name):
    n_tok = h.shape[0]
    tm = TOKEN_TILE
    tok = lambda w: pl.BlockSpec((tm, w), lambda i: (i, 0))
    in_specs = [tok(D_MODEL)] + [tok(m.shape[1]) for m in mixes] + [
        _resident(wo.shape), _resident((1, D_MODEL)), _resident(w1.shape), _resident(w2.shape)]
    args = [h, *mixes, wo, _row2(g), w1, w2]
    if gf is not None:
        in_specs.append(_resident((1, D_MODEL)))
        args.append(_row2(gf))
    return pl.pallas_call(
        functools.partial(_mlp_kernel, len(mixes), gf is not None),
        out_shape=jax.ShapeDtypeStruct((n_tok, D_MODEL), F32),
        grid=(n_tok // tm,),
        in_specs=in_specs,
        out_specs=tok(D_MODEL),
        compiler_params=_params("parallel"),
        name=name,
    )(*args)


def kernel(x, norm_mix_g, norm_mlp_g, ab_w_in, a_q_norm_g, a_k_norm_g, ab_w_out, rel_bias,
           c_w_down, c_q_norm_g, c_kv_norm_g, c_w_uq, c_w_ukv, c_w_out, mlp_w1, mlp_w2,
           final_norm_g):
    b, s, d = x.shape
    n_tok = b * s
    tm = TOKEN_TILE
    seq_tiles = s // tm
    h0 = x.reshape(n_tok, d)

    tables0 = _axial_tables(s)
    gq = _row2(jnp.tile(a_q_norm_g[0], LANES // HEAD_DIM))
    gk = _row2(jnp.tile(a_k_norm_g[0], LANES // HEAD_DIM))
    tab_spec = pl.BlockSpec((tm, LANES), lambda i: (i % seq_tiles, 0))
    proj = pl.pallas_call(
        _proj0_kernel,
        out_shape=jax.ShapeDtypeStruct((n_tok, AB_IN_W), BF16),
        grid=(n_tok // tm,),
        in_specs=[pl.BlockSpec((tm, d), lambda i: (i, 0)), _resident((1, d)),
                  _resident((d, AB_IN_W)), _resident((1, LANES)), _resident((1, LANES)),
                  tab_spec, tab_spec, tab_spec],
        out_specs=pl.BlockSpec((tm, AB_IN_W), lambda i: (i, 0)),
        compiler_params=_params("parallel"),
        name="proj0",
    )(h0, _row2(norm_mix_g[0]), ab_w_in[0].astype(BF16), gq, gk, *tables0)

    tq = ATTN_Q_TILE
    q_tiles = s // tq
    o_a = pl.pallas_call(
        _gqa_kernel,
        out_shape=jax.ShapeDtypeStruct((n_tok, A_Q_W), BF16),
        grid=(b, q_tiles),
        in_specs=[pl.BlockSpec((tq, A_Q_W), lambda bi, qi: (bi * q_tiles + qi, COL_QA // A_Q_W)),
                  pl.BlockSpec((s, A_KV_W), lambda bi, qi: (bi, COL_KA // A_KV_W)),
                  pl.BlockSpec((s, A_KV_W), lambda bi, qi: (bi, COL_VA // A_KV_W))],
        out_specs=pl.BlockSpec((tq, A_Q_W), lambda bi, qi: (bi * q_tiles + qi, 0)),
        compiler_params=_params("parallel", "parallel"),
        name="gqa",
    )(proj, proj, proj)

    bucket, log_mult = _dilated_structure(s)
    e_tab = jnp.take(rel_bias.astype(F32).T, jnp.asarray(bucket), axis=1) + jnp.asarray(log_mult)
    pairs = B_W // LANES
    heads_per_pair = LANES // HEAD_DIM
    o_b = pl.pallas_call(
        _dilated_kernel,
        out_shape=jax.ShapeDtypeStruct((n_tok, B_W), BF16),
        grid=(pairs, b),
        in_specs=[pl.BlockSpec((s, LANES), lambda p, bi: (bi, COL_QB // LANES + p)),
                  pl.BlockSpec((s, LANES), lambda p, bi: (bi, COL_KB // LANES + p)),
                  pl.BlockSpec((s, LANES), lambda p, bi: (bi, COL_VB // LANES + p)),
                  pl.BlockSpec((heads_per_pair,) + e_tab.shape[1:], lambda p, bi: (p, 0, 0))],
        out_specs=pl.BlockSpec((s, LANES), lambda p, bi: (bi, p)),
        compiler_params=_params("parallel", "parallel"),
        name="dilated",
    )(proj, proj, proj, e_tab)

    h1 = _mlp_call(h0, [o_a, o_b], ab_w_out[0].astype(BF16), norm_mlp_g[0],
                   mlp_w1[0].astype(BF16), mlp_w2[0].astype(BF16), None, "mix0_mlp0")

    scale_c = C_QK_DIM ** -0.5
    tables_q = _mla_tables(s, scale_c)
    tables_k = _mla_tables(s, 1.0)
    zeros = lambda r, c: jnp.zeros((r, c), F32)
    wd = c_w_down[0]
    kv_end = C_Q_RANK + C_KV_RANK
    wd_p = jnp.concatenate([wd[:, :kv_end], zeros(d, C_NOPE_DIM), wd[:, kv_end:],
                            zeros(d, LANES - C_QK_DIM)], axis=1).astype(BF16)
    wuq = c_w_uq[0].reshape(C_Q_RANK, C_HEADS, C_QK_DIM)
    wuq_p = jnp.pad(wuq, ((0, 0), (0, 0), (0, LANES - C_QK_DIM))).reshape(
        C_Q_RANK, C_HEADS * LANES).astype(BF16)
    wukv = c_w_ukv[0].reshape(C_KV_RANK, C_HEADS, C_NOPE_DIM + C_V_DIM)
    wuk_p = jnp.pad(wukv[:, :, :C_NOPE_DIM], ((0, 0), (0, 0), (0, LANES - C_NOPE_DIM))).reshape(
        C_KV_RANK, C_HEADS * LANES).astype(BF16)
    wuv_p = wukv[:, :, C_NOPE_DIM:].reshape(C_KV_RANK, C_HEADS * C_V_DIM).astype(BF16)

    qk_w = C_HEADS * LANES
    v_w = C_HEADS * C_V_DIM
    tok = lambda w: pl.BlockSpec((tm, w), lambda i: (i, 0))
    q_c, k_c, v_c = pl.pallas_call(
        _proj1_kernel,
        out_shape=(jax.ShapeDtypeStruct((n_tok, qk_w), BF16),
                   jax.ShapeDtypeStruct((n_tok, qk_w), BF16),
                   jax.ShapeDtypeStruct((n_tok, v_w), BF16)),
        grid=(n_tok // tm,),
        in_specs=[tok(d), _resident((1, d)), _resident(wd_p.shape),
                  _resident((1, C_Q_RANK)), _resident((1, C_KV_RANK)),
                  _resident(wuq_p.shape), _resident(wuk_p.shape), _resident(wuv_p.shape)]
                 + [tab_spec] * 6,
        out_specs=(tok(qk_w), tok(qk_w), tok(v_w)),
        compiler_params=_params("parallel"),
        name="proj1",
    )(h1, _row2(norm_mix_g[1]), wd_p, _row2(c_q_norm_g[0]), _row2(c_kv_norm_g[0]),
      wuq_p, wuk_p, wuv_p, *tables_q, *tables_k)

    o_c = pl.pallas_call(
        _mla_kernel,
        out_shape=jax.ShapeDtypeStruct((n_tok, v_w), BF16),
        grid=(b, C_HEADS // 2),
        in_specs=[pl.BlockSpec((s, 2 * LANES), lambda bi, p: (bi, p)),
                  pl.BlockSpec((s, 2 * LANES), lambda bi, p: (bi, p)),
                  pl.BlockSpec((s, 2 * C_V_DIM), lambda bi, p: (bi, p))],
        out_specs=pl.BlockSpec((s, 2 * C_V_DIM), lambda bi, p: (bi, p)),
        compiler_params=_params("parallel", "parallel"),
        name="mla",
    )(q_c, k_c, v_c)

    out = _mlp_call(h1, [o_c], c_w_out[0].astype(BF16), norm_mlp_g[1],
                    mlp_w1[1].astype(BF16), mlp_w2[1].astype(BF16), final_norm_g, "mix1_mlp1")
    return out.reshape(b, s, d)
```

```python
import functools
import math

import jax
import jax.numpy as jnp
import numpy as np
from jax import lax
from jax.experimental import pallas as pl
from jax.experimental.pallas import tpu as pltpu

F32 = jnp.float32
BF16 = jnp.bfloat16

D_MODEL = 1024
GRID_W = 64
HEAD_DIM = 64
ROPE_THETA = 10000.0
EPS = 1e-6
NEG_INF = -1e30

A_HEADS = 8
A_KV_HEADS = 2
B_HEADS = 8
B_BRANCHES = ((128, 1), (512, 4), (2048, 16))
NUM_BUCKETS = 32
REL_MAX_DISTANCE = 1024

C_HEADS = 16
C_Q_RANK = 256
C_KV_RANK = 128
C_NOPE_DIM = 64
C_ROPE_DIM = 32
C_V_DIM = 64
C_QK_DIM = C_NOPE_DIM + C_ROPE_DIM
D_FF = 4 * D_MODEL

A_Q_W = A_HEADS * HEAD_DIM
A_KV_W = A_KV_HEADS * HEAD_DIM
B_W = B_HEADS * HEAD_DIM
AB_IN_W = A_Q_W + 2 * A_KV_W + 3 * B_W

LANES = 128
VMEM_LIMIT_BYTES = 56 * 1024 * 1024

TOKEN_TILE = 512
ATTN_Q_TILE = 256
DIL_Q_BLOCK = 128
FF_CHUNK = 1024

COL_QA = 0
COL_KA = A_Q_W
COL_VA = COL_KA + A_KV_W
COL_QB = COL_VA + A_KV_W
COL_KB = COL_QB + B_W
COL_VB = COL_KB + B_W


def _rms(x):
    return x * lax.rsqrt(jnp.mean(x * x, axis=-1, keepdims=True) + EPS)


def _dot(a, b):
    return jnp.dot(a, b, preferred_element_type=F32)


def _dot_nt(a, b):
    return lax.dot_general(a, b, (((1,), (1,)), ((), ())), preferred_element_type=F32)


def _softmax_pv(s, v):
    m = jnp.max(s, axis=-1, keepdims=True)
    p = jnp.exp(s - m)
    l = jnp.sum(p, axis=-1, keepdims=True)
    return _dot(p.astype(BF16), v) / l


def _proj0_kernel(x_ref, g_ref, w_ref, gq_ref, gk_ref, c_ref, sa_ref, sb_ref, o_ref):
    tm = x_ref.shape[0]
    xn = (_rms(x_ref[...]) * g_ref[...]).astype(BF16)
    low = lax.broadcasted_iota(jnp.int32, (tm, LANES), 1) < HEAD_DIM
    cos, sin_a, sin_b = c_ref[...], sa_ref[...], sb_ref[...]

    def norm_rope(y, gain):
        sq = y * y
        s_all = jnp.sum(sq, axis=-1, keepdims=True)
        s_low = jnp.sum(jnp.where(low, sq, 0.0), axis=-1, keepdims=True)
        ms = jnp.where(low, s_low, s_all - s_low) * (1.0 / HEAD_DIM)
        y = y * lax.rsqrt(ms + EPS) * gain
        return (y * cos + pltpu.roll(y, LANES - HEAD_DIM // 2, 1) * sin_a
                + pltpu.roll(y, HEAD_DIM // 2, 1) * sin_b)

    scale = HEAD_DIM ** -0.5
    for c0 in range(COL_QA, COL_KA, LANES):
        y = _dot(xn, w_ref[:, c0:c0 + LANES])
        o_ref[:, c0:c0 + LANES] = (norm_rope(y, gq_ref[...]) * scale).astype(BF16)
    y = _dot(xn, w_ref[:, COL_KA:COL_VA])
    o_ref[:, COL_KA:COL_VA] = norm_rope(y, gk_ref[...]).astype(BF16)
    o_ref[:, COL_VA:COL_QB] = _dot(xn, w_ref[:, COL_VA:COL_QB]).astype(BF16)
    o_ref[:, COL_QB:COL_KB] = (_dot(xn, w_ref[:, COL_QB:COL_KB]) * scale).astype(BF16)
    o_ref[:, COL_KB:] = _dot(xn, w_ref[:, COL_KB:]).astype(BF16)


def _gqa_kernel(q_ref, k_ref, v_ref, o_ref):
    rep = A_HEADS // A_KV_HEADS
    outs = []
    for g in range(A_KV_HEADS):
        k = k_ref[:, g * HEAD_DIM:(g + 1) * HEAD_DIM]
        v = v_ref[:, g * HEAD_DIM:(g + 1) * HEAD_DIM]
        for r in range(rep):
            h = g * rep + r
            q = q_ref[:, h * HEAD_DIM:(h + 1) * HEAD_DIM]
            outs.append(_softmax_pv(_dot_nt(q, k), v))
    o_ref[...] = jnp.concatenate(outs, axis=-1).astype(o_ref.dtype)


def _dilated_kernel(q_ref, k_ref, v_ref, rb_ref, onehot_ref, logm_ref, o_ref, e_ref):
    s_len = q_ref.shape[0]
    n_blk = s_len // DIL_Q_BLOCK
    reach = max((w // (2 * d)) * d for w, d in B_BRANCHES) // DIL_Q_BLOCK
    heads = LANES // HEAD_DIM
    e_width = (2 * n_blk - 1) * DIL_Q_BLOCK

    @pl.when(pl.program_id(1) == 0)
    def _():
        for hh in range(heads):
            per_delta = jnp.sum(onehot_ref[...] * rb_ref[:, hh:hh + 1], axis=0, keepdims=True)
            per_delta = per_delta + logm_ref[...]
            spread = jnp.broadcast_to(per_delta, (DIL_Q_BLOCK, 2 * s_len))
            toeplitz = pltpu.roll(spread, 0, 1, stride=1, stride_axis=0)
            e_ref[hh] = toeplitz[:, DIL_Q_BLOCK:DIL_Q_BLOCK + e_width]

    for hh in range(heads):
        lanes = slice(hh * HEAD_DIM, (hh + 1) * HEAD_DIM)
        for i in range(n_blk):
            k_lo = max(0, i - reach) * DIL_Q_BLOCK
            k_hi = min(n_blk, i + reach + 1) * DIL_Q_BLOCK
            off = (n_blk - 1 - i) * DIL_Q_BLOCK
            rows = slice(i * DIL_Q_BLOCK, (i + 1) * DIL_Q_BLOCK)
            s = _dot_nt(q_ref[rows, lanes], k_ref[k_lo:k_hi, lanes])
            s = s + e_ref[hh, :, off + k_lo:off + k_hi]
            o = _softmax_pv(s, v_ref[k_lo:k_hi, lanes])
            o_ref[rows, lanes] = o.astype(o_ref.dtype)


def _mlp_kernel(n_mix, final_norm, *refs):
    h_ref = refs[0]
    mix_refs = refs[1:1 + n_mix]
    wo_ref, g_ref, w1_ref, w2_ref = refs[1 + n_mix:5 + n_mix]
    gf_ref = refs[5 + n_mix] if final_norm else None
    out_ref = refs[-1]

    mix = jnp.concatenate([m_ref[...] for m_ref in mix_refs], axis=-1)
    h = h_ref[...] + _dot(mix, wo_ref[...])
    hn = (_rms(h) * g_ref[...]).astype(BF16)
    for c0 in range(0, D_FF, FF_CHUNK):
        a = jnp.maximum(_dot(hn, w1_ref[:, c0:c0 + FF_CHUNK]), 0.0)
        h = h + _dot((a * a).astype(BF16), w2_ref[c0:c0 + FF_CHUNK, :])
    if final_norm:
        h = _rms(h) * gf_ref[...]
    out_ref[...] = h


def _proj1_kernel(h_ref, g_ref, wd_ref, gq_ref, gkv_ref, wuq_ref, wuk_ref, wuv_ref,
                  cq_ref, saq_ref, sbq_ref, ck_ref, sak_ref, sbk_ref,
                  q_ref, k_ref, v_ref):
    xn = (_rms(h_ref[...]) * g_ref[...]).astype(BF16)
    hd = _dot(xn, wd_ref[...])
    c_q = (_rms(hd[:, :C_Q_RANK]) * gq_ref[...]).astype(BF16)
    c_kv = (_rms(hd[:, C_Q_RANK:C_Q_RANK + C_KV_RANK]) * gkv_ref[...]).astype(BF16)
    half = C_ROPE_DIM // 2

    def rope(y, cos, sin_a, sin_b):
        return y * cos + pltpu.roll(y, half, 1) * sin_a + pltpu.roll(y, LANES - half, 1) * sin_b

    cq, saq, sbq = cq_ref[...], saq_ref[...], sbq_ref[...]
    k_rope = rope(hd[:, C_Q_RANK + C_KV_RANK:], ck_ref[...], sak_ref[...], sbk_ref[...])
    for hh in range(C_HEADS):
        lanes = slice(hh * LANES, (hh + 1) * LANES)
        q_ref[:, lanes] = rope(_dot(c_q, wuq_ref[:, lanes]), cq, saq, sbq).astype(BF16)
        k_ref[:, lanes] = (_dot(c_kv, wuk_ref[:, lanes]) + k_rope).astype(BF16)
    v_ref[...] = _dot(c_kv, wuv_ref[...]).astype(BF16)


def _mla_kernel(q_ref, k_ref, v_ref, o_ref):
    s_len = q_ref.shape[0]
    for i in range(s_len // ATTN_Q_TILE):
        rows = slice(i * ATTN_Q_TILE, (i + 1) * ATTN_Q_TILE)
        outs = []
        for hh in range(2):
            lanes = slice(hh * LANES, (hh + 1) * LANES)
            s = _dot_nt(q_ref[rows, lanes], k_ref[:, lanes])
            outs.append(_softmax_pv(s, v_ref[:, hh * C_V_DIM:(hh + 1) * C_V_DIM]))
        o_ref[rows, :] = jnp.concatenate(outs, axis=-1).astype(o_ref.dtype)


def _rope_angles(pos, dim):
    inv_freq = ROPE_THETA ** (-np.arange(0, dim, 2, dtype=np.float64) / dim)
    return pos.astype(np.float64)[:, None] * inv_freq[None, :]


def _axial_tables(s_len):
    t = np.arange(s_len)
    ang = np.concatenate([_rope_angles(t // GRID_W, HEAD_DIM // 2),
                          _rope_angles(t % GRID_W, HEAD_DIM // 2)], axis=-1)
    cos, sin, zero = np.cos(ang), np.sin(ang), np.zeros_like(ang)
    c = np.concatenate([cos, cos, cos, cos], axis=-1)
    sa = np.concatenate([-sin, zero, -sin, zero], axis=-1)
    sb = np.concatenate([zero, sin, zero, sin], axis=-1)
    return [jnp.asarray(a, F32) for a in (c, sa, sb)]


def _mla_tables(s_len, scale):
    ang = _rope_angles(np.arange(s_len), C_ROPE_DIM)
    cos, sin = np.cos(ang), np.sin(ang)
    half = C_ROPE_DIM // 2
    ones = np.ones((s_len, C_NOPE_DIM))
    z_nope = np.zeros((s_len, C_NOPE_DIM))
    z_half = np.zeros((s_len, half))
    z_tail = np.zeros((s_len, LANES - C_QK_DIM))
    c = np.concatenate([ones, cos, cos, z_tail], axis=-1)
    sa = np.concatenate([z_nope, z_half, sin, z_tail], axis=-1)
    sb = np.concatenate([z_nope, -sin, z_half, z_tail], axis=-1)
    return [jnp.asarray(a * scale, F32) for a in (c, sa, sb)]


def _t5_bucket(rel):
    nb = NUM_BUCKETS // 2
    max_exact = nb // 2
    base = np.where(rel > 0, nb, 0)
    n = np.abs(rel)
    nf = np.maximum(n, 1).astype(np.float32)
    large = max_exact + (np.log(nf / np.float32(max_exact)) / np.float32(math.log(REL_MAX_DISTANCE / max_exact))
                         * np.float32(nb - max_exact)).astype(np.int32)
    large = np.minimum(large, nb - 1)
    return base + np.where(n < max_exact, n, large)


def _dilated_structure(s_len):
    delta = np.arange(2 * s_len) - s_len
    mult = np.zeros(delta.shape, np.int32)
    for w, d in B_BRANCHES:
        n_side = w // (2 * d)
        mult += ((delta % d == 0) & (np.abs(delta) <= n_side * d)).astype(np.int32)
    log_mult = np.where(mult > 0, np.log(np.maximum(mult, 1).astype(np.float64)), NEG_INF)
    onehot = (_t5_bucket(delta)[None, :] == np.arange(NUM_BUCKETS)[:, None]) & (mult > 0)[None, :]
    return jnp.asarray(onehot, F32), jnp.asarray(log_mult[None, :], F32)


def _params(*sem):
    return pltpu.CompilerParams(dimension_semantics=sem, vmem_limit_bytes=VMEM_LIMIT_BYTES)


def _resident(shape):
    return pl.BlockSpec(shape, lambda *_: (0,) * len(shape), pipeline_mode=pl.Buffered(1))


def _row2(v):
    return v.reshape(1, -1).astype(F32)


def _mlp_call(h, mixes, wo, g, w1, w2, gf, name):
    n_tok = h.shape[0]
    tm = TOKEN_TILE
    tok = lambda w: pl.BlockSpec((tm, w), lambda i: (i, 0))
    in_specs = [tok(D_MODEL)] + [tok(m.shape[1]) for m in mixes] + [
        _resident(wo.shape), _resident((1, D_MODEL)), _resident(w1.shape), _resident(w2.shape)]
    args = [h, *mixes, wo, _row2(g), w1, w2]
    if gf is not None:
        in_specs.append(_resident((1, D_MODEL)))
        args.append(_row2(gf))
    return pl.pallas_call(
        functools.partial(_mlp_kernel, len(mixes), gf is not None),
        out_shape=jax.ShapeDtypeStruct((n_tok, D_MODEL), F32),
        grid=(n_tok // tm,),
        in_specs=in_specs,
        out_specs=tok(D_MODEL),
        compiler_params=_params("parallel"),
        name=name,
    )(*args)


def kernel(x, norm_mix_g, norm_mlp_g, ab_w_in, a_q_norm_g, a_k_norm_g, ab_w_out, rel_bias,
           c_w_down, c_q_norm_g, c_kv_norm_g, c_w_uq, c_w_ukv, c_w_out, mlp_w1, mlp_w2,
           final_norm_g):
    b, s, d = x.shape
    n_tok = b * s
    tm = TOKEN_TILE
    seq_tiles = s // tm
    h0 = x.reshape(n_tok, d)

    tables0 = _axial_tables(s)
    gq = _row2(jnp.tile(a_q_norm_g[0], LANES // HEAD_DIM))
    gk = _row2(jnp.tile(a_k_norm_g[0], LANES // HEAD_DIM))
    tab_spec = pl.BlockSpec((tm, LANES), lambda i: (i % seq_tiles, 0))
    proj = pl.pallas_call(
        _proj0_kernel,
        out_shape=jax.ShapeDtypeStruct((n_tok, AB_IN_W), BF16),
        grid=(n_tok // tm,),
        in_specs=[pl.BlockSpec((tm, d), lambda i: (i, 0)), _resident((1, d)),
                  _resident((d, AB_IN_W)), _resident((1, LANES)), _resident((1, LANES)),
                  tab_spec, tab_spec, tab_spec],
        out_specs=pl.BlockSpec((tm, AB_IN_W), lambda i: (i, 0)),
        compiler_params=_params("parallel"),
        name="proj0",
    )(h0, _row2(norm_mix_g[0]), ab_w_in[0].astype(BF16), gq, gk, *tables0)

    tq = ATTN_Q_TILE
    q_tiles = s // tq
    o_a = pl.pallas_call(
        _gqa_kernel,
        out_shape=jax.ShapeDtypeStruct((n_tok, A_Q_W), BF16),
        grid=(b, q_tiles),
        in_specs=[pl.BlockSpec((tq, A_Q_W), lambda bi, qi: (bi * q_tiles + qi, COL_QA // A_Q_W)),
                  pl.BlockSpec((s, A_KV_W), lambda bi, qi: (bi, COL_KA // A_KV_W)),
                  pl.BlockSpec((s, A_KV_W), lambda bi, qi: (bi, COL_VA // A_KV_W))],
        out_specs=pl.BlockSpec((tq, A_Q_W), lambda bi, qi: (bi * q_tiles + qi, 0)),
        compiler_params=_params("parallel", "parallel"),
        name="gqa",
    )(proj, proj, proj)

    onehot, log_mult = _dilated_structure(s)
    pairs = B_W // LANES
    heads_per_pair = LANES // HEAD_DIM
    n_blk = s // DIL_Q_BLOCK
    rb = rel_bias.astype(F32).reshape(NUM_BUCKETS, pairs, heads_per_pair).transpose(1, 0, 2)
    o_b = pl.pallas_call(
        _dilated_kernel,
        out_shape=jax.ShapeDtypeStruct((n_tok, B_W), BF16),
        grid=(pairs, b),
        in_specs=[pl.BlockSpec((s, LANES), lambda p, bi: (bi, COL_QB // LANES + p)),
                  pl.BlockSpec((s, LANES), lambda p, bi: (bi, COL_KB // LANES + p)),
                  pl.BlockSpec((s, LANES), lambda p, bi: (bi, COL_VB // LANES + p)),
                  pl.BlockSpec((None, NUM_BUCKETS, heads_per_pair), lambda p, bi: (p, 0, 0)),
                  _resident(onehot.shape), _resident(log_mult.shape)],
        out_specs=pl.BlockSpec((s, LANES), lambda p, bi: (bi, p)),
        scratch_shapes=[pltpu.VMEM((heads_per_pair, DIL_Q_BLOCK, (2 * n_blk - 1) * DIL_Q_BLOCK), F32)],
        compiler_params=_params("parallel", "arbitrary"),
        name="dilated",
    )(proj, proj, proj, rb, onehot, log_mult)

    h1 = _mlp_call(h0, [o_a, o_b], ab_w_out[0].astype(BF16), norm_mlp_g[0],
                   mlp_w1[0].astype(BF16), mlp_w2[0].astype(BF16), None, "mix0_mlp0")

    scale_c = C_QK_DIM ** -0.5
    tables_q = _mla_tables(s, scale_c)
    tables_k = _mla_tables(s, 1.0)
    zeros = lambda r, c: jnp.zeros((r, c), F32)
    wd = c_w_down[0]
    kv_end = C_Q_RANK + C_KV_RANK
    wd_p = jnp.concatenate([wd[:, :kv_end], zeros(d, C_NOPE_DIM), wd[:, kv_end:],
                            zeros(d, LANES - C_QK_DIM)], axis=1).astype(BF16)
    wuq = c_w_uq[0].reshape(C_Q_RANK, C_HEADS, C_QK_DIM)
    wuq_p = jnp.pad(wuq, ((0, 0), (0, 0), (0, LANES - C_QK_DIM))).reshape(
        C_Q_RANK, C_HEADS * LANES).astype(BF16)
    wukv = c_w_ukv[0].reshape(C_KV_RANK, C_HEADS, C_NOPE_DIM + C_V_DIM)
    wuk_p = jnp.pad(wukv[:, :, :C_NOPE_DIM], ((0, 0), (0, 0), (0, LANES - C_NOPE_DIM))).reshape(
        C_KV_RANK, C_HEADS * LANES).astype(BF16)
    wuv_p = wukv[:, :, C_NOPE_DIM:].reshape(C_KV_RANK, C_HEADS * C_V_DIM).astype(BF16)

    qk_w = C_HEADS * LANES
    v_w = C_HEADS * C_V_DIM
    tok = lambda w: pl.BlockSpec((tm, w), lambda i: (i, 0))
    q_c, k_c, v_c = pl.pallas_call(
        _proj1_kernel,
        out_shape=(jax.ShapeDtypeStruct((n_tok, qk_w), BF16),
                   jax.ShapeDtypeStruct((n_tok, qk_w), BF16),
                   jax.ShapeDtypeStruct((n_tok, v_w), BF16)),
        grid=(n_tok // tm,),
        in_specs=[tok(d), _resident((1, d)), _resident(wd_p.shape),
                  _resident((1, C_Q_RANK)), _resident((1, C_KV_RANK)),
                  _resident(wuq_p.shape), _resident(wuk_p.shape), _resident(wuv_p.shape)]
                 + [tab_spec] * 6,
        out_specs=(tok(qk_w), tok(qk_w), tok(v_w)),
        compiler_params=_params("parallel"),
        name="proj1",
    )(h1, _row2(norm_mix_g[1]), wd_p, _row2(c_q_norm_g[0]), _row2(c_kv_norm_g[0]),
      wuq_p, wuk_p, wuv_p, *tables_q, *tables_k)

    o_c = pl.pallas_call(
        _mla_kernel,
        out_shape=jax.ShapeDtypeStruct((n_tok, v_w), BF16),
        grid=(b, C_HEADS // 2),
        in_specs=[pl.BlockSpec((s, 2 * LANES), lambda bi, p: (bi, p)),
                  pl.BlockSpec((s, 2 * LANES), lambda bi, p: (bi, p)),
                  pl.BlockSpec((s, 2 * C_V_DIM), lambda bi, p: (bi, p))],
        out_specs=pl.BlockSpec((s, 2 * C_V_DIM), lambda bi, p: (bi, p)),
        compiler_params=_params("parallel", "parallel"),
        name="mla",
    )(q_c, k_c, v_c)

    out = _mlp_call(h1, [o_c], c_w_out[0].astype(BF16), norm_mlp_g[1],
                    mlp_w1[1].astype(BF16), mlp_w2[1].astype(BF16), final_norm_g, "mix1_mlp1")
    return out.reshape(b, s, d)
```

```python
import functools
import math

import jax
import jax.numpy as jnp
import numpy as np
from jax import lax
from jax.experimental import pallas as pl
from jax.experimental.pallas import tpu as pltpu

F32 = jnp.float32
BF16 = jnp.bfloat16

D_MODEL = 1024
GRID_W = 64
HEAD_DIM = 64
ROPE_THETA = 10000.0
EPS = 1e-6
NEG_INF = -1e30
LOG2E = math.log2(math.e)

A_HEADS = 8
A_KV_HEADS = 2
B_HEADS = 8
B_BRANCHES = ((128, 1), (512, 4), (2048, 16))
NUM_BUCKETS = 32
REL_MAX_DISTANCE = 1024

C_HEADS = 16
C_Q_RANK = 256
C_KV_RANK = 128
C_NOPE_DIM = 64
C_ROPE_DIM = 32
C_V_DIM = 64
C_QK_DIM = C_NOPE_DIM + C_ROPE_DIM
D_FF = 4 * D_MODEL

A_Q_W = A_HEADS * HEAD_DIM
A_KV_W = A_KV_HEADS * HEAD_DIM
B_W = B_HEADS * HEAD_DIM
AB_IN_W = A_Q_W + 2 * A_KV_W + 3 * B_W

LANES = 128
VMEM_LIMIT_BYTES = 56 * 1024 * 1024

TOKEN_TILE = 512
ATTN_Q_TILE = 256
DIL_Q_BLOCK = 128
FF_CHUNK = 1024

COL_QA = 0
COL_KA = A_Q_W
COL_VA = COL_KA + A_KV_W
COL_QB = COL_VA + A_KV_W
COL_KB = COL_QB + B_W
COL_VB = COL_KB + B_W


def _rms(x):
    return x * lax.rsqrt(jnp.mean(x * x, axis=-1, keepdims=True) + EPS)


def _dot(a, b):
    return jnp.dot(a, b, preferred_element_type=F32)


def _dot_nt(a, b):
    return lax.dot_general(a, b, (((1,), (1,)), ((), ())), preferred_element_type=F32)


def _with_ones(v):
    return jnp.concatenate([v, jnp.ones_like(v)], axis=-1)


def _softmax_pv(s, v_ones):
    dv = v_ones.shape[1] // 2
    m = jnp.max(s, axis=-1, keepdims=True)
    p = jnp.exp2(s - m).astype(BF16)
    acc = _dot(p, v_ones)
    return acc[:, :dv] / acc[:, dv:dv + 1]


def _proj0_kernel(x_ref, g_ref, w_ref, gq_ref, gk_ref, c_ref, sa_ref, sb_ref, o_ref):
    tm = x_ref.shape[0]
    xn = (_rms(x_ref[...]) * g_ref[...]).astype(BF16)
    low = lax.broadcasted_iota(jnp.int32, (tm, LANES), 1) < HEAD_DIM
    cos, sin_a, sin_b = c_ref[...], sa_ref[...], sb_ref[...]

    def norm_rope(y, gain):
        sq = y * y
        s_all = jnp.sum(sq, axis=-1, keepdims=True)
        s_low = jnp.sum(jnp.where(low, sq, 0.0), axis=-1, keepdims=True)
        ms = jnp.where(low, s_low, s_all - s_low) * (1.0 / HEAD_DIM)
        y = y * lax.rsqrt(ms + EPS) * gain
        return (y * cos + pltpu.roll(y, LANES - HEAD_DIM // 2, 1) * sin_a
                + pltpu.roll(y, HEAD_DIM // 2, 1) * sin_b)

    scale = HEAD_DIM ** -0.5 * LOG2E
    for c0 in range(COL_QA, COL_KA, LANES):
        y = _dot(xn, w_ref[:, c0:c0 + LANES])
        o_ref[:, c0:c0 + LANES] = (norm_rope(y, gq_ref[...]) * scale).astype(BF16)
    y = _dot(xn, w_ref[:, COL_KA:COL_VA])
    o_ref[:, COL_KA:COL_VA] = norm_rope(y, gk_ref[...]).astype(BF16)
    o_ref[:, COL_VA:COL_QB] = _dot(xn, w_ref[:, COL_VA:COL_QB]).astype(BF16)
    o_ref[:, COL_QB:COL_KB] = (_dot(xn, w_ref[:, COL_QB:COL_KB]) * scale).astype(BF16)
    o_ref[:, COL_KB:] = _dot(xn, w_ref[:, COL_KB:]).astype(BF16)


def _gqa_kernel(q_ref, k_ref, v_ref, o_ref):
    rep = A_HEADS // A_KV_HEADS
    outs = []
    for g in range(A_KV_HEADS):
        k = k_ref[:, g * HEAD_DIM:(g + 1) * HEAD_DIM]
        v = _with_ones(v_ref[:, g * HEAD_DIM:(g + 1) * HEAD_DIM])
        for r in range(rep):
            h = g * rep + r
            q = q_ref[:, h * HEAD_DIM:(h + 1) * HEAD_DIM]
            outs.append(_softmax_pv(_dot_nt(q, k), v))
    o_ref[...] = jnp.concatenate(outs, axis=-1).astype(o_ref.dtype)


def _dilated_kernel(q_ref, k_ref, v_ref, rb_ref, onehot_ref, logm_ref, o_ref, e_ref):
    s_len = q_ref.shape[0]
    n_blk = s_len // DIL_Q_BLOCK
    reach = max((w // (2 * d)) * d for w, d in B_BRANCHES) // DIL_Q_BLOCK
    heads = LANES // HEAD_DIM
    e_width = (2 * n_blk - 1) * DIL_Q_BLOCK

    @pl.when(pl.program_id(1) == 0)
    def _():
        for hh in range(heads):
            per_delta = jnp.sum(onehot_ref[...] * rb_ref[:, hh:hh + 1], axis=0, keepdims=True)
            per_delta = (per_delta + logm_ref[...]) * LOG2E
            spread = jnp.broadcast_to(per_delta, (DIL_Q_BLOCK, 2 * s_len))
            toeplitz = pltpu.roll(spread, 0, 1, stride=1, stride_axis=0)
            e_ref[hh] = toeplitz[:, DIL_Q_BLOCK:DIL_Q_BLOCK + e_width]

    for hh in range(heads):
        lanes = slice(hh * HEAD_DIM, (hh + 1) * HEAD_DIM)
        v_ones = _with_ones(v_ref[:, lanes])
        for i in range(n_blk):
            k_lo = max(0, i - reach) * DIL_Q_BLOCK
            k_hi = min(n_blk, i + reach + 1) * DIL_Q_BLOCK
            off = (n_blk - 1 - i) * DIL_Q_BLOCK
            rows = slice(i * DIL_Q_BLOCK, (i + 1) * DIL_Q_BLOCK)
            s = _dot_nt(q_ref[rows, lanes], k_ref[k_lo:k_hi, lanes])
            s = s + e_ref[hh, :, off + k_lo:off + k_hi]
            o = _softmax_pv(s, v_ones[k_lo:k_hi])
            o_ref[rows, lanes] = o.astype(o_ref.dtype)


def _mlp_kernel(n_mix, final_norm, *refs):
    h_ref = refs[0]
    mix_refs = refs[1:1 + n_mix]
    wo_ref, g_ref, w1_ref, w2_ref = refs[1 + n_mix:5 + n_mix]
    gf_ref = refs[5 + n_mix] if final_norm else None
    out_ref = refs[-1]

    mix = jnp.concatenate([m_ref[...] for m_ref in mix_refs], axis=-1)
    h = h_ref[...] + _dot(mix, wo_ref[...])
    hn = (_rms(h) * g_ref[...]).astype(BF16)
    for c0 in range(0, D_FF, FF_CHUNK):
        a = jnp.maximum(_dot(hn, w1_ref[:, c0:c0 + FF_CHUNK]), 0.0)
        h = h + _dot((a * a).astype(BF16), w2_ref[c0:c0 + FF_CHUNK, :])
    if final_norm:
        h = _rms(h) * gf_ref[...]
    out_ref[...] = h


def _proj1_kernel(h_ref, g_ref, wd_ref, gq_ref, gkv_ref, wuq_ref, wuk_ref, wuv_ref,
                  cq_ref, saq_ref, sbq_ref, ck_ref, sak_ref, sbk_ref,
                  q_ref, k_ref, v_ref):
    xn = (_rms(h_ref[...]) * g_ref[...]).astype(BF16)
    hd = _dot(xn, wd_ref[...])
    c_q = (_rms(hd[:, :C_Q_RANK]) * gq_ref[...]).astype(BF16)
    c_kv = (_rms(hd[:, C_Q_RANK:C_Q_RANK + C_KV_RANK]) * gkv_ref[...]).astype(BF16)
    half = C_ROPE_DIM // 2

    def rope(y, cos, sin_a, sin_b):
        return y * cos + pltpu.roll(y, half, 1) * sin_a + pltpu.roll(y, LANES - half, 1) * sin_b

    cq, saq, sbq = cq_ref[...], saq_ref[...], sbq_ref[...]
    k_rope = rope(hd[:, C_Q_RANK + C_KV_RANK:], ck_ref[...], sak_ref[...], sbk_ref[...])
    for hh in range(C_HEADS):
        lanes = slice(hh * LANES, (hh + 1) * LANES)
        q_ref[:, lanes] = rope(_dot(c_q, wuq_ref[:, lanes]), cq, saq, sbq).astype(BF16)
        k_ref[:, lanes] = (_dot(c_kv, wuk_ref[:, lanes]) + k_rope).astype(BF16)
    v_ref[...] = _dot(c_kv, wuv_ref[...]).astype(BF16)


def _mla_kernel(q_ref, k_ref, v_ref, o_ref):
    s_len = q_ref.shape[0]
    for hh in range(2):
        lanes = slice(hh * LANES, (hh + 1) * LANES)
        v_lanes = slice(hh * C_V_DIM, (hh + 1) * C_V_DIM)
        v_ones = _with_ones(v_ref[:, v_lanes])
        for i in range(s_len // ATTN_Q_TILE):
            rows = slice(i * ATTN_Q_TILE, (i + 1) * ATTN_Q_TILE)
            s = _dot_nt(q_ref[rows, lanes], k_ref[:, lanes])
            o_ref[rows, v_lanes] = _softmax_pv(s, v_ones).astype(o_ref.dtype)


def _rope_angles(pos, dim):
    inv_freq = ROPE_THETA ** (-np.arange(0, dim, 2, dtype=np.float64) / dim)
    return pos.astype(np.float64)[:, None] * inv_freq[None, :]


def _axial_tables(s_len):
    t = np.arange(s_len)
    ang = np.concatenate([_rope_angles(t // GRID_W, HEAD_DIM // 2),
                          _rope_angles(t % GRID_W, HEAD_DIM // 2)], axis=-1)
    cos, sin, zero = np.cos(ang), np.sin(ang), np.zeros_like(ang)
    c = np.concatenate([cos, cos, cos, cos], axis=-1)
    sa = np.concatenate([-sin, zero, -sin, zero], axis=-1)
    sb = np.concatenate([zero, sin, zero, sin], axis=-1)
    return [jnp.asarray(a, F32) for a in (c, sa, sb)]


def _mla_tables(s_len, scale):
    ang = _rope_angles(np.arange(s_len), C_ROPE_DIM)
    cos, sin = np.cos(ang), np.sin(ang)
    half = C_ROPE_DIM // 2
    ones = np.ones((s_len, C_NOPE_DIM))
    z_nope = np.zeros((s_len, C_NOPE_DIM))
    z_half = np.zeros((s_len, half))
    z_tail = np.zeros((s_len, LANES - C_QK_DIM))
    c = np.concatenate([ones, cos, cos, z_tail], axis=-1)
    sa = np.concatenate([z_nope, z_half, sin, z_tail], axis=-1)
    sb = np.concatenate([z_nope, -sin, z_half, z_tail], axis=-1)
    return [jnp.asarray(a * scale, F32) for a in (c, sa, sb)]


def _t5_bucket(rel):
    nb = NUM_BUCKETS // 2
    max_exact = nb // 2
    base = np.where(rel > 0, nb, 0)
    n = np.abs(rel)
    nf = np.maximum(n, 1).astype(np.float32)
    large = max_exact + (np.log(nf / np.float32(max_exact)) / np.float32(math.log(REL_MAX_DISTANCE / max_exact))
                         * np.float32(nb - max_exact)).astype(np.int32)
    large = np.minimum(large, nb - 1)
    return base + np.where(n < max_exact, n, large)


def _dilated_structure(s_len):
    delta = np.arange(2 * s_len) - s_len
    mult = np.zeros(delta.shape, np.int32)
    for w, d in B_BRANCHES:
        n_side = w // (2 * d)
        mult += ((delta % d == 0) & (np.abs(delta) <= n_side * d)).astype(np.int32)
    log_mult = np.where(mult > 0, np.log(np.maximum(mult, 1).astype(np.float64)), NEG_INF)
    onehot = (_t5_bucket(delta)[None, :] == np.arange(NUM_BUCKETS)[:, None]) & (mult > 0)[None, :]
    return jnp.asarray(onehot, F32), jnp.asarray(log_mult[None, :], F32)


def _params(*sem):
    return pltpu.CompilerParams(dimension_semantics=sem, vmem_limit_bytes=VMEM_LIMIT_BYTES)


def _resident(shape):
    return pl.BlockSpec(shape, lambda *_: (0,) * len(shape), pipeline_mode=pl.Buffered(1))


def _row2(v):
    return v.reshape(1, -1).astype(F32)


def _mlp_call(h, mixes, wo, g, w1, w2, gf, name):
    n_tok = h.shape[0]
    tm = TOKEN_TILE
    tok = lambda w: pl.BlockSpec((tm, w), lambda i: (i, 0))
    in_specs = [tok(D_MODEL)] + [tok(m.shape[1]) for m in mixes] + [
        _resident(wo.shape), _resident((1, D_MODEL)), _resident(w1.shape), _resident(w2.shape)]
    args = [h, *mixes, wo, _row2(g), w1, w2]
    if gf is not None:
        in_specs.append(_resident((1, D_MODEL)))
        args.append(_row2(gf))
    return pl.pallas_call(
        functools.partial(_mlp_kernel, len(mixes), gf is not None),
        out_shape=jax.ShapeDtypeStruct((n_tok, D_MODEL), F32),
        grid=(n_tok // tm,),
        in_specs=in_specs,
        out_specs=tok(D_MODEL),
        compiler_params=_params("parallel"),
        name=name,
    )(*args)


def kernel(x, norm_mix_g, norm_mlp_g, ab_w_in, a_q_norm_g, a_k_norm_g, ab_w_out, rel_bias,
           c_w_down, c_q_norm_g, c_kv_norm_g, c_w_uq, c_w_ukv, c_w_out, mlp_w1, mlp_w2,
           final_norm_g):
    b, s, d = x.shape
    n_tok = b * s
    tm = TOKEN_TILE
    seq_tiles = s // tm
    h0 = x.reshape(n_tok, d)

    tables0 = _axial_tables(s)
    gq = _row2(jnp.tile(a_q_norm_g[0], LANES // HEAD_DIM))
    gk = _row2(jnp.tile(a_k_norm_g[0], LANES // HEAD_DIM))
    tab_spec = pl.BlockSpec((tm, LANES), lambda i: (i % seq_tiles, 0))
    proj = pl.pallas_call(
        _proj0_kernel,
        out_shape=jax.ShapeDtypeStruct((n_tok, AB_IN_W), BF16),
        grid=(n_tok // tm,),
        in_specs=[pl.BlockSpec((tm, d), lambda i: (i, 0)), _resident((1, d)),
                  _resident((d, AB_IN_W)), _resident((1, LANES)), _resident((1, LANES)),
                  tab_spec, tab_spec, tab_spec],
        out_specs=pl.BlockSpec((tm, AB_IN_W), lambda i: (i, 0)),
        compiler_params=_params("parallel"),
        name="proj0",
    )(h0, _row2(norm_mix_g[0]), ab_w_in[0].astype(BF16), gq, gk, *tables0)

    tq = ATTN_Q_TILE
    q_tiles = s // tq
    o_a = pl.pallas_call(
        _gqa_kernel,
        out_shape=jax.ShapeDtypeStruct((n_tok, A_Q_W), BF16),
        grid=(b, q_tiles),
        in_specs=[pl.BlockSpec((tq, A_Q_W), lambda bi, qi: (bi * q_tiles + qi, COL_QA // A_Q_W)),
                  pl.BlockSpec((s, A_KV_W), lambda bi, qi: (bi, COL_KA // A_KV_W)),
                  pl.BlockSpec((s, A_KV_W), lambda bi, qi: (bi, COL_VA // A_KV_W))],
        out_specs=pl.BlockSpec((tq, A_Q_W), lambda bi, qi: (bi * q_tiles + qi, 0)),
        compiler_params=_params("parallel", "parallel"),
        name="gqa",
    )(proj, proj, proj)

    onehot, log_mult = _dilated_structure(s)
    pairs = B_W // LANES
    heads_per_pair = LANES // HEAD_DIM
    n_blk = s // DIL_Q_BLOCK
    rb = rel_bias.astype(F32).reshape(NUM_BUCKETS, pairs, heads_per_pair).transpose(1, 0, 2)
    o_b = pl.pallas_call(
        _dilated_kernel,
        out_shape=jax.ShapeDtypeStruct((n_tok, B_W), BF16),
        grid=(pairs, b),
        in_specs=[pl.BlockSpec((s, LANES), lambda p, bi: (bi, COL_QB // LANES + p)),
                  pl.BlockSpec((s, LANES), lambda p, bi: (bi, COL_KB // LANES + p)),
                  pl.BlockSpec((s, LANES), lambda p, bi: (bi, COL_VB // LANES + p)),
                  pl.BlockSpec((None, NUM_BUCKETS, heads_per_pair), lambda p, bi: (p, 0, 0)),
                  _resident(onehot.shape), _resident(log_mult.shape)],
        out_specs=pl.BlockSpec((s, LANES), lambda p, bi: (bi, p)),
        scratch_shapes=[pltpu.VMEM((heads_per_pair, DIL_Q_BLOCK, (2 * n_blk - 1) * DIL_Q_BLOCK), F32)],
        compiler_params=_params("parallel", "arbitrary"),
        name="dilated",
    )(proj, proj, proj, rb, onehot, log_mult)

    h1 = _mlp_call(h0, [o_a, o_b], ab_w_out[0].astype(BF16), norm_mlp_g[0],
                   mlp_w1[0].astype(BF16), mlp_w2[0].astype(BF16), None, "mix0_mlp0")

    scale_c = C_QK_DIM ** -0.5 * LOG2E
    tables_q = _mla_tables(s, scale_c)
    tables_k = _mla_tables(s, 1.0)
    zeros = lambda r, c: jnp.zeros((r, c), F32)
    wd = c_w_down[0]
    kv_end = C_Q_RANK + C_KV_RANK
    wd_p = jnp.concatenate([wd[:, :kv_end], zeros(d, C_NOPE_DIM), wd[:, kv_end:],
                            zeros(d, LANES - C_QK_DIM)], axis=1).astype(BF16)
    wuq = c_w_uq[0].reshape(C_Q_RANK, C_HEADS, C_QK_DIM)
    wuq_p = jnp.pad(wuq, ((0, 0), (0, 0), (0, LANES - C_QK_DIM))).reshape(
        C_Q_RANK, C_HEADS * LANES).astype(BF16)
    wukv = c_w_ukv[0].reshape(C_KV_RANK, C_HEADS, C_NOPE_DIM + C_V_DIM)
    wuk_p = jnp.pad(wukv[:, :, :C_NOPE_DIM], ((0, 0), (0, 0), (0, LANES - C_NOPE_DIM))).reshape(
        C_KV_RANK, C_HEADS * LANES).astype(BF16)
    wuv_p = wukv[:, :, C_NOPE_DIM:].reshape(C_KV_RANK, C_HEADS * C_V_DIM).astype(BF16)

    qk_w = C_HEADS * LANES
    v_w = C_HEADS * C_V_DIM
    tok = lambda w: pl.BlockSpec((tm, w), lambda i: (i, 0))
    q_c, k_c, v_c = pl.pallas_call(
        _proj1_kernel,
        out_shape=(jax.ShapeDtypeStruct((n_tok, qk_w), BF16),
                   jax.ShapeDtypeStruct((n_tok, qk_w), BF16),
                   jax.ShapeDtypeStruct((n_tok, v_w), BF16)),
        grid=(n_tok // tm,),
        in_specs=[tok(d), _resident((1, d)), _resident(wd_p.shape),
                  _resident((1, C_Q_RANK)), _resident((1, C_KV_RANK)),
                  _resident(wuq_p.shape), _resident(wuk_p.shape), _resident(wuv_p.shape)]
                 + [tab_spec] * 6,
        out_specs=(tok(qk_w), tok(qk_w), tok(v_w)),
        compiler_params=_params("parallel"),
        name="proj1",
    )(h1, _row2(norm_mix_g[1]), wd_p, _row2(c_q_norm_g[0]), _row2(c_kv_norm_g[0]),
      wuq_p, wuk_p, wuv_p, *tables_q, *tables_k)

    o_c = pl.pallas_call(
        _mla_kernel,
        out_shape=jax.ShapeDtypeStruct((n_tok, v_w), BF16),
        grid=(b, C_HEADS // 2),
        in_specs=[pl.BlockSpec((s, 2 * LANES), lambda bi, p: (bi, p)),
                  pl.BlockSpec((s, 2 * LANES), lambda bi, p: (bi, p)),
                  pl.BlockSpec((s, 2 * C_V_DIM), lambda bi, p: (bi, p))],
        out_specs=pl.BlockSpec((s, 2 * C_V_DIM), lambda bi, p: (bi, p)),
        compiler_params=_params("parallel", "parallel"),
        name="mla",
    )(q_c, k_c, v_c)

    out = _mlp_call(h1, [o_c], c_w_out[0].astype(BF16), norm_mlp_g[1],
                    mlp_w1[1].astype(BF16), mlp_w2[1].astype(BF16), final_norm_g, "mix1_mlp1")
    return out.reshape(b, s, d)
```

```python
import functools
import math

import jax
import jax.numpy as jnp
import numpy as np
from jax import lax
from jax.experimental import pallas as pl
from jax.experimental.pallas import tpu as pltpu

F32 = jnp.float32
BF16 = jnp.bfloat16

D_MODEL = 1024
GRID_W = 64
HEAD_DIM = 64
ROPE_THETA = 10000.0
EPS = 1e-6
NEG_INF = -1e30
LOG2E = math.log2(math.e)

A_HEADS = 8
A_KV_HEADS = 2
B_HEADS = 8
B_BRANCHES = ((128, 1), (512, 4), (2048, 16))
NUM_BUCKETS = 32
REL_MAX_DISTANCE = 1024

C_HEADS = 16
C_Q_RANK = 256
C_KV_RANK = 128
C_NOPE_DIM = 64
C_ROPE_DIM = 32
C_V_DIM = 64
C_QK_DIM = C_NOPE_DIM + C_ROPE_DIM
D_FF = 4 * D_MODEL

A_Q_W = A_HEADS * HEAD_DIM
A_KV_W = A_KV_HEADS * HEAD_DIM
B_W = B_HEADS * HEAD_DIM

LANES = 128
VMEM_LIMIT_BYTES = 56 * 1024 * 1024

TOKEN_TILE = 512
ATTN_Q_TILE = 256
FF_CHUNK = 1024

COL_QA = 0
COL_KA = COL_QA + A_Q_W
COL_QB = COL_KA + A_KV_W
COL_KB = COL_QB + B_W
QK0_W = COL_KB + B_W
ROW_VA = 0
ROW_VB = A_KV_W
VT0_ROWS = ROW_VB + B_W


def _rms(x):
    return x * lax.rsqrt(jnp.mean(x * x, axis=-1, keepdims=True) + EPS)


def _dot(a, b):
    return jnp.dot(a, b, preferred_element_type=F32)


def _dot_nt(a, b):
    return lax.dot_general(a, b, (((1,), (1,)), ((), ())), preferred_element_type=F32)


def _dot_tn(a, b):
    return lax.dot_general(a, b, (((0,), (0,)), ((), ())), preferred_element_type=F32)


def _with_ones(vt):
    return jnp.concatenate([vt, jnp.ones_like(vt)], axis=0)


def _attend(k, q, vt_ones, bias_t=None):
    dv = vt_ones.shape[0] // 2
    s = _dot_nt(k, q)
    if bias_t is not None:
        s = s + bias_t
    m = jnp.max(s, axis=0, keepdims=True)
    p = jnp.exp2(s - m).astype(BF16)
    acc = _dot(vt_ones, p)
    return acc[:dv] / acc[dv:dv + 1]


def _proj0_kernel(x_ref, g_ref, w_ref, wvt_ref, gq_ref, gk_ref, c_ref, sa_ref, sb_ref,
                  qk_ref, vt_ref):
    tm = x_ref.shape[0]
    xn = (_rms(x_ref[...]) * g_ref[...]).astype(BF16)
    low = lax.broadcasted_iota(jnp.int32, (tm, LANES), 1) < HEAD_DIM
    cos, sin_a, sin_b = c_ref[...], sa_ref[...], sb_ref[...]

    def norm_rope(y, gain):
        sq = y * y
        s_all = jnp.sum(sq, axis=-1, keepdims=True)
        s_low = jnp.sum(jnp.where(low, sq, 0.0), axis=-1, keepdims=True)
        ms = jnp.where(low, s_low, s_all - s_low) * (1.0 / HEAD_DIM)
        y = y * lax.rsqrt(ms + EPS) * gain
        return (y * cos + pltpu.roll(y, LANES - HEAD_DIM // 2, 1) * sin_a
                + pltpu.roll(y, HEAD_DIM // 2, 1) * sin_b)

    scale = HEAD_DIM ** -0.5 * LOG2E
    for c0 in range(COL_QA, COL_KA, LANES):
        y = _dot(xn, w_ref[:, c0:c0 + LANES])
        qk_ref[:, c0:c0 + LANES] = (norm_rope(y, gq_ref[...]) * scale).astype(BF16)
    y = _dot(xn, w_ref[:, COL_KA:COL_QB])
    qk_ref[:, COL_KA:COL_QB] = norm_rope(y, gk_ref[...]).astype(BF16)
    qk_ref[:, COL_QB:COL_KB] = (_dot(xn, w_ref[:, COL_QB:COL_KB]) * scale).astype(BF16)
    qk_ref[:, COL_KB:] = _dot(xn, w_ref[:, COL_KB:]).astype(BF16)
    vt_ref[...] = _dot_nt(wvt_ref[...], xn).astype(BF16)


def _gqa_kernel(q_ref, k_ref, vt_ref, o_ref):
    rep = A_HEADS // A_KV_HEADS
    for g in range(A_KV_HEADS):
        k = k_ref[:, g * HEAD_DIM:(g + 1) * HEAD_DIM]
        vt_ones = _with_ones(vt_ref[g * HEAD_DIM:(g + 1) * HEAD_DIM, :])
        for r in range(rep):
            h = g * rep + r
            q = q_ref[:, h * HEAD_DIM:(h + 1) * HEAD_DIM]
            o_ref[h * HEAD_DIM:(h + 1) * HEAD_DIM, :] = _attend(k, q, vt_ones).astype(o_ref.dtype)


def _dilated_kernel(q_ref, k_ref, vt_ref, rb_ref, onehot_ref, logm_ref, o_ref, e_ref):
    s_len = q_ref.shape[0]
    blk = ATTN_Q_TILE
    n_blk = s_len // blk
    reach = -(-max((w // (2 * d)) * d for w, d in B_BRANCHES) // blk)
    heads = LANES // HEAD_DIM
    e_rows = (2 * n_blk - 1) * blk

    @pl.when(pl.program_id(1) == 0)
    def _():
        for hh in range(heads):
            per_delta = jnp.sum(onehot_ref[...] * rb_ref[:, hh:hh + 1], axis=0, keepdims=True)
            per_delta = (per_delta + logm_ref[...]) * LOG2E
            spread = jnp.broadcast_to(per_delta, (blk, 2 * s_len))
            toeplitz = pltpu.roll(spread, 0, 1, stride=1, stride_axis=0)
            e_ref[hh] = toeplitz[:, blk:blk + e_rows].T

    for hh in range(heads):
        lanes = slice(hh * HEAD_DIM, (hh + 1) * HEAD_DIM)
        vt_ones = _with_ones(vt_ref[lanes, :])
        for i in range(n_blk):
            k_lo = max(0, i - reach) * blk
            k_hi = min(n_blk, i + reach + 1) * blk
            off = (n_blk - 1 - i) * blk
            cols = slice(i * blk, (i + 1) * blk)
            o = _attend(k_ref[k_lo:k_hi, lanes], q_ref[cols, lanes], vt_ones[:, k_lo:k_hi],
                        e_ref[hh, off + k_lo:off + k_hi, :])
            o_ref[lanes, cols] = o.astype(o_ref.dtype)


def _mlp_kernel(n_mix, final_norm, *refs):
    h_ref = refs[0]
    mix_refs = refs[1:1 + n_mix]
    wo_ref, g_ref, w1_ref, w2_ref = refs[1 + n_mix:5 + n_mix]
    gf_ref = refs[5 + n_mix] if final_norm else None
    out_ref = refs[-1]

    mix_t = jnp.concatenate([m_ref[...] for m_ref in mix_refs], axis=0)
    h = h_ref[...] + _dot_tn(mix_t, wo_ref[...])
    hn = (_rms(h) * g_ref[...]).astype(BF16)
    for c0 in range(0, D_FF, FF_CHUNK):
        a = jnp.maximum(_dot(hn, w1_ref[:, c0:c0 + FF_CHUNK]), 0.0)
        h = h + _dot((a * a).astype(BF16), w2_ref[c0:c0 + FF_CHUNK, :])
    if final_norm:
        h = _rms(h) * gf_ref[...]
    out_ref[...] = h


def _proj1_kernel(h_ref, g_ref, wd_ref, gq_ref, gkv_ref, wuq_ref, wuk_ref, wuvt_ref,
                  cq_ref, saq_ref, sbq_ref, ck_ref, sak_ref, sbk_ref,
                  q_ref, k_ref, vt_ref):
    xn = (_rms(h_ref[...]) * g_ref[...]).astype(BF16)
    hd = _dot(xn, wd_ref[...])
    c_q = (_rms(hd[:, :C_Q_RANK]) * gq_ref[...]).astype(BF16)
    c_kv = (_rms(hd[:, C_Q_RANK:C_Q_RANK + C_KV_RANK]) * gkv_ref[...]).astype(BF16)
    half = C_ROPE_DIM // 2

    def rope(y, cos, sin_a, sin_b):
        return y * cos + pltpu.roll(y, half, 1) * sin_a + pltpu.roll(y, LANES - half, 1) * sin_b

    cq, saq, sbq = cq_ref[...], saq_ref[...], sbq_ref[...]
    k_rope = rope(hd[:, C_Q_RANK + C_KV_RANK:], ck_ref[...], sak_ref[...], sbk_ref[...])
    for hh in range(C_HEADS):
        lanes = slice(hh * LANES, (hh + 1) * LANES)
        q_ref[:, lanes] = rope(_dot(c_q, wuq_ref[:, lanes]), cq, saq, sbq).astype(BF16)
        k_ref[:, lanes] = (_dot(c_kv, wuk_ref[:, lanes]) + k_rope).astype(BF16)
    vt_ref[...] = _dot_nt(wuvt_ref[...], c_kv).astype(BF16)


def _mla_kernel(q_ref, k_ref, vt_ref, o_ref):
    s_len = q_ref.shape[0]
    for hh in range(2):
        lanes = slice(hh * LANES, (hh + 1) * LANES)
        v_rows = slice(hh * C_V_DIM, (hh + 1) * C_V_DIM)
        vt_ones = _with_ones(vt_ref[v_rows, :])
        for i in range(s_len // ATTN_Q_TILE):
            cols = slice(i * ATTN_Q_TILE, (i + 1) * ATTN_Q_TILE)
            o = _attend(k_ref[:, lanes], q_ref[cols, lanes], vt_ones)
            o_ref[v_rows, cols] = o.astype(o_ref.dtype)


def _rope_angles(pos, dim):
    inv_freq = ROPE_THETA ** (-np.arange(0, dim, 2, dtype=np.float64) / dim)
    return pos.astype(np.float64)[:, None] * inv_freq[None, :]


def _axial_tables(s_len):
    t = np.arange(s_len)
    ang = np.concatenate([_rope_angles(t // GRID_W, HEAD_DIM // 2),
                          _rope_angles(t % GRID_W, HEAD_DIM // 2)], axis=-1)
    cos, sin, zero = np.cos(ang), np.sin(ang), np.zeros_like(ang)
    c = np.concatenate([cos, cos, cos, cos], axis=-1)
    sa = np.concatenate([-sin, zero, -sin, zero], axis=-1)
    sb = np.concatenate([zero, sin, zero, sin], axis=-1)
    return [jnp.asarray(a, F32) for a in (c, sa, sb)]


def _mla_tables(s_len, scale):
    ang = _rope_angles(np.arange(s_len), C_ROPE_DIM)
    cos, sin = np.cos(ang), np.sin(ang)
    half = C_ROPE_DIM // 2
    ones = np.ones((s_len, C_NOPE_DIM))
    z_nope = np.zeros((s_len, C_NOPE_DIM))
    z_half = np.zeros((s_len, half))
    z_tail = np.zeros((s_len, LANES - C_QK_DIM))
    c = np.concatenate([ones, cos, cos, z_tail], axis=-1)
    sa = np.concatenate([z_nope, z_half, sin, z_tail], axis=-1)
    sb = np.concatenate([z_nope, -sin, z_half, z_tail], axis=-1)
    return [jnp.asarray(a * scale, F32) for a in (c, sa, sb)]


def _t5_bucket(rel):
    nb = NUM_BUCKETS // 2
    max_exact = nb // 2
    base = np.where(rel > 0, nb, 0)
    n = np.abs(rel)
    nf = np.maximum(n, 1).astype(np.float32)
    large = max_exact + (np.log(nf / np.float32(max_exact)) / np.float32(math.log(REL_MAX_DISTANCE / max_exact))
                         * np.float32(nb - max_exact)).astype(np.int32)
    large = np.minimum(large, nb - 1)
    return base + np.where(n < max_exact, n, large)


def _dilated_structure(s_len):
    delta = np.arange(2 * s_len) - s_len
    mult = np.zeros(delta.shape, np.int32)
    for w, d in B_BRANCHES:
        n_side = w // (2 * d)
        mult += ((delta % d == 0) & (np.abs(delta) <= n_side * d)).astype(np.int32)
    log_mult = np.where(mult > 0, np.log(np.maximum(mult, 1).astype(np.float64)), NEG_INF)
    onehot = (_t5_bucket(delta)[None, :] == np.arange(NUM_BUCKETS)[:, None]) & (mult > 0)[None, :]
    return jnp.asarray(onehot, F32), jnp.asarray(log_mult[None, :], F32)


def _params(*sem):
    return pltpu.CompilerParams(dimension_semantics=sem, vmem_limit_bytes=VMEM_LIMIT_BYTES)


def _resident(shape):
    return pl.BlockSpec(shape, lambda *_: (0,) * len(shape), pipeline_mode=pl.Buffered(1))


def _row2(v):
    return v.reshape(1, -1).astype(F32)


def _mlp_call(h, mixes_t, seq_tiles, wo, g, w1, w2, gf, name):
    n_tok = h.shape[0]
    tm = TOKEN_TILE
    tok = lambda w: pl.BlockSpec((tm, w), lambda i: (i, 0))
    tok_t = lambda w: pl.BlockSpec((None, w, tm), lambda i: (i // seq_tiles, 0, i % seq_tiles))
    in_specs = [tok(D_MODEL)] + [tok_t(m.shape[1]) for m in mixes_t] + [
        _resident(wo.shape), _resident((1, D_MODEL)), _resident(w1.shape), _resident(w2.shape)]
    args = [h, *mixes_t, wo, _row2(g), w1, w2]
    if gf is not None:
        in_specs.append(_resident((1, D_MODEL)))
        args.append(_row2(gf))
    return pl.pallas_call(
        functools.partial(_mlp_kernel, len(mixes_t), gf is not None),
        out_shape=jax.ShapeDtypeStruct((n_tok, D_MODEL), F32),
        grid=(n_tok // tm,),
        in_specs=in_specs,
        out_specs=tok(D_MODEL),
        compiler_params=_params("parallel"),
        name=name,
    )(*args)


def kernel(x, norm_mix_g, norm_mlp_g, ab_w_in, a_q_norm_g, a_k_norm_g, ab_w_out, rel_bias,
           c_w_down, c_q_norm_g, c_kv_norm_g, c_w_uq, c_w_ukv, c_w_out, mlp_w1, mlp_w2,
           final_norm_g):
    b, s, d = x.shape
    n_tok = b * s
    tm = TOKEN_TILE
    seq_tiles = s // tm
    h0 = x.reshape(n_tok, d)
    tok = lambda w: pl.BlockSpec((tm, w), lambda i: (i, 0))
    tok_t = lambda w: pl.BlockSpec((None, w, tm), lambda i: (i // seq_tiles, 0, i % seq_tiles))
    tab_spec = pl.BlockSpec((tm, LANES), lambda i: (i % seq_tiles, 0))

    tables0 = _axial_tables(s)
    gq = _row2(jnp.tile(a_q_norm_g[0], LANES // HEAD_DIM))
    gk = _row2(jnp.tile(a_k_norm_g[0], LANES // HEAD_DIM))
    w_in = ab_w_in[0]
    q_a, k_a, v_a, q_b, k_b, v_b = jnp.split(
        w_in, np.cumsum([A_Q_W, A_KV_W, A_KV_W, B_W, B_W]).tolist(), axis=1)
    w_qk = jnp.concatenate([q_a, k_a, q_b, k_b], axis=1).astype(BF16)
    w_vt = jnp.concatenate([v_a, v_b], axis=1).T.astype(BF16)
    qk0, vt0 = pl.pallas_call(
        _proj0_kernel,
        out_shape=(jax.ShapeDtypeStruct((n_tok, QK0_W), BF16),
                   jax.ShapeDtypeStruct((b, VT0_ROWS, s), BF16)),
        grid=(n_tok // tm,),
        in_specs=[tok(d), _resident((1, d)), _resident(w_qk.shape), _resident(w_vt.shape),
                  _resident((1, LANES)), _resident((1, LANES)), tab_spec, tab_spec, tab_spec],
        out_specs=(tok(QK0_W), tok_t(VT0_ROWS)),
        compiler_params=_params("parallel"),
        name="proj0",
    )(h0, _row2(norm_mix_g[0]), w_qk, w_vt, gq, gk, *tables0)

    tq = ATTN_Q_TILE
    q_tiles = s // tq
    o_a = pl.pallas_call(
        _gqa_kernel,
        out_shape=jax.ShapeDtypeStruct((b, A_Q_W, s), BF16),
        grid=(b, q_tiles),
        in_specs=[pl.BlockSpec((tq, A_Q_W), lambda bi, qi: (bi * q_tiles + qi, COL_QA // A_Q_W)),
                  pl.BlockSpec((s, A_KV_W), lambda bi, qi: (bi, COL_KA // A_KV_W)),
                  pl.BlockSpec((None, A_KV_W, s), lambda bi, qi: (bi, ROW_VA // A_KV_W, 0))],
        out_specs=pl.BlockSpec((None, A_Q_W, tq), lambda bi, qi: (bi, 0, qi)),
        compiler_params=_params("parallel", "parallel"),
        name="gqa",
    )(qk0, qk0, vt0)

    onehot, log_mult = _dilated_structure(s)
    pairs = B_W // LANES
    heads_per_pair = LANES // HEAD_DIM
    rb = rel_bias.astype(F32).reshape(NUM_BUCKETS, pairs, heads_per_pair).transpose(1, 0, 2)
    o_b = pl.pallas_call(
        _dilated_kernel,
        out_shape=jax.ShapeDtypeStruct((b, B_W, s), BF16),
        grid=(pairs, b),
        in_specs=[pl.BlockSpec((s, LANES), lambda p, bi: (bi, COL_QB // LANES + p)),
                  pl.BlockSpec((s, LANES), lambda p, bi: (bi, COL_KB // LANES + p)),
                  pl.BlockSpec((None, LANES, s), lambda p, bi: (bi, ROW_VB // LANES + p, 0)),
                  pl.BlockSpec((None, NUM_BUCKETS, heads_per_pair), lambda p, bi: (p, 0, 0)),
                  _resident(onehot.shape), _resident(log_mult.shape)],
        out_specs=pl.BlockSpec((None, LANES, s), lambda p, bi: (bi, p, 0)),
        scratch_shapes=[pltpu.VMEM((heads_per_pair, (2 * q_tiles - 1) * tq, tq), F32)],
        compiler_params=_params("parallel", "arbitrary"),
        name="dilated",
    )(qk0, qk0, vt0, rb, onehot, log_mult)

    h1 = _mlp_call(h0, [o_a, o_b], seq_tiles, ab_w_out[0].astype(BF16), norm_mlp_g[0],
                   mlp_w1[0].astype(BF16), mlp_w2[0].astype(BF16), None, "mix0_mlp0")

    scale_c = C_QK_DIM ** -0.5 * LOG2E
    tables_q = _mla_tables(s, scale_c)
    tables_k = _mla_tables(s, 1.0)
    zeros = lambda r, c: jnp.zeros((r, c), F32)
    wd = c_w_down[0]
    kv_end = C_Q_RANK + C_KV_RANK
    wd_p = jnp.concatenate([wd[:, :kv_end], zeros(d, C_NOPE_DIM), wd[:, kv_end:],
                            zeros(d, LANES - C_QK_DIM)], axis=1).astype(BF16)
    wuq = c_w_uq[0].reshape(C_Q_RANK, C_HEADS, C_QK_DIM)
    wuq_p = jnp.pad(wuq, ((0, 0), (0, 0), (0, LANES - C_QK_DIM))).reshape(
        C_Q_RANK, C_HEADS * LANES).astype(BF16)
    wukv = c_w_ukv[0].reshape(C_KV_RANK, C_HEADS, C_NOPE_DIM + C_V_DIM)
    wuk_p = jnp.pad(wukv[:, :, :C_NOPE_DIM], ((0, 0), (0, 0), (0, LANES - C_NOPE_DIM))).reshape(
        C_KV_RANK, C_HEADS * LANES).astype(BF16)
    wuv_t = wukv[:, :, C_NOPE_DIM:].reshape(C_KV_RANK, C_HEADS * C_V_DIM).T.astype(BF16)

    qk_w = C_HEADS * LANES
    v_w = C_HEADS * C_V_DIM
    q_c, k_c, vt_c = pl.pallas_call(
        _proj1_kernel,
        out_shape=(jax.ShapeDtypeStruct((n_tok, qk_w), BF16),
                   jax.ShapeDtypeStruct((n_tok, qk_w), BF16),
                   jax.ShapeDtypeStruct((b, v_w, s), BF16)),
        grid=(n_tok // tm,),
        in_specs=[tok(d), _resident((1, d)), _resident(wd_p.shape),
                  _resident((1, C_Q_RANK)), _resident((1, C_KV_RANK)),
                  _resident(wuq_p.shape), _resident(wuk_p.shape), _resident(wuv_t.shape)]
                 + [tab_spec] * 6,
        out_specs=(tok(qk_w), tok(qk_w), tok_t(v_w)),
        compiler_params=_params("parallel"),
        name="proj1",
    )(h1, _row2(norm_mix_g[1]), wd_p, _row2(c_q_norm_g[0]), _row2(c_kv_norm_g[0]),
      wuq_p, wuk_p, wuv_t, *tables_q, *tables_k)

    o_c = pl.pallas_call(
        _mla_kernel,
        out_shape=jax.ShapeDtypeStruct((b, v_w, s), BF16),
        grid=(b, C_HEADS // 2),
        in_specs=[pl.BlockSpec((s, 2 * LANES), lambda bi, p: (bi, p)),
                  pl.BlockSpec((s, 2 * LANES), lambda bi, p: (bi, p)),
                  pl.BlockSpec((None, 2 * C_V_DIM, s), lambda bi, p: (bi, p, 0))],
        out_specs=pl.BlockSpec((None, 2 * C_V_DIM, s), lambda bi, p: (bi, p, 0)),
        compiler_params=_params("parallel", "parallel"),
        name="mla",
    )(q_c, k_c, vt_c)

    out = _mlp_call(h1, [o_c], seq_tiles, c_w_out[0].astype(BF16), norm_mlp_g[1],
                    mlp_w1[1].astype(BF16), mlp_w2[1].astype(BF16), final_norm_g, "mix1_mlp1")
    return out.reshape(b, s, d)
```

```python
import functools
import math

import jax
import jax.numpy as jnp
import numpy as np
from jax import lax
from jax.experimental import pallas as pl
from jax.experimental.pallas import tpu as pltpu

F32 = jnp.float32
BF16 = jnp.bfloat16

D_MODEL = 1024
GRID_W = 64
HEAD_DIM = 64
ROPE_THETA = 10000.0
EPS = 1e-6
NEG_INF = -1e30
LOG2E = math.log2(math.e)

A_HEADS = 8
A_KV_HEADS = 2
B_HEADS = 8
B_BRANCHES = ((128, 1), (512, 4), (2048, 16))
NUM_BUCKETS = 32
REL_MAX_DISTANCE = 1024

C_HEADS = 16
C_Q_RANK = 256
C_KV_RANK = 128
C_NOPE_DIM = 64
C_ROPE_DIM = 32
C_V_DIM = 64
C_QK_DIM = C_NOPE_DIM + C_ROPE_DIM
D_FF = 4 * D_MODEL

A_Q_W = A_HEADS * HEAD_DIM
A_KV_W = A_KV_HEADS * HEAD_DIM
B_W = B_HEADS * HEAD_DIM

LANES = 128
VMEM_LIMIT_BYTES = 56 * 1024 * 1024

TOKEN_TILE = 512
ATTN_Q_TILE = 256
FF_CHUNK = 1024

COL_QA = 0
COL_KA = COL_QA + A_Q_W
COL_QB = COL_KA + A_KV_W
COL_KB = COL_QB + B_W
QK0_W = COL_KB + B_W
ROW_VA = 0
ROW_VB = A_KV_W
VT0_ROWS = ROW_VB + B_W


def _rms(x):
    return x * lax.rsqrt(jnp.mean(x * x, axis=-1, keepdims=True) + EPS)


def _dot(a, b):
    return jnp.dot(a, b, preferred_element_type=F32)


def _dot_nt(a, b):
    return lax.dot_general(a, b, (((1,), (1,)), ((), ())), preferred_element_type=F32)


def _dot_tn(a, b):
    return lax.dot_general(a, b, (((0,), (0,)), ((), ())), preferred_element_type=F32)


def _with_ones(vt):
    return jnp.concatenate([vt, jnp.ones_like(vt)], axis=0)


def _attend(k, q, vt_ones, bias=None):
    dv = vt_ones.shape[0] // 2
    s = _dot_nt(q, k)
    if bias is not None:
        s = s + bias
    m = jnp.max(s, axis=-1, keepdims=True)
    p = jnp.exp2(s - m).astype(BF16)
    acc = _dot(vt_ones, p.T)
    return acc[:dv] / acc[dv:dv + 1]


def _proj0_kernel(x_ref, g_ref, w_ref, wvt_ref, gq_ref, gk_ref, c_ref, sa_ref, sb_ref,
                  qk_ref, vt_ref):
    tm = x_ref.shape[0]
    xn = (_rms(x_ref[...]) * g_ref[...]).astype(BF16)
    low = lax.broadcasted_iota(jnp.int32, (tm, LANES), 1) < HEAD_DIM
    cos, sin_a, sin_b = c_ref[...], sa_ref[...], sb_ref[...]

    def norm_rope(y, gain):
        sq = y * y
        s_all = jnp.sum(sq, axis=-1, keepdims=True)
        s_low = jnp.sum(jnp.where(low, sq, 0.0), axis=-1, keepdims=True)
        ms = jnp.where(low, s_low, s_all - s_low) * (1.0 / HEAD_DIM)
        y = y * lax.rsqrt(ms + EPS) * gain
        return (y * cos + pltpu.roll(y, LANES - HEAD_DIM // 2, 1) * sin_a
                + pltpu.roll(y, HEAD_DIM // 2, 1) * sin_b)

    scale = HEAD_DIM ** -0.5 * LOG2E
    for c0 in range(COL_QA, COL_KA, LANES):
        y = _dot(xn, w_ref[:, c0:c0 + LANES])
        qk_ref[:, c0:c0 + LANES] = (norm_rope(y, gq_ref[...]) * scale).astype(BF16)
    y = _dot(xn, w_ref[:, COL_KA:COL_QB])
    qk_ref[:, COL_KA:COL_QB] = norm_rope(y, gk_ref[...]).astype(BF16)
    qk_ref[:, COL_QB:COL_KB] = (_dot(xn, w_ref[:, COL_QB:COL_KB]) * scale).astype(BF16)
    qk_ref[:, COL_KB:] = _dot(xn, w_ref[:, COL_KB:]).astype(BF16)
    vt_ref[...] = _dot_nt(wvt_ref[...], xn).astype(BF16)


def _gqa_kernel(q_ref, k_ref, vt_ref, o_ref):
    rep = A_HEADS // A_KV_HEADS
    for g in range(A_KV_HEADS):
        k = k_ref[:, g * HEAD_DIM:(g + 1) * HEAD_DIM]
        vt_ones = _with_ones(vt_ref[g * HEAD_DIM:(g + 1) * HEAD_DIM, :])
        for r in range(rep):
            h = g * rep + r
            q = q_ref[:, h * HEAD_DIM:(h + 1) * HEAD_DIM]
            o_ref[h * HEAD_DIM:(h + 1) * HEAD_DIM, :] = _attend(k, q, vt_ones).astype(o_ref.dtype)


def _dilated_kernel(q_ref, k_ref, vt_ref, rb_ref, onehot_ref, logm_ref, o_ref, e_ref):
    s_len = q_ref.shape[0]
    blk = ATTN_Q_TILE
    n_blk = s_len // blk
    reach = -(-max((w // (2 * d)) * d for w, d in B_BRANCHES) // blk)
    heads = LANES // HEAD_DIM
    e_cols = (2 * n_blk - 1) * blk

    @pl.when(pl.program_id(1) == 0)
    def _():
        for hh in range(heads):
            per_delta = jnp.sum(onehot_ref[...] * rb_ref[:, hh:hh + 1], axis=0, keepdims=True)
            per_delta = (per_delta + logm_ref[...]) * LOG2E
            spread = jnp.broadcast_to(per_delta, (blk, 2 * s_len))
            toeplitz = pltpu.roll(spread, 0, 1, stride=1, stride_axis=0)
            e_ref[hh] = toeplitz[:, blk:blk + e_cols]

    for hh in range(heads):
        lanes = slice(hh * HEAD_DIM, (hh + 1) * HEAD_DIM)
        vt_ones = _with_ones(vt_ref[lanes, :])
        for i in range(n_blk):
            k_lo = max(0, i - reach) * blk
            k_hi = min(n_blk, i + reach + 1) * blk
            off = (n_blk - 1 - i) * blk
            cols = slice(i * blk, (i + 1) * blk)
            o = _attend(k_ref[k_lo:k_hi, lanes], q_ref[cols, lanes], vt_ones[:, k_lo:k_hi],
                        e_ref[hh, :, off + k_lo:off + k_hi])
            o_ref[lanes, cols] = o.astype(o_ref.dtype)


def _mlp_kernel(n_mix, final_norm, *refs):
    h_ref = refs[0]
    mix_refs = refs[1:1 + n_mix]
    wo_ref, g_ref, w1_ref, w2_ref = refs[1 + n_mix:5 + n_mix]
    gf_ref = refs[5 + n_mix] if final_norm else None
    out_ref = refs[-1]

    mix_t = jnp.concatenate([m_ref[...] for m_ref in mix_refs], axis=0)
    h = h_ref[...] + _dot_tn(mix_t, wo_ref[...])
    hn = (_rms(h) * g_ref[...]).astype(BF16)
    for c0 in range(0, D_FF, FF_CHUNK):
        a = jnp.maximum(_dot(hn, w1_ref[:, c0:c0 + FF_CHUNK]), 0.0)
        h = h + _dot((a * a).astype(BF16), w2_ref[c0:c0 + FF_CHUNK, :])
    if final_norm:
        h = _rms(h) * gf_ref[...]
    out_ref[...] = h


def _proj1_kernel(h_ref, g_ref, wd_ref, gq_ref, gkv_ref, wuq_ref, wuk_ref, wuvt_ref,
                  cq_ref, saq_ref, sbq_ref, ck_ref, sak_ref, sbk_ref,
                  q_ref, k_ref, vt_ref):
    xn = (_rms(h_ref[...]) * g_ref[...]).astype(BF16)
    hd = _dot(xn, wd_ref[...])
    c_q = (_rms(hd[:, :C_Q_RANK]) * gq_ref[...]).astype(BF16)
    c_kv = (_rms(hd[:, C_Q_RANK:C_Q_RANK + C_KV_RANK]) * gkv_ref[...]).astype(BF16)
    half = C_ROPE_DIM // 2

    def rope(y, cos, sin_a, sin_b):
        return y * cos + pltpu.roll(y, half, 1) * sin_a + pltpu.roll(y, LANES - half, 1) * sin_b

    cq, saq, sbq = cq_ref[...], saq_ref[...], sbq_ref[...]
    k_rope = rope(hd[:, C_Q_RANK + C_KV_RANK:], ck_ref[...], sak_ref[...], sbk_ref[...])
    for hh in range(C_HEADS):
        lanes = slice(hh * LANES, (hh + 1) * LANES)
        q_ref[:, lanes] = rope(_dot(c_q, wuq_ref[:, lanes]), cq, saq, sbq).astype(BF16)
        k_ref[:, lanes] = (_dot(c_kv, wuk_ref[:, lanes]) + k_rope).astype(BF16)
    vt_ref[...] = _dot_nt(wuvt_ref[...], c_kv).astype(BF16)


def _mla_kernel(q_ref, k_ref, vt_ref, o_ref):
    s_len = q_ref.shape[0]
    for hh in range(2):
        lanes = slice(hh * LANES, (hh + 1) * LANES)
        v_rows = slice(hh * C_V_DIM, (hh + 1) * C_V_DIM)
        vt_ones = _with_ones(vt_ref[v_rows, :])
        for i in range(s_len // ATTN_Q_TILE):
            cols = slice(i * ATTN_Q_TILE, (i + 1) * ATTN_Q_TILE)
            o = _attend(k_ref[:, lanes], q_ref[cols, lanes], vt_ones)
            o_ref[v_rows, cols] = o.astype(o_ref.dtype)


def _rope_angles(pos, dim):
    inv_freq = ROPE_THETA ** (-np.arange(0, dim, 2, dtype=np.float64) / dim)
    return pos.astype(np.float64)[:, None] * inv_freq[None, :]


def _axial_tables(s_len):
    t = np.arange(s_len)
    ang = np.concatenate([_rope_angles(t // GRID_W, HEAD_DIM // 2),
                          _rope_angles(t % GRID_W, HEAD_DIM // 2)], axis=-1)
    cos, sin, zero = np.cos(ang), np.sin(ang), np.zeros_like(ang)
    c = np.concatenate([cos, cos, cos, cos], axis=-1)
    sa = np.concatenate([-sin, zero, -sin, zero], axis=-1)
    sb = np.concatenate([zero, sin, zero, sin], axis=-1)
    return [jnp.asarray(a, F32) for a in (c, sa, sb)]


def _mla_tables(s_len, scale):
    ang = _rope_angles(np.arange(s_len), C_ROPE_DIM)
    cos, sin = np.cos(ang), np.sin(ang)
    half = C_ROPE_DIM // 2
    ones = np.ones((s_len, C_NOPE_DIM))
    z_nope = np.zeros((s_len, C_NOPE_DIM))
    z_half = np.zeros((s_len, half))
    z_tail = np.zeros((s_len, LANES - C_QK_DIM))
    c = np.concatenate([ones, cos, cos, z_tail], axis=-1)
    sa = np.concatenate([z_nope, z_half, sin, z_tail], axis=-1)
    sb = np.concatenate([z_nope, -sin, z_half, z_tail], axis=-1)
    return [jnp.asarray(a * scale, F32) for a in (c, sa, sb)]


def _t5_bucket(rel):
    nb = NUM_BUCKETS // 2
    max_exact = nb // 2
    base = np.where(rel > 0, nb, 0)
    n = np.abs(rel)
    nf = np.maximum(n, 1).astype(np.float32)
    large = max_exact + (np.log(nf / np.float32(max_exact)) / np.float32(math.log(REL_MAX_DISTANCE / max_exact))
                         * np.float32(nb - max_exact)).astype(np.int32)
    large = np.minimum(large, nb - 1)
    return base + np.where(n < max_exact, n, large)


def _dilated_structure(s_len):
    delta = np.arange(2 * s_len) - s_len
    mult = np.zeros(delta.shape, np.int32)
    for w, d in B_BRANCHES:
        n_side = w // (2 * d)
        mult += ((delta % d == 0) & (np.abs(delta) <= n_side * d)).astype(np.int32)
    log_mult = np.where(mult > 0, np.log(np.maximum(mult, 1).astype(np.float64)), NEG_INF)
    onehot = (_t5_bucket(delta)[None, :] == np.arange(NUM_BUCKETS)[:, None]) & (mult > 0)[None, :]
    return jnp.asarray(onehot, F32), jnp.asarray(log_mult[None, :], F32)


def _params(*sem):
    return pltpu.CompilerParams(dimension_semantics=sem, vmem_limit_bytes=VMEM_LIMIT_BYTES)


def _resident(shape):
    return pl.BlockSpec(shape, lambda *_: (0,) * len(shape), pipeline_mode=pl.Buffered(1))


def _row2(v):
    return v.reshape(1, -1).astype(F32)


def _mlp_call(h, mixes_t, seq_tiles, wo, g, w1, w2, gf, name):
    n_tok = h.shape[0]
    tm = TOKEN_TILE
    tok = lambda w: pl.BlockSpec((tm, w), lambda i: (i, 0))
    tok_t = lambda w: pl.BlockSpec((None, w, tm), lambda i: (i // seq_tiles, 0, i % seq_tiles))
    in_specs = [tok(D_MODEL)] + [tok_t(m.shape[1]) for m in mixes_t] + [
        _resident(wo.shape), _resident((1, D_MODEL)), _resident(w1.shape), _resident(w2.shape)]
    args = [h, *mixes_t, wo, _row2(g), w1, w2]
    if gf is not None:
        in_specs.append(_resident((1, D_MODEL)))
        args.append(_row2(gf))
    return pl.pallas_call(
        functools.partial(_mlp_kernel, len(mixes_t), gf is not None),
        out_shape=jax.ShapeDtypeStruct((n_tok, D_MODEL), F32),
        grid=(n_tok // tm,),
        in_specs=in_specs,
        out_specs=tok(D_MODEL),
        compiler_params=_params("parallel"),
        name=name,
    )(*args)


def kernel(x, norm_mix_g, norm_mlp_g, ab_w_in, a_q_norm_g, a_k_norm_g, ab_w_out, rel_bias,
           c_w_down, c_q_norm_g, c_kv_norm_g, c_w_uq, c_w_ukv, c_w_out, mlp_w1, mlp_w2,
           final_norm_g):
    b, s, d = x.shape
    n_tok = b * s
    tm = TOKEN_TILE
    seq_tiles = s // tm
    h0 = x.reshape(n_tok, d)
    tok = lambda w: pl.BlockSpec((tm, w), lambda i: (i, 0))
    tok_t = lambda w: pl.BlockSpec((None, w, tm), lambda i: (i // seq_tiles, 0, i % seq_tiles))
    tab_spec = pl.BlockSpec((tm, LANES), lambda i: (i % seq_tiles, 0))

    tables0 = _axial_tables(s)
    gq = _row2(jnp.tile(a_q_norm_g[0], LANES // HEAD_DIM))
    gk = _row2(jnp.tile(a_k_norm_g[0], LANES // HEAD_DIM))
    w_in = ab_w_in[0]
    q_a, k_a, v_a, q_b, k_b, v_b = jnp.split(
        w_in, np.cumsum([A_Q_W, A_KV_W, A_KV_W, B_W, B_W]).tolist(), axis=1)
    w_qk = jnp.concatenate([q_a, k_a, q_b, k_b], axis=1).astype(BF16)
    w_vt = jnp.concatenate([v_a, v_b], axis=1).T.astype(BF16)
    qk0, vt0 = pl.pallas_call(
        _proj0_kernel,
        out_shape=(jax.ShapeDtypeStruct((n_tok, QK0_W), BF16),
                   jax.ShapeDtypeStruct((b, VT0_ROWS, s), BF16)),
        grid=(n_tok // tm,),
        in_specs=[tok(d), _resident((1, d)), _resident(w_qk.shape), _resident(w_vt.shape),
                  _resident((1, LANES)), _resident((1, LANES)), tab_spec, tab_spec, tab_spec],
        out_specs=(tok(QK0_W), tok_t(VT0_ROWS)),
        compiler_params=_params("parallel"),
        name="proj0",
    )(h0, _row2(norm_mix_g[0]), w_qk, w_vt, gq, gk, *tables0)

    tq = ATTN_Q_TILE
    q_tiles = s // tq
    o_a = pl.pallas_call(
        _gqa_kernel,
        out_shape=jax.ShapeDtypeStruct((b, A_Q_W, s), BF16),
        grid=(b, q_tiles),
        in_specs=[pl.BlockSpec((tq, A_Q_W), lambda bi, qi: (bi * q_tiles + qi, COL_QA // A_Q_W)),
                  pl.BlockSpec((s, A_KV_W), lambda bi, qi: (bi, COL_KA // A_KV_W)),
                  pl.BlockSpec((None, A_KV_W, s), lambda bi, qi: (bi, ROW_VA // A_KV_W, 0))],
        out_specs=pl.BlockSpec((None, A_Q_W, tq), lambda bi, qi: (bi, 0, qi)),
        compiler_params=_params("parallel", "parallel"),
        name="gqa",
    )(qk0, qk0, vt0)

    onehot, log_mult = _dilated_structure(s)
    pairs = B_W // LANES
    heads_per_pair = LANES // HEAD_DIM
    rb = rel_bias.astype(F32).reshape(NUM_BUCKETS, pairs, heads_per_pair).transpose(1, 0, 2)
    o_b = pl.pallas_call(
        _dilated_kernel,
        out_shape=jax.ShapeDtypeStruct((b, B_W, s), BF16),
        grid=(pairs, b),
        in_specs=[pl.BlockSpec((s, LANES), lambda p, bi: (bi, COL_QB // LANES + p)),
                  pl.BlockSpec((s, LANES), lambda p, bi: (bi, COL_KB // LANES + p)),
                  pl.BlockSpec((None, LANES, s), lambda p, bi: (bi, ROW_VB // LANES + p, 0)),
                  pl.BlockSpec((None, NUM_BUCKETS, heads_per_pair), lambda p, bi: (p, 0, 0)),
                  _resident(onehot.shape), _resident(log_mult.shape)],
        out_specs=pl.BlockSpec((None, LANES, s), lambda p, bi: (bi, p, 0)),
        scratch_shapes=[pltpu.VMEM((heads_per_pair, tq, (2 * q_tiles - 1) * tq), F32)],
        compiler_params=_params("parallel", "arbitrary"),
        name="dilated",
    )(qk0, qk0, vt0, rb, onehot, log_mult)

    h1 = _mlp_call(h0, [o_a, o_b], seq_tiles, ab_w_out[0].astype(BF16), norm_mlp_g[0],
                   mlp_w1[0].astype(BF16), mlp_w2[0].astype(BF16), None, "mix0_mlp0")

    scale_c = C_QK_DIM ** -0.5 * LOG2E
    tables_q = _mla_tables(s, scale_c)
    tables_k = _mla_tables(s, 1.0)
    zeros = lambda r, c: jnp.zeros((r, c), F32)
    wd = c_w_down[0]
    kv_end = C_Q_RANK + C_KV_RANK
    wd_p = jnp.concatenate([wd[:, :kv_end], zeros(d, C_NOPE_DIM), wd[:, kv_end:],
                            zeros(d, LANES - C_QK_DIM)], axis=1).astype(BF16)
    wuq = c_w_uq[0].reshape(C_Q_RANK, C_HEADS, C_QK_DIM)
    wuq_p = jnp.pad(wuq, ((0, 0), (0, 0), (0, LANES - C_QK_DIM))).reshape(
        C_Q_RANK, C_HEADS * LANES).astype(BF16)
    wukv = c_w_ukv[0].reshape(C_KV_RANK, C_HEADS, C_NOPE_DIM + C_V_DIM)
    wuk_p = jnp.pad(wukv[:, :, :C_NOPE_DIM], ((0, 0), (0, 0), (0, LANES - C_NOPE_DIM))).reshape(
        C_KV_RANK, C_HEADS * LANES).astype(BF16)
    wuv_t = wukv[:, :, C_NOPE_DIM:].reshape(C_KV_RANK, C_HEADS * C_V_DIM).T.astype(BF16)

    qk_w = C_HEADS * LANES
    v_w = C_HEADS * C_V_DIM
    q_c, k_c, vt_c = pl.pallas_call(
        _proj1_kernel,
        out_shape=(jax.ShapeDtypeStruct((n_tok, qk_w), BF16),
                   jax.ShapeDtypeStruct((n_tok, qk_w), BF16),
                   jax.ShapeDtypeStruct((b, v_w, s), BF16)),
        grid=(n_tok // tm,),
        in_specs=[tok(d), _resident((1, d)), _resident(wd_p.shape),
                  _resident((1, C_Q_RANK)), _resident((1, C_KV_RANK)),
                  _resident(wuq_p.shape), _resident(wuk_p.shape), _resident(wuv_t.shape)]
                 + [tab_spec] * 6,
        out_specs=(tok(qk_w), tok(qk_w), tok_t(v_w)),
        compiler_params=_params("parallel"),
        name="proj1",
    )(h1, _row2(norm_mix_g[1]), wd_p, _row2(c_q_norm_g[0]), _row2(c_kv_norm_g[0]),
      wuq_p, wuk_p, wuv_t, *tables_q, *tables_k)

    o_c = pl.pallas_call(
        _mla_kernel,
        out_shape=jax.ShapeDtypeStruct((b, v_w, s), BF16),
        grid=(b, C_HEADS // 2),
        in_specs=[pl.BlockSpec((s, 2 * LANES), lambda bi, p: (bi, p)),
                  pl.BlockSpec((s, 2 * LANES), lambda bi, p: (bi, p)),
                  pl.BlockSpec((None, 2 * C_V_DIM, s), lambda bi, p: (bi, p, 0))],
        out_specs=pl.BlockSpec((None, 2 * C_V_DIM, s), lambda bi, p: (bi, p, 0)),
        compiler_params=_params("parallel", "parallel"),
        name="mla",
    )(q_c, k_c, vt_c)

    out = _mlp_call(h1, [o_c], seq_tiles, c_w_out[0].astype(BF16), norm_mlp_g[1],
                    mlp_w1[1].astype(BF16), mlp_w2[1].astype(BF16), final_norm_g, "mix1_mlp1")
    return out.reshape(b, s, d)
```

```python
import functools
import math

import jax
import jax.numpy as jnp
import numpy as np
from jax import lax
from jax.experimental import pallas as pl
from jax.experimental.pallas import tpu as pltpu

F32 = jnp.float32
BF16 = jnp.bfloat16

D_MODEL = 1024
GRID_W = 64
HEAD_DIM = 64
ROPE_THETA = 10000.0
EPS = 1e-6
NEG_INF = -1e30
LOG2E = math.log2(math.e)

A_HEADS = 8
A_KV_HEADS = 2
B_HEADS = 8
B_BRANCHES = ((128, 1), (512, 4), (2048, 16))
NUM_BUCKETS = 32
REL_MAX_DISTANCE = 1024

C_HEADS = 16
C_Q_RANK = 256
C_KV_RANK = 128
C_NOPE_DIM = 64
C_ROPE_DIM = 32
C_V_DIM = 64
C_QK_DIM = C_NOPE_DIM + C_ROPE_DIM
D_FF = 4 * D_MODEL

A_Q_W = A_HEADS * HEAD_DIM
A_KV_W = A_KV_HEADS * HEAD_DIM
B_W = B_HEADS * HEAD_DIM

LANES = 128
VMEM_LIMIT_BYTES = 56 * 1024 * 1024

TOKEN_TILE = 512
ATTN_Q_TILE = 256
FF_CHUNK = 1024

COL_QA = 0
COL_KA = COL_QA + A_Q_W
COL_QB = COL_KA + A_KV_W
COL_KB = COL_QB + B_W
QK0_W = COL_KB + B_W
ROW_VA = 0
ROW_VB = A_KV_W
VT0_ROWS = ROW_VB + B_W


def _rms(x):
    return x * lax.rsqrt(jnp.mean(x * x, axis=-1, keepdims=True) + EPS)


def _dot(a, b):
    return jnp.dot(a, b, preferred_element_type=F32)


def _dot_nt(a, b):
    return lax.dot_general(a, b, (((1,), (1,)), ((), ())), preferred_element_type=F32)


def _dot_tn(a, b):
    return lax.dot_general(a, b, (((0,), (0,)), ((), ())), preferred_element_type=F32)


def _with_ones(vt):
    return jnp.concatenate([vt, jnp.ones_like(vt)], axis=0)


def _softmax_pv(s, vt_ones):
    dv = vt_ones.shape[0] // 2
    m = jnp.max(s, axis=-1, keepdims=True)
    p = jnp.exp2(s - m).astype(BF16)
    acc = _dot(vt_ones, p.T)
    return acc[:dv] / acc[dv:dv + 1]


def _pipelined(blocks, scores, finish):
    s_next = scores(blocks[0])
    for n, blk in enumerate(blocks):
        s_cur = s_next
        if n + 1 < len(blocks):
            s_next = scores(blocks[n + 1])
        finish(blk, s_cur)


def _proj0_kernel(x_ref, g_ref, w_ref, wvt_ref, gq_ref, gk_ref, c_ref, sa_ref, sb_ref,
                  qk_ref, vt_ref):
    tm = x_ref.shape[0]
    xn = (_rms(x_ref[...]) * g_ref[...]).astype(BF16)
    low = lax.broadcasted_iota(jnp.int32, (tm, LANES), 1) < HEAD_DIM
    cos, sin_a, sin_b = c_ref[...], sa_ref[...], sb_ref[...]

    def norm_rope(y, gain):
        sq = y * y
        s_all = jnp.sum(sq, axis=-1, keepdims=True)
        s_low = jnp.sum(jnp.where(low, sq, 0.0), axis=-1, keepdims=True)
        ms = jnp.where(low, s_low, s_all - s_low) * (1.0 / HEAD_DIM)
        y = y * lax.rsqrt(ms + EPS) * gain
        return (y * cos + pltpu.roll(y, LANES - HEAD_DIM // 2, 1) * sin_a
                + pltpu.roll(y, HEAD_DIM // 2, 1) * sin_b)

    scale = HEAD_DIM ** -0.5 * LOG2E
    for c0 in range(COL_QA, COL_KA, LANES):
        y = _dot(xn, w_ref[:, c0:c0 + LANES])
        qk_ref[:, c0:c0 + LANES] = (norm_rope(y, gq_ref[...]) * scale).astype(BF16)
    y = _dot(xn, w_ref[:, COL_KA:COL_QB])
    qk_ref[:, COL_KA:COL_QB] = norm_rope(y, gk_ref[...]).astype(BF16)
    qk_ref[:, COL_QB:COL_KB] = (_dot(xn, w_ref[:, COL_QB:COL_KB]) * scale).astype(BF16)
    qk_ref[:, COL_KB:] = _dot(xn, w_ref[:, COL_KB:]).astype(BF16)
    vt_ref[...] = _dot_nt(wvt_ref[...], xn).astype(BF16)


def _gqa_kernel(q_ref, k_ref, vt_ref, o_ref):
    rep = A_HEADS // A_KV_HEADS
    head = lambda i: slice(i * HEAD_DIM, (i + 1) * HEAD_DIM)
    keys = [k_ref[:, head(g)] for g in range(A_KV_HEADS)]
    vt_ones = [_with_ones(vt_ref[head(g), :]) for g in range(A_KV_HEADS)]

    def scores(h):
        return _dot_nt(q_ref[:, head(h)], keys[h // rep])

    def finish(h, s):
        o_ref[head(h), :] = _softmax_pv(s, vt_ones[h // rep]).astype(o_ref.dtype)

    _pipelined(list(range(A_HEADS)), scores, finish)


def _dilated_kernel(q_ref, k_ref, vt_ref, rb_ref, onehot_ref, logm_ref, o_ref, e_ref):
    s_len = q_ref.shape[0]
    blk = ATTN_Q_TILE
    n_blk = s_len // blk
    reach = -(-max((w // (2 * d)) * d for w, d in B_BRANCHES) // blk)
    heads = LANES // HEAD_DIM
    e_cols = (2 * n_blk - 1) * blk

    @pl.when(pl.program_id(1) == 0)
    def _():
        for hh in range(heads):
            per_delta = jnp.sum(onehot_ref[...] * rb_ref[:, hh:hh + 1], axis=0, keepdims=True)
            per_delta = (per_delta + logm_ref[...]) * LOG2E
            spread = jnp.broadcast_to(per_delta, (blk, 2 * s_len))
            toeplitz = pltpu.roll(spread, 0, 1, stride=1, stride_axis=0)
            e_ref[hh] = toeplitz[:, blk:blk + e_cols]

    head = lambda hh: slice(hh * HEAD_DIM, (hh + 1) * HEAD_DIM)
    vt_ones = [_with_ones(vt_ref[head(hh), :]) for hh in range(heads)]

    def key_range(i):
        return max(0, i - reach) * blk, min(n_blk, i + reach + 1) * blk

    def scores(block):
        hh, i = block
        k_lo, k_hi = key_range(i)
        off = (n_blk - 1 - i) * blk
        s = _dot_nt(q_ref[i * blk:(i + 1) * blk, head(hh)], k_ref[k_lo:k_hi, head(hh)])
        return s + e_ref[hh, :, off + k_lo:off + k_hi]

    def finish(block, s):
        hh, i = block
        k_lo, k_hi = key_range(i)
        o = _softmax_pv(s, vt_ones[hh][:, k_lo:k_hi])
        o_ref[head(hh), i * blk:(i + 1) * blk] = o.astype(o_ref.dtype)

    _pipelined([(hh, i) for hh in range(heads) for i in range(n_blk)], scores, finish)


def _mlp_kernel(n_mix, final_norm, *refs):
    h_ref = refs[0]
    mix_refs = refs[1:1 + n_mix]
    wo_ref, g_ref, w1_ref, w2_ref = refs[1 + n_mix:5 + n_mix]
    gf_ref = refs[5 + n_mix] if final_norm else None
    out_ref = refs[-1]

    mix_t = jnp.concatenate([m_ref[...] for m_ref in mix_refs], axis=0)
    h = h_ref[...] + _dot_tn(mix_t, wo_ref[...])
    hn = (_rms(h) * g_ref[...]).astype(BF16)
    for c0 in range(0, D_FF, FF_CHUNK):
        a = jnp.maximum(_dot(hn, w1_ref[:, c0:c0 + FF_CHUNK]), 0.0)
        h = h + _dot((a * a).astype(BF16), w2_ref[c0:c0 + FF_CHUNK, :])
    if final_norm:
        h = _rms(h) * gf_ref[...]
    out_ref[...] = h


def _proj1_kernel(h_ref, g_ref, wd_ref, gq_ref, gkv_ref, wuq_ref, wuk_ref, wuvt_ref,
                  cq_ref, saq_ref, sbq_ref, ck_ref, sak_ref, sbk_ref,
                  q_ref, k_ref, vt_ref):
    xn = (_rms(h_ref[...]) * g_ref[...]).astype(BF16)
    hd = _dot(xn, wd_ref[...])
    c_q = (_rms(hd[:, :C_Q_RANK]) * gq_ref[...]).astype(BF16)
    c_kv = (_rms(hd[:, C_Q_RANK:C_Q_RANK + C_KV_RANK]) * gkv_ref[...]).astype(BF16)
    half = C_ROPE_DIM // 2

    def rope(y, cos, sin_a, sin_b):
        return y * cos + pltpu.roll(y, half, 1) * sin_a + pltpu.roll(y, LANES - half, 1) * sin_b

    cq, saq, sbq = cq_ref[...], saq_ref[...], sbq_ref[...]
    k_rope = rope(hd[:, C_Q_RANK + C_KV_RANK:], ck_ref[...], sak_ref[...], sbk_ref[...])
    for hh in range(C_HEADS):
        lanes = slice(hh * LANES, (hh + 1) * LANES)
        q_ref[:, lanes] = rope(_dot(c_q, wuq_ref[:, lanes]), cq, saq, sbq).astype(BF16)
        k_ref[:, lanes] = (_dot(c_kv, wuk_ref[:, lanes]) + k_rope).astype(BF16)
    vt_ref[...] = _dot_nt(wuvt_ref[...], c_kv).astype(BF16)


def _mla_kernel(q_ref, k_ref, vt_ref, o_ref):
    s_len = q_ref.shape[0]
    tq = ATTN_Q_TILE
    v_rows = lambda hh: slice(hh * C_V_DIM, (hh + 1) * C_V_DIM)
    vt_ones = [_with_ones(vt_ref[v_rows(hh), :]) for hh in range(2)]

    def scores(block):
        hh, i = block
        lanes = slice(hh * LANES, (hh + 1) * LANES)
        return _dot_nt(q_ref[i * tq:(i + 1) * tq, lanes], k_ref[:, lanes])

    def finish(block, s):
        hh, i = block
        o_ref[v_rows(hh), i * tq:(i + 1) * tq] = _softmax_pv(s, vt_ones[hh]).astype(o_ref.dtype)

    _pipelined([(hh, i) for hh in range(2) for i in range(s_len // tq)], scores, finish)


def _rope_angles(pos, dim):
    inv_freq = ROPE_THETA ** (-np.arange(0, dim, 2, dtype=np.float64) / dim)
    return pos.astype(np.float64)[:, None] * inv_freq[None, :]


def _axial_tables(s_len):
    t = np.arange(s_len)
    ang = np.concatenate([_rope_angles(t // GRID_W, HEAD_DIM // 2),
                          _rope_angles(t % GRID_W, HEAD_DIM // 2)], axis=-1)
    cos, sin, zero = np.cos(ang), np.sin(ang), np.zeros_like(ang)
    c = np.concatenate([cos, cos, cos, cos], axis=-1)
    sa = np.concatenate([-sin, zero, -sin, zero], axis=-1)
    sb = np.concatenate([zero, sin, zero, sin], axis=-1)
    return [jnp.asarray(a, F32) for a in (c, sa, sb)]


def _mla_tables(s_len, scale):
    ang = _rope_angles(np.arange(s_len), C_ROPE_DIM)
    cos, sin = np.cos(ang), np.sin(ang)
    half = C_ROPE_DIM // 2
    ones = np.ones((s_len, C_NOPE_DIM))
    z_nope = np.zeros((s_len, C_NOPE_DIM))
    z_half = np.zeros((s_len, half))
    z_tail = np.zeros((s_len, LANES - C_QK_DIM))
    c = np.concatenate([ones, cos, cos, z_tail], axis=-1)
    sa = np.concatenate([z_nope, z_half, sin, z_tail], axis=-1)
    sb = np.concatenate([z_nope, -sin, z_half, z_tail], axis=-1)
    return [jnp.asarray(a * scale, F32) for a in (c, sa, sb)]


def _t5_bucket(rel):
    nb = NUM_BUCKETS // 2
    max_exact = nb // 2
    base = np.where(rel > 0, nb, 0)
    n = np.abs(rel)
    nf = np.maximum(n, 1).astype(np.float32)
    large = max_exact + (np.log(nf / np.float32(max_exact)) / np.float32(math.log(REL_MAX_DISTANCE / max_exact))
                         * np.float32(nb - max_exact)).astype(np.int32)
    large = np.minimum(large, nb - 1)
    return base + np.where(n < max_exact, n, large)


def _dilated_structure(s_len):
    delta = np.arange(2 * s_len) - s_len
    mult = np.zeros(delta.shape, np.int32)
    for w, d in B_BRANCHES:
        n_side = w // (2 * d)
        mult += ((delta % d == 0) & (np.abs(delta) <= n_side * d)).astype(np.int32)
    log_mult = np.where(mult > 0, np.log(np.maximum(mult, 1).astype(np.float64)), NEG_INF)
    onehot = (_t5_bucket(delta)[None, :] == np.arange(NUM_BUCKETS)[:, None]) & (mult > 0)[None, :]
    return jnp.asarray(onehot, F32), jnp.asarray(log_mult[None, :], F32)


def _params(*sem):
    return pltpu.CompilerParams(dimension_semantics=sem, vmem_limit_bytes=VMEM_LIMIT_BYTES)


def _resident(shape):
    return pl.BlockSpec(shape, lambda *_: (0,) * len(shape), pipeline_mode=pl.Buffered(1))


def _row2(v):
    return v.reshape(1, -1).astype(F32)


def _mlp_call(h, mixes_t, seq_tiles, wo, g, w1, w2, gf, name):
    n_tok = h.shape[0]
    tm = TOKEN_TILE
    tok = lambda w: pl.BlockSpec((tm, w), lambda i: (i, 0))
    tok_t = lambda w: pl.BlockSpec((None, w, tm), lambda i: (i // seq_tiles, 0, i % seq_tiles))
    in_specs = [tok(D_MODEL)] + [tok_t(m.shape[1]) for m in mixes_t] + [
        _resident(wo.shape), _resident((1, D_MODEL)), _resident(w1.shape), _resident(w2.shape)]
    args = [h, *mixes_t, wo, _row2(g), w1, w2]
    if gf is not None:
        in_specs.append(_resident((1, D_MODEL)))
        args.append(_row2(gf))
    return pl.pallas_call(
        functools.partial(_mlp_kernel, len(mixes_t), gf is not None),
        out_shape=jax.ShapeDtypeStruct((n_tok, D_MODEL), F32),
        grid=(n_tok // tm,),
        in_specs=in_specs,
        out_specs=tok(D_MODEL),
        compiler_params=_params("parallel"),
        name=name,
    )(*args)


def kernel(x, norm_mix_g, norm_mlp_g, ab_w_in, a_q_norm_g, a_k_norm_g, ab_w_out, rel_bias,
           c_w_down, c_q_norm_g, c_kv_norm_g, c_w_uq, c_w_ukv, c_w_out, mlp_w1, mlp_w2,
           final_norm_g):
    b, s, d = x.shape
    n_tok = b * s
    tm = TOKEN_TILE
    seq_tiles = s // tm
    h0 = x.reshape(n_tok, d)
    tok = lambda w: pl.BlockSpec((tm, w), lambda i: (i, 0))
    tok_t = lambda w: pl.BlockSpec((None, w, tm), lambda i: (i // seq_tiles, 0, i % seq_tiles))
    tab_spec = pl.BlockSpec((tm, LANES), lambda i: (i % seq_tiles, 0))

    tables0 = _axial_tables(s)
    gq = _row2(jnp.tile(a_q_norm_g[0], LANES // HEAD_DIM))
    gk = _row2(jnp.tile(a_k_norm_g[0], LANES // HEAD_DIM))
    w_in = ab_w_in[0]
    q_a, k_a, v_a, q_b, k_b, v_b = jnp.split(
        w_in, np.cumsum([A_Q_W, A_KV_W, A_KV_W, B_W, B_W]).tolist(), axis=1)
    w_qk = jnp.concatenate([q_a, k_a, q_b, k_b], axis=1).astype(BF16)
    w_vt = jnp.concatenate([v_a, v_b], axis=1).T.astype(BF16)
    qk0, vt0 = pl.pallas_call(
        _proj0_kernel,
        out_shape=(jax.ShapeDtypeStruct((n_tok, QK0_W), BF16),
                   jax.ShapeDtypeStruct((b, VT0_ROWS, s), BF16)),
        grid=(n_tok // tm,),
        in_specs=[tok(d), _resident((1, d)), _resident(w_qk.shape), _resident(w_vt.shape),
                  _resident((1, LANES)), _resident((1, LANES)), tab_spec, tab_spec, tab_spec],
        out_specs=(tok(QK0_W), tok_t(VT0_ROWS)),
        compiler_params=_params("parallel"),
        name="proj0",
    )(h0, _row2(norm_mix_g[0]), w_qk, w_vt, gq, gk, *tables0)

    tq = ATTN_Q_TILE
    q_tiles = s // tq
    o_a = pl.pallas_call(
        _gqa_kernel,
        out_shape=jax.ShapeDtypeStruct((b, A_Q_W, s), BF16),
        grid=(b, q_tiles),
        in_specs=[pl.BlockSpec((tq, A_Q_W), lambda bi, qi: (bi * q_tiles + qi, COL_QA // A_Q_W)),
                  pl.BlockSpec((s, A_KV_W), lambda bi, qi: (bi, COL_KA // A_KV_W)),
                  pl.BlockSpec((None, A_KV_W, s), lambda bi, qi: (bi, ROW_VA // A_KV_W, 0))],
        out_specs=pl.BlockSpec((None, A_Q_W, tq), lambda bi, qi: (bi, 0, qi)),
        compiler_params=_params("parallel", "parallel"),
        name="gqa",
    )(qk0, qk0, vt0)

    onehot, log_mult = _dilated_structure(s)
    pairs = B_W // LANES
    heads_per_pair = LANES // HEAD_DIM
    rb = rel_bias.astype(F32).reshape(NUM_BUCKETS, pairs, heads_per_pair).transpose(1, 0, 2)
    o_b = pl.pallas_call(
        _dilated_kernel,
        out_shape=jax.ShapeDtypeStruct((b, B_W, s), BF16),
        grid=(pairs, b),
        in_specs=[pl.BlockSpec((s, LANES), lambda p, bi: (bi, COL_QB // LANES + p)),
                  pl.BlockSpec((s, LANES), lambda p, bi: (bi, COL_KB // LANES + p)),
                  pl.BlockSpec((None, LANES, s), lambda p, bi: (bi, ROW_VB // LANES + p, 0)),
                  pl.BlockSpec((None, NUM_BUCKETS, heads_per_pair), lambda p, bi: (p, 0, 0)),
                  _resident(onehot.shape), _resident(log_mult.shape)],
        out_specs=pl.BlockSpec((None, LANES, s), lambda p, bi: (bi, p, 0)),
        scratch_shapes=[pltpu.VMEM((heads_per_pair, tq, (2 * q_tiles - 1) * tq), F32)],
        compiler_params=_params("parallel", "arbitrary"),
        name="dilated",
    )(qk0, qk0, vt0, rb, onehot, log_mult)

    h1 = _mlp_call(h0, [o_a, o_b], seq_tiles, ab_w_out[0].astype(BF16), norm_mlp_g[0],
                   mlp_w1[0].astype(BF16), mlp_w2[0].astype(BF16), None, "mix0_mlp0")

    scale_c = C_QK_DIM ** -0.5 * LOG2E
    tables_q = _mla_tables(s, scale_c)
    tables_k = _mla_tables(s, 1.0)
    zeros = lambda r, c: jnp.zeros((r, c), F32)
    wd = c_w_down[0]
    kv_end = C_Q_RANK + C_KV_RANK
    wd_p = jnp.concatenate([wd[:, :kv_end], zeros(d, C_NOPE_DIM), wd[:, kv_end:],
                            zeros(d, LANES - C_QK_DIM)], axis=1).astype(BF16)
    wuq = c_w_uq[0].reshape(C_Q_RANK, C_HEADS, C_QK_DIM)
    wuq_p = jnp.pad(wuq, ((0, 0), (0, 0), (0, LANES - C_QK_DIM))).reshape(
        C_Q_RANK, C_HEADS * LANES).astype(BF16)
    wukv = c_w_ukv[0].reshape(C_KV_RANK, C_HEADS, C_NOPE_DIM + C_V_DIM)
    wuk_p = jnp.pad(wukv[:, :, :C_NOPE_DIM], ((0, 0), (0, 0), (0, LANES - C_NOPE_DIM))).reshape(
        C_KV_RANK, C_HEADS * LANES).astype(BF16)
    wuv_t = wukv[:, :, C_NOPE_DIM:].reshape(C_KV_RANK, C_HEADS * C_V_DIM).T.astype(BF16)

    qk_w = C_HEADS * LANES
    v_w = C_HEADS * C_V_DIM
    q_c, k_c, vt_c = pl.pallas_call(
        _proj1_kernel,
        out_shape=(jax.ShapeDtypeStruct((n_tok, qk_w), BF16),
                   jax.ShapeDtypeStruct((n_tok, qk_w), BF16),
                   jax.ShapeDtypeStruct((b, v_w, s), BF16)),
        grid=(n_tok // tm,),
        in_specs=[tok(d), _resident((1, d)), _resident(wd_p.shape),
                  _resident((1, C_Q_RANK)), _resident((1, C_KV_RANK)),
                  _resident(wuq_p.shape), _resident(wuk_p.shape), _resident(wuv_t.shape)]
                 + [tab_spec] * 6,
        out_specs=(tok(qk_w), tok(qk_w), tok_t(v_w)),
        compiler_params=_params("parallel"),
        name="proj1",
    )(h1, _row2(norm_mix_g[1]), wd_p, _row2(c_q_norm_g[0]), _row2(c_kv_norm_g[0]),
      wuq_p, wuk_p, wuv_t, *tables_q, *tables_k)

    o_c = pl.pallas_call(
        _mla_kernel,
        out_shape=jax.ShapeDtypeStruct((b, v_w, s), BF16),
        grid=(b, C_HEADS // 2),
        in_specs=[pl.BlockSpec((s, 2 * LANES), lambda bi, p: (bi, p)),
                  pl.BlockSpec((s, 2 * LANES), lambda bi, p: (bi, p)),
                  pl.BlockSpec((None, 2 * C_V_DIM, s), lambda bi, p: (bi, p, 0))],
        out_specs=pl.BlockSpec((None, 2 * C_V_DIM, s), lambda bi, p: (bi, p, 0)),
        compiler_params=_params("parallel", "parallel"),
        name="mla",
    )(q_c, k_c, vt_c)

    out = _mlp_call(h1, [o_c], seq_tiles, c_w_out[0].astype(BF16), norm_mlp_g[1],
                    mlp_w1[1].astype(BF16), mlp_w2[1].astype(BF16), final_norm_g, "mix1_mlp1")
    return out.reshape(b, s, d)
```

```python
import functools
import math

import jax
import jax.numpy as jnp
import numpy as np
from jax import lax
from jax.experimental import pallas as pl
from jax.experimental.pallas import tpu as pltpu

F32 = jnp.float32
BF16 = jnp.bfloat16

D_MODEL = 1024
GRID_W = 64
HEAD_DIM = 64
ROPE_THETA = 10000.0
EPS = 1e-6
NEG_INF = -1e30
LOG2E = math.log2(math.e)

A_HEADS = 8
A_KV_HEADS = 2
B_HEADS = 8
B_BRANCHES = ((128, 1), (512, 4), (2048, 16))
NUM_BUCKETS = 32
REL_MAX_DISTANCE = 1024

C_HEADS = 16
C_Q_RANK = 256
C_KV_RANK = 128
C_NOPE_DIM = 64
C_ROPE_DIM = 32
C_V_DIM = 64
C_QK_DIM = C_NOPE_DIM + C_ROPE_DIM
D_FF = 4 * D_MODEL

A_Q_W = A_HEADS * HEAD_DIM
A_KV_W = A_KV_HEADS * HEAD_DIM
B_W = B_HEADS * HEAD_DIM

LANES = 128
VMEM_LIMIT_BYTES = 56 * 1024 * 1024

PROJ0_TOKEN_TILE = 512
PROJ1_TOKEN_TILE = 1024
MLP_TOKEN_TILE = 512
ATTN_Q_TILE = 256
FF_CHUNK = 1024

COL_QA = 0
COL_KA = COL_QA + A_Q_W
COL_QB = COL_KA + A_KV_W
COL_KB = COL_QB + B_W
QK0_W = COL_KB + B_W
ROW_VA = 0
ROW_VB = A_KV_W
VT0_ROWS = ROW_VB + B_W


def _rms(x):
    return x * lax.rsqrt(jnp.mean(x * x, axis=-1, keepdims=True) + EPS)


def _dot(a, b):
    return jnp.dot(a, b, preferred_element_type=F32)


def _dot_nt(a, b):
    return lax.dot_general(a, b, (((1,), (1,)), ((), ())), preferred_element_type=F32)


def _dot_tn(a, b):
    return lax.dot_general(a, b, (((0,), (0,)), ((), ())), preferred_element_type=F32)


def _with_ones(vt):
    return jnp.concatenate([vt, jnp.ones_like(vt)], axis=0)


def _softmax_pv(s, vt_ones):
    dv = vt_ones.shape[0] // 2
    m = jnp.max(s, axis=-1, keepdims=True)
    p = jnp.exp2(s - m).astype(BF16)
    acc = _dot(vt_ones, p.T)
    return acc[:dv] / acc[dv:dv + 1]


def _pipelined(blocks, scores, finish):
    s_next = scores(blocks[0])
    for n, blk in enumerate(blocks):
        s_cur = s_next
        if n + 1 < len(blocks):
            s_next = scores(blocks[n + 1])
        finish(blk, s_cur)


def _proj0_kernel(x_ref, g_ref, w_ref, wvt_ref, gq_ref, gk_ref, c_ref, sa_ref, sb_ref,
                  qk_ref, vt_ref):
    tm = x_ref.shape[0]
    xn = (_rms(x_ref[...]) * g_ref[...]).astype(BF16)
    low = lax.broadcasted_iota(jnp.int32, (tm, LANES), 1) < HEAD_DIM
    cos, sin_a, sin_b = c_ref[...], sa_ref[...], sb_ref[...]

    def norm_rope(y, gain):
        sq = y * y
        s_all = jnp.sum(sq, axis=-1, keepdims=True)
        s_low = jnp.sum(jnp.where(low, sq, 0.0), axis=-1, keepdims=True)
        ms = jnp.where(low, s_low, s_all - s_low) * (1.0 / HEAD_DIM)
        y = y * lax.rsqrt(ms + EPS) * gain
        return (y * cos + pltpu.roll(y, LANES - HEAD_DIM // 2, 1) * sin_a
                + pltpu.roll(y, HEAD_DIM // 2, 1) * sin_b)

    scale = HEAD_DIM ** -0.5 * LOG2E
    y = _dot(xn, w_ref[:, COL_QA:COL_QB])
    for c0 in range(COL_QA, COL_KA, LANES):
        qk_ref[:, c0:c0 + LANES] = (norm_rope(y[:, c0:c0 + LANES], gq_ref[...]) * scale).astype(BF16)
    qk_ref[:, COL_KA:COL_QB] = norm_rope(y[:, COL_KA:COL_QB], gk_ref[...]).astype(BF16)
    qk_ref[:, COL_QB:COL_KB] = (_dot(xn, w_ref[:, COL_QB:COL_KB]) * scale).astype(BF16)
    qk_ref[:, COL_KB:] = _dot(xn, w_ref[:, COL_KB:]).astype(BF16)
    vt_ref[...] = _dot_nt(wvt_ref[...], xn).astype(BF16)


def _gqa_kernel(q_ref, k_ref, vt_ref, o_ref):
    rep = A_HEADS // A_KV_HEADS
    head = lambda i: slice(i * HEAD_DIM, (i + 1) * HEAD_DIM)
    keys = [k_ref[:, head(g)] for g in range(A_KV_HEADS)]
    vt_ones = [_with_ones(vt_ref[head(g), :]) for g in range(A_KV_HEADS)]

    def scores(h):
        return _dot_nt(q_ref[:, head(h)], keys[h // rep])

    def finish(h, s):
        o_ref[head(h), :] = _softmax_pv(s, vt_ones[h // rep]).astype(o_ref.dtype)

    _pipelined(list(range(A_HEADS)), scores, finish)


def _dilated_kernel(q_ref, k_ref, vt_ref, rb_ref, onehot_ref, logm_ref, o_ref, e_ref):
    s_len = q_ref.shape[0]
    blk = ATTN_Q_TILE
    n_blk = s_len // blk
    reach = -(-max((w // (2 * d)) * d for w, d in B_BRANCHES) // blk)
    heads = LANES // HEAD_DIM
    e_cols = (2 * n_blk - 1) * blk

    @pl.when(pl.program_id(1) == 0)
    def _():
        for hh in range(heads):
            per_delta = jnp.sum(onehot_ref[...] * rb_ref[:, hh:hh + 1], axis=0, keepdims=True)
            per_delta = (per_delta + logm_ref[...]) * LOG2E
            spread = jnp.broadcast_to(per_delta, (blk, 2 * s_len))
            toeplitz = pltpu.roll(spread, 0, 1, stride=1, stride_axis=0)
            e_ref[hh] = toeplitz[:, blk:blk + e_cols]

    head = lambda hh: slice(hh * HEAD_DIM, (hh + 1) * HEAD_DIM)
    vt_ones = [_with_ones(vt_ref[head(hh), :]) for hh in range(heads)]

    def key_range(i):
        return max(0, i - reach) * blk, min(n_blk, i + reach + 1) * blk

    def scores(block):
        hh, i = block
        k_lo, k_hi = key_range(i)
        off = (n_blk - 1 - i) * blk
        s = _dot_nt(q_ref[i * blk:(i + 1) * blk, head(hh)], k_ref[k_lo:k_hi, head(hh)])
        return s + e_ref[hh, :, off + k_lo:off + k_hi]

    def finish(block, s):
        hh, i = block
        k_lo, k_hi = key_range(i)
        o = _softmax_pv(s, vt_ones[hh][:, k_lo:k_hi])
        o_ref[head(hh), i * blk:(i + 1) * blk] = o.astype(o_ref.dtype)

    _pipelined([(hh, i) for hh in range(heads) for i in range(n_blk)], scores, finish)


def _mlp_kernel(n_mix, final_norm, *refs):
    h_ref = refs[0]
    mix_refs = refs[1:1 + n_mix]
    wo_ref, g_ref, w1_ref, w2_ref = refs[1 + n_mix:5 + n_mix]
    gf_ref = refs[5 + n_mix] if final_norm else None
    out_ref = refs[-1]

    mix_t = jnp.concatenate([m_ref[...] for m_ref in mix_refs], axis=0)
    h = h_ref[...] + _dot_tn(mix_t, wo_ref[...])
    hn = (_rms(h) * g_ref[...]).astype(BF16)
    for c0 in range(0, D_FF, FF_CHUNK):
        a = jnp.maximum(_dot(hn, w1_ref[:, c0:c0 + FF_CHUNK]), 0.0)
        h = h + _dot((a * a).astype(BF16), w2_ref[c0:c0 + FF_CHUNK, :])
    if final_norm:
        h = _rms(h) * gf_ref[...]
    out_ref[...] = h


def _proj1_kernel(h_ref, g_ref, wd_ref, gq_ref, gkv_ref, wuq_ref, wuk_ref, wuvt_ref,
                  cq_ref, sq_ref, ck_ref, sk_ref, q_ref, k_ref, vt_ref):
    xn = (_rms(h_ref[...]) * g_ref[...]).astype(BF16)
    hd = _dot(xn, wd_ref[...])
    c_q = (_rms(hd[:, :C_Q_RANK]) * gq_ref[...]).astype(BF16)
    c_kv = (_rms(hd[:, C_Q_RANK:C_Q_RANK + C_KV_RANK]) * gkv_ref[...]).astype(BF16)

    def rope(y, cos, sin):
        return y * cos + pltpu.roll(y, LANES - C_ROPE_DIM, 1) * sin

    cq, sq = cq_ref[...], sq_ref[...]
    k_rope = rope(hd[:, C_Q_RANK + C_KV_RANK:], ck_ref[...], sk_ref[...])
    for c0 in range(0, C_HEADS * LANES, 2 * LANES):
        q2 = _dot(c_q, wuq_ref[:, c0:c0 + 2 * LANES])
        k2 = _dot(c_kv, wuk_ref[:, c0:c0 + 2 * LANES])
        for lo in (0, LANES):
            lanes = slice(c0 + lo, c0 + lo + LANES)
            q_ref[:, lanes] = rope(q2[:, lo:lo + LANES], cq, sq).astype(BF16)
            k_ref[:, lanes] = (k2[:, lo:lo + LANES] + k_rope).astype(BF16)
    vt_ref[...] = _dot_nt(wuvt_ref[...], c_kv).astype(BF16)


def _mla_kernel(q_ref, k_ref, vt_ref, o_ref):
    s_len = q_ref.shape[0]
    tq = ATTN_Q_TILE
    v_rows = lambda hh: slice(hh * C_V_DIM, (hh + 1) * C_V_DIM)
    vt_ones = [_with_ones(vt_ref[v_rows(hh), :]) for hh in range(2)]

    def scores(block):
        hh, i = block
        lanes = slice(hh * LANES, (hh + 1) * LANES)
        return _dot_nt(q_ref[i * tq:(i + 1) * tq, lanes], k_ref[:, lanes])

    def finish(block, s):
        hh, i = block
        o_ref[v_rows(hh), i * tq:(i + 1) * tq] = _softmax_pv(s, vt_ones[hh]).astype(o_ref.dtype)

    _pipelined([(hh, i) for hh in range(2) for i in range(s_len // tq)], scores, finish)


def _rope_angles(pos, dim):
    inv_freq = ROPE_THETA ** (-np.arange(0, dim, 2, dtype=np.float64) / dim)
    return pos.astype(np.float64)[:, None] * inv_freq[None, :]


def _axial_tables(s_len):
    t = np.arange(s_len)
    ang = np.concatenate([_rope_angles(t // GRID_W, HEAD_DIM // 2),
                          _rope_angles(t % GRID_W, HEAD_DIM // 2)], axis=-1)
    cos, sin, zero = np.cos(ang), np.sin(ang), np.zeros_like(ang)
    c = np.concatenate([cos, cos, cos, cos], axis=-1)
    sa = np.concatenate([-sin, zero, -sin, zero], axis=-1)
    sb = np.concatenate([zero, sin, zero, sin], axis=-1)
    return [jnp.asarray(a, F32) for a in (c, sa, sb)]


def _mla_tables(s_len, scale):
    ang = _rope_angles(np.arange(s_len), C_ROPE_DIM)
    cos, sin = np.cos(ang), np.sin(ang)
    ones = np.ones((s_len, C_NOPE_DIM))
    z_nope = np.zeros((s_len, C_NOPE_DIM))
    z_tail = np.zeros((s_len, LANES - C_QK_DIM))
    c = np.concatenate([ones, cos, cos, z_tail], axis=-1)
    sn = np.concatenate([z_nope, sin, sin, z_tail], axis=-1)
    return [jnp.asarray(a * scale, F32) for a in (c, sn)]


def _rotate_half_cols(w):
    half = w.shape[-1] // 2
    return jnp.concatenate([-w[..., half:], w[..., :half]], axis=-1)


def _t5_bucket(rel):
    nb = NUM_BUCKETS // 2
    max_exact = nb // 2
    base = np.where(rel > 0, nb, 0)
    n = np.abs(rel)
    nf = np.maximum(n, 1).astype(np.float32)
    large = max_exact + (np.log(nf / np.float32(max_exact)) / np.float32(math.log(REL_MAX_DISTANCE / max_exact))
                         * np.float32(nb - max_exact)).astype(np.int32)
    large = np.minimum(large, nb - 1)
    return base + np.where(n < max_exact, n, large)


def _dilated_structure(s_len):
    delta = np.arange(2 * s_len) - s_len
    mult = np.zeros(delta.shape, np.int32)
    for w, d in B_BRANCHES:
        n_side = w // (2 * d)
        mult += ((delta % d == 0) & (np.abs(delta) <= n_side * d)).astype(np.int32)
    log_mult = np.where(mult > 0, np.log(np.maximum(mult, 1).astype(np.float64)), NEG_INF)
    onehot = (_t5_bucket(delta)[None, :] == np.arange(NUM_BUCKETS)[:, None]) & (mult > 0)[None, :]
    return jnp.asarray(onehot, F32), jnp.asarray(log_mult[None, :], F32)


def _params(*sem):
    return pltpu.CompilerParams(dimension_semantics=sem, vmem_limit_bytes=VMEM_LIMIT_BYTES)


def _resident(shape):
    return pl.BlockSpec(shape, lambda *_: (0,) * len(shape), pipeline_mode=pl.Buffered(1))


def _row2(v):
    return v.reshape(1, -1).astype(F32)


def _token_specs(tm, s_len):
    seq_tiles = s_len // tm
    tok = lambda w: pl.BlockSpec((tm, w), lambda i: (i, 0))
    tok_t = lambda w: pl.BlockSpec((None, w, tm), lambda i: (i // seq_tiles, 0, i % seq_tiles))
    table = pl.BlockSpec((tm, LANES), lambda i: (i % seq_tiles, 0))
    return tok, tok_t, table


def _mlp_call(h, mixes_t, wo, g, w1, w2, gf, name):
    n_tok = h.shape[0]
    tm = MLP_TOKEN_TILE
    tok, tok_t, _ = _token_specs(tm, mixes_t[0].shape[2])
    in_specs = [tok(D_MODEL)] + [tok_t(m.shape[1]) for m in mixes_t] + [
        _resident(wo.shape), _resident((1, D_MODEL)), _resident(w1.shape), _resident(w2.shape)]
    args = [h, *mixes_t, wo, _row2(g), w1, w2]
    if gf is not None:
        in_specs.append(_resident((1, D_MODEL)))
        args.append(_row2(gf))
    return pl.pallas_call(
        functools.partial(_mlp_kernel, len(mixes_t), gf is not None),
        out_shape=jax.ShapeDtypeStruct((n_tok, D_MODEL), F32),
        grid=(n_tok // tm,),
        in_specs=in_specs,
        out_specs=tok(D_MODEL),
        compiler_params=_params("parallel"),
        name=name,
    )(*args)


def kernel(x, norm_mix_g, norm_mlp_g, ab_w_in, a_q_norm_g, a_k_norm_g, ab_w_out, rel_bias,
           c_w_down, c_q_norm_g, c_kv_norm_g, c_w_uq, c_w_ukv, c_w_out, mlp_w1, mlp_w2,
           final_norm_g):
    b, s, d = x.shape
    n_tok = b * s
    h0 = x.reshape(n_tok, d)
    tm = PROJ0_TOKEN_TILE
    tok, tok_t, tab_spec = _token_specs(tm, s)

    tables0 = _axial_tables(s)
    gq = _row2(jnp.tile(a_q_norm_g[0], LANES // HEAD_DIM))
    gk = _row2(jnp.tile(a_k_norm_g[0], LANES // HEAD_DIM))
    w_in = ab_w_in[0]
    q_a, k_a, v_a, q_b, k_b, v_b = jnp.split(
        w_in, np.cumsum([A_Q_W, A_KV_W, A_KV_W, B_W, B_W]).tolist(), axis=1)
    w_qk = jnp.concatenate([q_a, k_a, q_b, k_b], axis=1).astype(BF16)
    w_vt = jnp.concatenate([v_a, v_b], axis=1).T.astype(BF16)
    qk0, vt0 = pl.pallas_call(
        _proj0_kernel,
        out_shape=(jax.ShapeDtypeStruct((n_tok, QK0_W), BF16),
                   jax.ShapeDtypeStruct((b, VT0_ROWS, s), BF16)),
        grid=(n_tok // tm,),
        in_specs=[tok(d), _resident((1, d)), _resident(w_qk.shape), _resident(w_vt.shape),
                  _resident((1, LANES)), _resident((1, LANES)), tab_spec, tab_spec, tab_spec],
        out_specs=(tok(QK0_W), tok_t(VT0_ROWS)),
        compiler_params=_params("parallel"),
        name="proj0",
    )(h0, _row2(norm_mix_g[0]), w_qk, w_vt, gq, gk, *tables0)

    tq = ATTN_Q_TILE
    q_tiles = s // tq
    o_a = pl.pallas_call(
        _gqa_kernel,
        out_shape=jax.ShapeDtypeStruct((b, A_Q_W, s), BF16),
        grid=(b, q_tiles),
        in_specs=[pl.BlockSpec((tq, A_Q_W), lambda bi, qi: (bi * q_tiles + qi, COL_QA // A_Q_W)),
                  pl.BlockSpec((s, A_KV_W), lambda bi, qi: (bi, COL_KA // A_KV_W)),
                  pl.BlockSpec((None, A_KV_W, s), lambda bi, qi: (bi, ROW_VA // A_KV_W, 0))],
        out_specs=pl.BlockSpec((None, A_Q_W, tq), lambda bi, qi: (bi, 0, qi)),
        compiler_params=_params("parallel", "parallel"),
        name="gqa",
    )(qk0, qk0, vt0)

    onehot, log_mult = _dilated_structure(s)
    pairs = B_W // LANES
    heads_per_pair = LANES // HEAD_DIM
    rb = rel_bias.astype(F32).reshape(NUM_BUCKETS, pairs, heads_per_pair).transpose(1, 0, 2)
    o_b = pl.pallas_call(
        _dilated_kernel,
        out_shape=jax.ShapeDtypeStruct((b, B_W, s), BF16),
        grid=(pairs, b),
        in_specs=[pl.BlockSpec((s, LANES), lambda p, bi: (bi, COL_QB // LANES + p)),
                  pl.BlockSpec((s, LANES), lambda p, bi: (bi, COL_KB // LANES + p)),
                  pl.BlockSpec((None, LANES, s), lambda p, bi: (bi, ROW_VB // LANES + p, 0)),
                  pl.BlockSpec((None, NUM_BUCKETS, heads_per_pair), lambda p, bi: (p, 0, 0)),
                  _resident(onehot.shape), _resident(log_mult.shape)],
        out_specs=pl.BlockSpec((None, LANES, s), lambda p, bi: (bi, p, 0)),
        scratch_shapes=[pltpu.VMEM((heads_per_pair, tq, (2 * q_tiles - 1) * tq), F32)],
        compiler_params=_params("parallel", "arbitrary"),
        name="dilated",
    )(qk0, qk0, vt0, rb, onehot, log_mult)

    h1 = _mlp_call(h0, [o_a, o_b], ab_w_out[0].astype(BF16), norm_mlp_g[0],
                   mlp_w1[0].astype(BF16), mlp_w2[0].astype(BF16), None, "mix0_mlp0")

    scale_c = C_QK_DIM ** -0.5 * LOG2E
    tables_q = _mla_tables(s, scale_c)
    tables_k = _mla_tables(s, 1.0)
    zeros = lambda r, c: jnp.zeros((r, c), F32)
    wd = c_w_down[0]
    kv_end = C_Q_RANK + C_KV_RANK
    wd_p = jnp.concatenate([wd[:, :kv_end], zeros(d, C_NOPE_DIM), wd[:, kv_end:],
                            _rotate_half_cols(wd[:, kv_end:])], axis=1).astype(BF16)
    wuq = c_w_uq[0].reshape(C_Q_RANK, C_HEADS, C_QK_DIM)
    wuq_p = jnp.concatenate([wuq, _rotate_half_cols(wuq[:, :, C_NOPE_DIM:])], axis=-1).reshape(
        C_Q_RANK, C_HEADS * LANES).astype(BF16)
    wukv = c_w_ukv[0].reshape(C_KV_RANK, C_HEADS, C_NOPE_DIM + C_V_DIM)
    wuk_p = jnp.pad(wukv[:, :, :C_NOPE_DIM], ((0, 0), (0, 0), (0, LANES - C_NOPE_DIM))).reshape(
        C_KV_RANK, C_HEADS * LANES).astype(BF16)
    wuv_t = wukv[:, :, C_NOPE_DIM:].reshape(C_KV_RANK, C_HEADS * C_V_DIM).T.astype(BF16)

    qk_w = C_HEADS * LANES
    v_w = C_HEADS * C_V_DIM
    tm = PROJ1_TOKEN_TILE
    tok, tok_t, tab_spec = _token_specs(tm, s)
    q_c, k_c, vt_c = pl.pallas_call(
        _proj1_kernel,
        out_shape=(jax.ShapeDtypeStruct((n_tok, qk_w), BF16),
                   jax.ShapeDtypeStruct((n_tok, qk_w), BF16),
                   jax.ShapeDtypeStruct((b, v_w, s), BF16)),
        grid=(n_tok // tm,),
        in_specs=[tok(d), _resident((1, d)), _resident(wd_p.shape),
                  _resident((1, C_Q_RANK)), _resident((1, C_KV_RANK)),
                  _resident(wuq_p.shape), _resident(wuk_p.shape), _resident(wuv_t.shape)]
                 + [tab_spec] * 4,
        out_specs=(tok(qk_w), tok(qk_w), tok_t(v_w)),
        compiler_params=_params("parallel"),
        name="proj1",
    )(h1, _row2(norm_mix_g[1]), wd_p, _row2(c_q_norm_g[0]), _row2(c_kv_norm_g[0]),
      wuq_p, wuk_p, wuv_t, *tables_q, *tables_k)

    o_c = pl.pallas_call(
        _mla_kernel,
        out_shape=jax.ShapeDtypeStruct((b, v_w, s), BF16),
        grid=(b, C_HEADS // 2),
        in_specs=[pl.BlockSpec((s, 2 * LANES), lambda bi, p: (bi, p)),
                  pl.BlockSpec((s, 2 * LANES), lambda bi, p: (bi, p)),
                  pl.BlockSpec((None, 2 * C_V_DIM, s), lambda bi, p: (bi, p, 0))],
        out_specs=pl.BlockSpec((None, 2 * C_V_DIM, s), lambda bi, p: (bi, p, 0)),
        compiler_params=_params("parallel", "parallel"),
        name="mla",
    )(q_c, k_c, vt_c)

    out = _mlp_call(h1, [o_c], c_w_out[0].astype(BF16), norm_mlp_g[1],
                    mlp_w1[1].astype(BF16), mlp_w2[1].astype(BF16), final_norm_g, "mix1_mlp1")
    return out.reshape(b, s, d)
```

```python
import functools
import math

import jax
import jax.numpy as jnp
import numpy as np
from jax import lax
from jax.experimental import pallas as pl
from jax.experimental.pallas import tpu as pltpu

F32 = jnp.float32
BF16 = jnp.bfloat16

D_MODEL = 1024
GRID_W = 64
HEAD_DIM = 64
ROPE_THETA = 10000.0
EPS = 1e-6
NEG_INF = -1e30
LOG2E = math.log2(math.e)

A_HEADS = 8
A_KV_HEADS = 2
B_HEADS = 8
B_BRANCHES = ((128, 1), (512, 4), (2048, 16))
NUM_BUCKETS = 32
REL_MAX_DISTANCE = 1024

C_HEADS = 16
C_Q_RANK = 256
C_KV_RANK = 128
C_NOPE_DIM = 64
C_ROPE_DIM = 32
C_V_DIM = 64
C_QK_DIM = C_NOPE_DIM + C_ROPE_DIM
D_FF = 4 * D_MODEL

A_Q_W = A_HEADS * HEAD_DIM
A_KV_W = A_KV_HEADS * HEAD_DIM
B_W = B_HEADS * HEAD_DIM

LANES = 128
VMEM_LIMIT_BYTES = 56 * 1024 * 1024

PROJ0_TOKEN_TILE = 512
PROJ1_TOKEN_TILE = 1024
MLP_TOKEN_TILE = 1024
ATTN_Q_TILE = 256
FF_CHUNK = 1024

COL_QA = 0
COL_KA = COL_QA + A_Q_W
COL_QB = COL_KA + A_KV_W
COL_KB = COL_QB + B_W
QK0_W = COL_KB + B_W
ROW_VA = 0
ROW_VB = A_KV_W
VT0_ROWS = ROW_VB + B_W


def _rms(x):
    return x * lax.rsqrt(jnp.mean(x * x, axis=-1, keepdims=True) + EPS)


def _dot(a, b):
    return jnp.dot(a, b, preferred_element_type=F32)


def _dot_nt(a, b):
    return lax.dot_general(a, b, (((1,), (1,)), ((), ())), preferred_element_type=F32)


def _dot_tn(a, b):
    return lax.dot_general(a, b, (((0,), (0,)), ((), ())), preferred_element_type=F32)


def _with_ones(vt):
    return jnp.concatenate([vt, jnp.ones_like(vt)], axis=0)


def _softmax_pv(s, vt_ones):
    dv = vt_ones.shape[0] // 2
    m = jnp.max(s, axis=-1, keepdims=True)
    p = jnp.exp2(s - m).astype(BF16)
    acc = _dot(vt_ones, p.T)
    return acc[:dv] / acc[dv:dv + 1]


def _pipelined(blocks, scores, finish):
    s_next = scores(blocks[0])
    for n, blk in enumerate(blocks):
        s_cur = s_next
        if n + 1 < len(blocks):
            s_next = scores(blocks[n + 1])
        finish(blk, s_cur)


def _proj0_kernel(x_ref, g_ref, w_ref, wvt_ref, gq_ref, gk_ref, c_ref, sa_ref, sb_ref,
                  qk_ref, vt_ref):
    tm = x_ref.shape[0]
    xn = (_rms(x_ref[...]) * g_ref[...]).astype(BF16)
    low = lax.broadcasted_iota(jnp.int32, (tm, LANES), 1) < HEAD_DIM
    cos, sin_a, sin_b = c_ref[...], sa_ref[...], sb_ref[...]

    def norm_rope(y, gain):
        sq = y * y
        s_all = jnp.sum(sq, axis=-1, keepdims=True)
        s_low = jnp.sum(jnp.where(low, sq, 0.0), axis=-1, keepdims=True)
        ms = jnp.where(low, s_low, s_all - s_low) * (1.0 / HEAD_DIM)
        y = y * lax.rsqrt(ms + EPS) * gain
        return (y * cos + pltpu.roll(y, LANES - HEAD_DIM // 2, 1) * sin_a
                + pltpu.roll(y, HEAD_DIM // 2, 1) * sin_b)

    scale = HEAD_DIM ** -0.5 * LOG2E
    y = _dot(xn, w_ref[:, COL_QA:COL_QB])
    for c0 in range(COL_QA, COL_KA, LANES):
        qk_ref[:, c0:c0 + LANES] = (norm_rope(y[:, c0:c0 + LANES], gq_ref[...]) * scale).astype(BF16)
    qk_ref[:, COL_KA:COL_QB] = norm_rope(y[:, COL_KA:COL_QB], gk_ref[...]).astype(BF16)
    qk_ref[:, COL_QB:COL_KB] = (_dot(xn, w_ref[:, COL_QB:COL_KB]) * scale).astype(BF16)
    qk_ref[:, COL_KB:] = _dot(xn, w_ref[:, COL_KB:]).astype(BF16)
    vt_ref[...] = _dot_nt(wvt_ref[...], xn).astype(BF16)


def _gqa_kernel(q_ref, k_ref, vt_ref, o_ref):
    rep = A_HEADS // A_KV_HEADS
    head = lambda i: slice(i * HEAD_DIM, (i + 1) * HEAD_DIM)
    keys = [k_ref[:, head(g)] for g in range(A_KV_HEADS)]
    vt_ones = [_with_ones(vt_ref[head(g), :]) for g in range(A_KV_HEADS)]

    def scores(h):
        return _dot_nt(q_ref[:, head(h)], keys[h // rep])

    def finish(h, s):
        o_ref[head(h), :] = _softmax_pv(s, vt_ones[h // rep]).astype(o_ref.dtype)

    _pipelined(list(range(A_HEADS)), scores, finish)


def _dilated_kernel(q_ref, k_ref, vt_ref, rb_ref, onehot_ref, logm_ref, o_ref, e_ref):
    s_len = q_ref.shape[0]
    blk = ATTN_Q_TILE
    n_blk = s_len // blk
    reach = -(-max((w // (2 * d)) * d for w, d in B_BRANCHES) // blk)
    heads = LANES // HEAD_DIM
    e_cols = (2 * n_blk - 1) * blk

    @pl.when(pl.program_id(1) == 0)
    def _():
        for hh in range(heads):
            per_delta = jnp.sum(onehot_ref[...] * rb_ref[:, hh:hh + 1], axis=0, keepdims=True)
            per_delta = (per_delta + logm_ref[...]) * LOG2E
            spread = jnp.broadcast_to(per_delta, (blk, 2 * s_len))
            toeplitz = pltpu.roll(spread, 0, 1, stride=1, stride_axis=0)
            e_ref[hh] = toeplitz[:, blk:blk + e_cols]

    head = lambda hh: slice(hh * HEAD_DIM, (hh + 1) * HEAD_DIM)
    vt_ones = [_with_ones(vt_ref[head(hh), :]) for hh in range(heads)]

    def key_range(i):
        return max(0, i - reach) * blk, min(n_blk, i + reach + 1) * blk

    def scores(block):
        hh, i = block
        k_lo, k_hi = key_range(i)
        off = (n_blk - 1 - i) * blk
        s = _dot_nt(q_ref[i * blk:(i + 1) * blk, head(hh)], k_ref[k_lo:k_hi, head(hh)])
        return s + e_ref[hh, :, off + k_lo:off + k_hi]

    def finish(block, s):
        hh, i = block
        k_lo, k_hi = key_range(i)
        o = _softmax_pv(s, vt_ones[hh][:, k_lo:k_hi])
        o_ref[head(hh), i * blk:(i + 1) * blk] = o.astype(o_ref.dtype)

    _pipelined([(hh, i) for hh in range(heads) for i in range(n_blk)], scores, finish)


def _mlp_kernel(n_mix, final_norm, *refs):
    h_ref = refs[0]
    mix_refs = refs[1:1 + n_mix]
    wo_ref, g_ref, w1_ref, w2_ref = refs[1 + n_mix:5 + n_mix]
    gf_ref = refs[5 + n_mix] if final_norm else None
    out_ref = refs[-1]

    mix_t = jnp.concatenate([m_ref[...] for m_ref in mix_refs], axis=0)
    h = h_ref[...] + _dot_tn(mix_t, wo_ref[...])
    hn = (_rms(h) * g_ref[...]).astype(BF16)
    for c0 in range(0, D_FF, FF_CHUNK):
        a = jnp.maximum(_dot(hn, w1_ref[:, c0:c0 + FF_CHUNK]), 0.0)
        h = h + _dot((a * a).astype(BF16), w2_ref[c0:c0 + FF_CHUNK, :])
    if final_norm:
        h = _rms(h) * gf_ref[...]
    out_ref[...] = h


def _proj1_kernel(h_ref, g_ref, wd_ref, gq_ref, gkv_ref, wuq_ref, wuk_ref, wuvt_ref,
                  cq_ref, sq_ref, ck_ref, sk_ref, q_ref, k_ref, vt_ref):
    xn = (_rms(h_ref[...]) * g_ref[...]).astype(BF16)
    hd = _dot(xn, wd_ref[...])
    c_q = (_rms(hd[:, :C_Q_RANK]) * gq_ref[...]).astype(BF16)
    c_kv = (_rms(hd[:, C_Q_RANK:C_Q_RANK + C_KV_RANK]) * gkv_ref[...]).astype(BF16)

    def rope(y, cos, sin):
        return y * cos + pltpu.roll(y, LANES - C_ROPE_DIM, 1) * sin

    cq, sq = cq_ref[...], sq_ref[...]
    k_rope = rope(hd[:, C_Q_RANK + C_KV_RANK:], ck_ref[...], sk_ref[...])
    for c0 in range(0, C_HEADS * LANES, 2 * LANES):
        q2 = _dot(c_q, wuq_ref[:, c0:c0 + 2 * LANES])
        k2 = _dot(c_kv, wuk_ref[:, c0:c0 + 2 * LANES])
        for lo in (0, LANES):
            lanes = slice(c0 + lo, c0 + lo + LANES)
            q_ref[:, lanes] = rope(q2[:, lo:lo + LANES], cq, sq).astype(BF16)
            k_ref[:, lanes] = (k2[:, lo:lo + LANES] + k_rope).astype(BF16)
    vt_ref[...] = _dot_nt(wuvt_ref[...], c_kv).astype(BF16)


def _mla_kernel(q_ref, k_ref, vt_ref, o_ref):
    s_len = q_ref.shape[0]
    tq = ATTN_Q_TILE
    v_rows = lambda hh: slice(hh * C_V_DIM, (hh + 1) * C_V_DIM)
    vt_ones = [_with_ones(vt_ref[v_rows(hh), :]) for hh in range(2)]

    def scores(block):
        hh, i = block
        lanes = slice(hh * LANES, (hh + 1) * LANES)
        return _dot_nt(q_ref[i * tq:(i + 1) * tq, lanes], k_ref[:, lanes])

    def finish(block, s):
        hh, i = block
        o_ref[v_rows(hh), i * tq:(i + 1) * tq] = _softmax_pv(s, vt_ones[hh]).astype(o_ref.dtype)

    _pipelined([(hh, i) for hh in range(2) for i in range(s_len // tq)], scores, finish)


def _rope_angles(pos, dim):
    inv_freq = ROPE_THETA ** (-np.arange(0, dim, 2, dtype=np.float64) / dim)
    return pos.astype(np.float64)[:, None] * inv_freq[None, :]


def _axial_tables(s_len):
    t = np.arange(s_len)
    ang = np.concatenate([_rope_angles(t // GRID_W, HEAD_DIM // 2),
                          _rope_angles(t % GRID_W, HEAD_DIM // 2)], axis=-1)
    cos, sin, zero = np.cos(ang), np.sin(ang), np.zeros_like(ang)
    c = np.concatenate([cos, cos, cos, cos], axis=-1)
    sa = np.concatenate([-sin, zero, -sin, zero], axis=-1)
    sb = np.concatenate([zero, sin, zero, sin], axis=-1)
    return [jnp.asarray(a, F32) for a in (c, sa, sb)]


def _mla_tables(s_len, scale):
    ang = _rope_angles(np.arange(s_len), C_ROPE_DIM)
    cos, sin = np.cos(ang), np.sin(ang)
    ones = np.ones((s_len, C_NOPE_DIM))
    z_nope = np.zeros((s_len, C_NOPE_DIM))
    z_tail = np.zeros((s_len, LANES - C_QK_DIM))
    c = np.concatenate([ones, cos, cos, z_tail], axis=-1)
    sn = np.concatenate([z_nope, sin, sin, z_tail], axis=-1)
    return [jnp.asarray(a * scale, F32) for a in (c, sn)]


def _rotate_half_cols(w):
    half = w.shape[-1] // 2
    return jnp.concatenate([-w[..., half:], w[..., :half]], axis=-1)


def _t5_bucket(rel):
    nb = NUM_BUCKETS // 2
    max_exact = nb // 2
    base = np.where(rel > 0, nb, 0)
    n = np.abs(rel)
    nf = np.maximum(n, 1).astype(np.float32)
    large = max_exact + (np.log(nf / np.float32(max_exact)) / np.float32(math.log(REL_MAX_DISTANCE / max_exact))
                         * np.float32(nb - max_exact)).astype(np.int32)
    large = np.minimum(large, nb - 1)
    return base + np.where(n < max_exact, n, large)


def _dilated_structure(s_len):
    delta = np.arange(2 * s_len) - s_len
    mult = np.zeros(delta.shape, np.int32)
    for w, d in B_BRANCHES:
        n_side = w // (2 * d)
        mult += ((delta % d == 0) & (np.abs(delta) <= n_side * d)).astype(np.int32)
    log_mult = np.where(mult > 0, np.log(np.maximum(mult, 1).astype(np.float64)), NEG_INF)
    onehot = (_t5_bucket(delta)[None, :] == np.arange(NUM_BUCKETS)[:, None]) & (mult > 0)[None, :]
    return jnp.asarray(onehot, F32), jnp.asarray(log_mult[None, :], F32)


def _params(*sem):
    return pltpu.CompilerParams(dimension_semantics=sem, vmem_limit_bytes=VMEM_LIMIT_BYTES)


def _resident(shape):
    return pl.BlockSpec(shape, lambda *_: (0,) * len(shape), pipeline_mode=pl.Buffered(1))


def _row2(v):
    return v.reshape(1, -1).astype(F32)


def _token_specs(tm, s_len):
    seq_tiles = s_len // tm
    tok = lambda w: pl.BlockSpec((tm, w), lambda i: (i, 0))
    tok_t = lambda w: pl.BlockSpec((None, w, tm), lambda i: (i // seq_tiles, 0, i % seq_tiles))
    table = pl.BlockSpec((tm, LANES), lambda i: (i % seq_tiles, 0))
    return tok, tok_t, table


def _mlp_call(h, mixes_t, wo, g, w1, w2, gf, name):
    n_tok = h.shape[0]
    tm = MLP_TOKEN_TILE
    tok, tok_t, _ = _token_specs(tm, mixes_t[0].shape[2])
    in_specs = [tok(D_MODEL)] + [tok_t(m.shape[1]) for m in mixes_t] + [
        _resident(wo.shape), _resident((1, D_MODEL)), _resident(w1.shape), _resident(w2.shape)]
    args = [h, *mixes_t, wo, _row2(g), w1, w2]
    if gf is not None:
        in_specs.append(_resident((1, D_MODEL)))
        args.append(_row2(gf))
    return pl.pallas_call(
        functools.partial(_mlp_kernel, len(mixes_t), gf is not None),
        out_shape=jax.ShapeDtypeStruct((n_tok, D_MODEL), F32),
        grid=(n_tok // tm,),
        in_specs=in_specs,
        out_specs=tok(D_MODEL),
        compiler_params=_params("parallel"),
        name=name,
    )(*args)


def kernel(x, norm_mix_g, norm_mlp_g, ab_w_in, a_q_norm_g, a_k_norm_g, ab_w_out, rel_bias,
           c_w_down, c_q_norm_g, c_kv_norm_g, c_w_uq, c_w_ukv, c_w_out, mlp_w1, mlp_w2,
           final_norm_g):
    b, s, d = x.shape
    n_tok = b * s
    h0 = x.reshape(n_tok, d)
    tm = PROJ0_TOKEN_TILE
    tok, tok_t, tab_spec = _token_specs(tm, s)

    tables0 = _axial_tables(s)
    gq = _row2(jnp.tile(a_q_norm_g[0], LANES // HEAD_DIM))
    gk = _row2(jnp.tile(a_k_norm_g[0], LANES // HEAD_DIM))
    w_in = ab_w_in[0]
    q_a, k_a, v_a, q_b, k_b, v_b = jnp.split(
        w_in, np.cumsum([A_Q_W, A_KV_W, A_KV_W, B_W, B_W]).tolist(), axis=1)
    w_qk = jnp.concatenate([q_a, k_a, q_b, k_b], axis=1).astype(BF16)
    w_vt = jnp.concatenate([v_a, v_b], axis=1).T.astype(BF16)
    qk0, vt0 = pl.pallas_call(
        _proj0_kernel,
        out_shape=(jax.ShapeDtypeStruct((n_tok, QK0_W), BF16),
                   jax.ShapeDtypeStruct((b, VT0_ROWS, s), BF16)),
        grid=(n_tok // tm,),
        in_specs=[tok(d), _resident((1, d)), _resident(w_qk.shape), _resident(w_vt.shape),
                  _resident((1, LANES)), _resident((1, LANES)), tab_spec, tab_spec, tab_spec],
        out_specs=(tok(QK0_W), tok_t(VT0_ROWS)),
        compiler_params=_params("parallel"),
        name="proj0",
    )(h0, _row2(norm_mix_g[0]), w_qk, w_vt, gq, gk, *tables0)

    tq = ATTN_Q_TILE
    q_tiles = s // tq
    o_a = pl.pallas_call(
        _gqa_kernel,
        out_shape=jax.ShapeDtypeStruct((b, A_Q_W, s), BF16),
        grid=(b, q_tiles),
        in_specs=[pl.BlockSpec((tq, A_Q_W), lambda bi, qi: (bi * q_tiles + qi, COL_QA // A_Q_W)),
                  pl.BlockSpec((s, A_KV_W), lambda bi, qi: (bi, COL_KA // A_KV_W)),
                  pl.BlockSpec((None, A_KV_W, s), lambda bi, qi: (bi, ROW_VA // A_KV_W, 0))],
        out_specs=pl.BlockSpec((None, A_Q_W, tq), lambda bi, qi: (bi, 0, qi)),
        compiler_params=_params("parallel", "parallel"),
        name="gqa",
    )(qk0, qk0, vt0)

    onehot, log_mult = _dilated_structure(s)
    pairs = B_W // LANES
    heads_per_pair = LANES // HEAD_DIM
    rb = rel_bias.astype(F32).reshape(NUM_BUCKETS, pairs, heads_per_pair).transpose(1, 0, 2)
    o_b = pl.pallas_call(
        _dilated_kernel,
        out_shape=jax.ShapeDtypeStruct((b, B_W, s), BF16),
        grid=(pairs, b),
        in_specs=[pl.BlockSpec((s, LANES), lambda p, bi: (bi, COL_QB // LANES + p)),
                  pl.BlockSpec((s, LANES), lambda p, bi: (bi, COL_KB // LANES + p)),
                  pl.BlockSpec((None, LANES, s), lambda p, bi: (bi, ROW_VB // LANES + p, 0)),
                  pl.BlockSpec((None, NUM_BUCKETS, heads_per_pair), lambda p, bi: (p, 0, 0)),
                  _resident(onehot.shape), _resident(log_mult.shape)],
        out_specs=pl.BlockSpec((None, LANES, s), lambda p, bi: (bi, p, 0)),
        scratch_shapes=[pltpu.VMEM((heads_per_pair, tq, (2 * q_tiles - 1) * tq), F32)],
        compiler_params=_params("parallel", "arbitrary"),
        name="dilated",
    )(qk0, qk0, vt0, rb, onehot, log_mult)

    h1 = _mlp_call(h0, [o_a, o_b], ab_w_out[0].astype(BF16), norm_mlp_g[0],
                   mlp_w1[0].astype(BF16), mlp_w2[0].astype(BF16), None, "mix0_mlp0")

    scale_c = C_QK_DIM ** -0.5 * LOG2E
    tables_q = _mla_tables(s, scale_c)
    tables_k = _mla_tables(s, 1.0)
    zeros = lambda r, c: jnp.zeros((r, c), F32)
    wd = c_w_down[0]
    kv_end = C_Q_RANK + C_KV_RANK
    wd_p = jnp.concatenate([wd[:, :kv_end], zeros(d, C_NOPE_DIM), wd[:, kv_end:],
                            _rotate_half_cols(wd[:, kv_end:])], axis=1).astype(BF16)
    wuq = c_w_uq[0].reshape(C_Q_RANK, C_HEADS, C_QK_DIM)
    wuq_p = jnp.concatenate([wuq, _rotate_half_cols(wuq[:, :, C_NOPE_DIM:])], axis=-1).reshape(
        C_Q_RANK, C_HEADS * LANES).astype(BF16)
    wukv = c_w_ukv[0].reshape(C_KV_RANK, C_HEADS, C_NOPE_DIM + C_V_DIM)
    wuk_p = jnp.pad(wukv[:, :, :C_NOPE_DIM], ((0, 0), (0, 0), (0, LANES - C_NOPE_DIM))).reshape(
        C_KV_RANK, C_HEADS * LANES).astype(BF16)
    wuv_t = wukv[:, :, C_NOPE_DIM:].reshape(C_KV_RANK, C_HEADS * C_V_DIM).T.astype(BF16)

    qk_w = C_HEADS * LANES
    v_w = C_HEADS * C_V_DIM
    tm = PROJ1_TOKEN_TILE
    tok, tok_t, tab_spec = _token_specs(tm, s)
    q_c, k_c, vt_c = pl.pallas_call(
        _proj1_kernel,
        out_shape=(jax.ShapeDtypeStruct((n_tok, qk_w), BF16),
                   jax.ShapeDtypeStruct((n_tok, qk_w), BF16),
                   jax.ShapeDtypeStruct((b, v_w, s), BF16)),
        grid=(n_tok // tm,),
        in_specs=[tok(d), _resident((1, d)), _resident(wd_p.shape),
                  _resident((1, C_Q_RANK)), _resident((1, C_KV_RANK)),
                  _resident(wuq_p.shape), _resident(wuk_p.shape), _resident(wuv_t.shape)]
                 + [tab_spec] * 4,
        out_specs=(tok(qk_w), tok(qk_w), tok_t(v_w)),
        compiler_params=_params("parallel"),
        name="proj1",
    )(h1, _row2(norm_mix_g[1]), wd_p, _row2(c_q_norm_g[0]), _row2(c_kv_norm_g[0]),
      wuq_p, wuk_p, wuv_t, *tables_q, *tables_k)

    o_c = pl.pallas_call(
        _mla_kernel,
        out_shape=jax.ShapeDtypeStruct((b, v_w, s), BF16),
        grid=(b, C_HEADS // 2),
        in_specs=[pl.BlockSpec((s, 2 * LANES), lambda bi, p: (bi, p)),
                  pl.BlockSpec((s, 2 * LANES), lambda bi, p: (bi, p)),
                  pl.BlockSpec((None, 2 * C_V_DIM, s), lambda bi, p: (bi, p, 0))],
        out_specs=pl.BlockSpec((None, 2 * C_V_DIM, s), lambda bi, p: (bi, p, 0)),
        compiler_params=_params("parallel", "parallel"),
        name="mla",
    )(q_c, k_c, vt_c)

    out = _mlp_call(h1, [o_c], c_w_out[0].astype(BF16), norm_mlp_g[1],
                    mlp_w1[1].astype(BF16), mlp_w2[1].astype(BF16), final_norm_g, "mix1_mlp1")
    return out.reshape(b, s, d)
```

```python
import functools
import math

import jax
import jax.numpy as jnp
import numpy as np
from jax import lax
from jax.experimental import pallas as pl
from jax.experimental.pallas import tpu as pltpu

F32 = jnp.float32
BF16 = jnp.bfloat16

D_MODEL = 1024
GRID_W = 64
HEAD_DIM = 64
ROPE_THETA = 10000.0
EPS = 1e-6
NEG_INF = -1e30
LOG2E = math.log2(math.e)

A_HEADS = 8
A_KV_HEADS = 2
B_HEADS = 8
B_NEAR_BRANCHES = ((128, 1), (512, 4))
B_FAR_BRANCH = (2048, 16)
NUM_BUCKETS = 32
REL_MAX_DISTANCE = 1024

C_HEADS = 16
C_Q_RANK = 256
C_KV_RANK = 128
C_NOPE_DIM = 64
C_ROPE_DIM = 32
C_V_DIM = 64
C_QK_DIM = C_NOPE_DIM + C_ROPE_DIM
D_FF = 4 * D_MODEL

A_Q_W = A_HEADS * HEAD_DIM
A_KV_W = A_KV_HEADS * HEAD_DIM
B_W = B_HEADS * HEAD_DIM

LANES = 128
VMEM_LIMIT_BYTES = 56 * 1024 * 1024

PROJ0_TOKEN_TILE = 512
PROJ1_TOKEN_TILE = 1024
MLP_TOKEN_TILE = 1024
ATTN_Q_TILE = 256
FF_CHUNK = 1024

COL_QA = 0
COL_KA = COL_QA + A_Q_W
COL_VA = COL_KA + A_KV_W
COL_QB = COL_VA + A_KV_W
COL_KB = COL_QB + B_W
COL_VB = COL_KB + B_W


def _rms(x):
    return x * lax.rsqrt(jnp.mean(x * x, axis=-1, keepdims=True) + EPS)


def _dot(a, b):
    return jnp.dot(a, b, preferred_element_type=F32)


def _dot_nt(a, b):
    return lax.dot_general(a, b, (((1,), (1,)), ((), ())), preferred_element_type=F32)


def _dot_tn(a, b):
    return lax.dot_general(a, b, (((0,), (0,)), ((), ())), preferred_element_type=F32)


def _with_ones(vt):
    return jnp.concatenate([vt, jnp.ones_like(vt)], axis=0)


def _softmax_pv(s, vt_ones):
    dv = vt_ones.shape[0] // 2
    m = jnp.max(s, axis=-1, keepdims=True)
    p = jnp.exp2(s - m).astype(BF16)
    acc = _dot(vt_ones, p.T)
    return acc[:dv] / acc[dv:dv + 1]


def _pipelined(blocks, scores, finish):
    s_next = scores(blocks[0])
    for n, blk in enumerate(blocks):
        s_cur = s_next
        if n + 1 < len(blocks):
            s_next = scores(blocks[n + 1])
        finish(blk, s_cur)


def _toeplitz(per_delta, rows):
    spread = jnp.broadcast_to(per_delta, (rows, per_delta.shape[1]))
    return pltpu.roll(spread, 0, 1, stride=1, stride_axis=0)


def _proj0_kernel(x_ref, g_ref, w_ref, gq_ref, gk_ref, c_ref, sa_ref, sb_ref,
                  qka_ref, vat_ref, qb_ref, kb_ref, vb_ref):
    tm = x_ref.shape[0]
    xn = (_rms(x_ref[...]) * g_ref[...]).astype(BF16)
    low = lax.broadcasted_iota(jnp.int32, (tm, LANES), 1) < HEAD_DIM
    cos, sin_a, sin_b = c_ref[...], sa_ref[...], sb_ref[...]

    def norm_rope(y, gain):
        sq = y * y
        s_all = jnp.sum(sq, axis=-1, keepdims=True)
        s_low = jnp.sum(jnp.where(low, sq, 0.0), axis=-1, keepdims=True)
        ms = jnp.where(low, s_low, s_all - s_low) * (1.0 / HEAD_DIM)
        y = y * lax.rsqrt(ms + EPS) * gain
        return (y * cos + pltpu.roll(y, LANES - HEAD_DIM // 2, 1) * sin_a
                + pltpu.roll(y, HEAD_DIM // 2, 1) * sin_b)

    scale = HEAD_DIM ** -0.5 * LOG2E
    y = _dot(xn, w_ref[:, COL_QA:COL_QB])
    for c0 in range(COL_QA, COL_KA, LANES):
        qka_ref[:, c0:c0 + LANES] = (norm_rope(y[:, c0:c0 + LANES], gq_ref[...]) * scale).astype(BF16)
    qka_ref[:, COL_KA:COL_VA] = norm_rope(y[:, COL_KA:COL_VA], gk_ref[...]).astype(BF16)
    vat_ref[...] = y[:, COL_VA:COL_QB].T.astype(BF16)
    qb_ref[...] = (_dot(xn, w_ref[:, COL_QB:COL_KB]) * scale).astype(BF16)
    kb_ref[...] = _dot(xn, w_ref[:, COL_KB:COL_VB]).astype(BF16)
    vb_ref[...] = _dot(xn, w_ref[:, COL_VB:]).astype(BF16)


def _gqa_kernel(q_ref, k_ref, vt_ref, o_ref):
    rep = A_HEADS // A_KV_HEADS
    head = lambda i: slice(i * HEAD_DIM, (i + 1) * HEAD_DIM)
    keys = [k_ref[:, head(g)] for g in range(A_KV_HEADS)]
    vt_ones = [_with_ones(vt_ref[head(g), :]) for g in range(A_KV_HEADS)]

    def scores(h):
        return _dot_nt(q_ref[:, head(h)], keys[h // rep])

    def finish(h, s):
        o_ref[head(h), :] = _softmax_pv(s, vt_ones[h // rep]).astype(o_ref.dtype)

    _pipelined(list(range(A_HEADS)), scores, finish)


def _per_delta(onehot_ref, rb_col, logm_ref):
    bias = jnp.sum(onehot_ref[...] * rb_col, axis=0, keepdims=True)
    return (bias + logm_ref[...]) * LOG2E


def _dilated_near_kernel(q_ref, k_ref, v_ref, rb_ref, onehot_ref, logm_ref, o_ref, lse_ref, e_ref):
    s_len = q_ref.shape[0]
    blk = ATTN_Q_TILE
    n_blk = s_len // blk
    n_delta = onehot_ref.shape[1]

    @pl.when(pl.program_id(1) == 0)
    def _():
        for hh in range(2):
            toeplitz = _toeplitz(_per_delta(onehot_ref, rb_ref[:, hh:hh + 1], logm_ref), blk)
            start = n_delta // 2 - blk
            e_ref[hh] = toeplitz[:, start:start + 3 * blk]

    low = lax.broadcasted_iota(jnp.int32, (blk, LANES), 1) < HEAD_DIM
    vt = v_ref[...].T
    ones = jnp.ones((HEAD_DIM, s_len), BF16)
    vt_ones = [jnp.concatenate([vt[:HEAD_DIM], ones], axis=0),
               jnp.concatenate([ones, vt[HEAD_DIM:]], axis=0)]

    def key_range(i):
        return max(0, i - 1) * blk, min(n_blk, i + 2) * blk

    def scores(i):
        k_lo, k_hi = key_range(i)
        off = k_lo - (i - 1) * blk
        q = q_ref[i * blk:(i + 1) * blk, :]
        k = k_ref[k_lo:k_hi, :]
        return [_dot_nt(jnp.where(mine, q, jnp.zeros_like(q)), k)
                + e_ref[hh, :, off:off + (k_hi - k_lo)] for hh, mine in enumerate((low, ~low))]

    def finish(i, s_pair):
        k_lo, k_hi = key_range(i)
        rows = slice(i * blk, (i + 1) * blk)
        outs, lses = [], []
        for hh, s in enumerate(s_pair):
            m = jnp.max(s, axis=-1, keepdims=True)
            p = jnp.exp2(s - m).astype(BF16)
            acc = _dot(vt_ones[hh][:, k_lo:k_hi], p.T).T
            den = pltpu.roll(acc, HEAD_DIM, 1)
            outs.append(acc / den)
            lses.append(m + jnp.log2(den))
        o_ref[rows, :] = jnp.where(low, outs[0], outs[1]).astype(o_ref.dtype)
        lse_ref[rows, :] = jnp.where(low, lses[0], lses[1])

    _pipelined(list(range(n_blk)), scores, finish)


def _dilated_far_kernel(q_ref, k_ref, v_ref, on_ref, lse_ref, rb_ref, onehot_ref, logm_ref,
                        o_ref, e_ref):
    cl = q_ref.shape[0]
    n_cls = q_ref.shape[1] // B_W

    @pl.when(pl.program_id(0) == 0)
    def _():
        for h in range(B_HEADS):
            band = _toeplitz(_per_delta(onehot_ref, rb_ref[:, h:h + 1], logm_ref), cl)[:, cl:]
            neg = jnp.full_like(band, NEG_INF * LOG2E)
            e_ref[h] = jnp.concatenate([jnp.concatenate([band, neg], axis=1),
                                        jnp.concatenate([neg, band], axis=1)], axis=0)

    low = lax.broadcasted_iota(jnp.int32, (2 * cl, LANES), 1) < HEAD_DIM

    def two_classes(ref, unit):
        pair, r0 = unit
        lanes = lambda r: slice(B_W * r + LANES * pair, B_W * r + LANES * (pair + 1))
        return jnp.concatenate([ref[:, lanes(r0)], ref[:, lanes(r0 + 1)]], axis=0)

    def scores(unit):
        q, k = two_classes(q_ref, unit), two_classes(k_ref, unit)
        return [_dot_nt(jnp.where(mine, q, jnp.zeros_like(q)), k) + e_ref[2 * unit[0] + hh]
                for hh, mine in enumerate((low, ~low))]

    def finish(unit, s_pair):
        pair, r0 = unit
        v = two_classes(v_ref, unit)
        v_ones = jnp.concatenate([v, jnp.ones_like(v)], axis=-1)
        accs, maxes = [], []
        for s in s_pair:
            m_far = jnp.max(s, axis=-1, keepdims=True)
            accs.append(_dot(jnp.exp2(s - m_far).astype(BF16), v_ones))
            maxes.append(jnp.broadcast_to(m_far, (2 * cl, LANES)))
        num_far = jnp.where(low, accs[0][:, :LANES], accs[1][:, :LANES])
        den_far = jnp.where(low, accs[0][:, LANES:], accs[1][:, LANES:])
        m_far = jnp.where(low, maxes[0], maxes[1])
        o_near = two_classes(on_ref, unit).astype(F32)
        lse_near = two_classes(lse_ref, unit)
        m = jnp.maximum(lse_near, m_far)
        w_near, w_far = jnp.exp2(lse_near - m), jnp.exp2(m_far - m)
        out = ((o_near * w_near + num_far * w_far) / (w_near + den_far * w_far)).astype(o_ref.dtype)
        o_ref[:, B_W * r0 + LANES * pair:B_W * r0 + LANES * (pair + 1)] = out[:cl]
        o_ref[:, B_W * (r0 + 1) + LANES * pair:B_W * (r0 + 1) + LANES * (pair + 1)] = out[cl:]

    _pipelined([(pair, r0) for pair in range(B_W // LANES) for r0 in range(0, n_cls, 2)],
               scores, finish)


def _mlp_kernel(n_t, n_tok, final_norm, *refs):
    h_ref = refs[0]
    n_mix = n_t + n_tok
    mix_t_refs = refs[1:1 + n_t]
    mix_tok_refs = refs[1 + n_t:1 + n_mix]
    wo_ref, g_ref, w1_ref, w2_ref = refs[1 + n_mix:5 + n_mix]
    gf_ref = refs[5 + n_mix] if final_norm else None
    out_ref = refs[-1]

    parts = [m_ref[...] for m_ref in mix_t_refs] + [m_ref[...].T for m_ref in mix_tok_refs]
    h = h_ref[...] + _dot_tn(jnp.concatenate(parts, axis=0), wo_ref[...])
    hn = (_rms(h) * g_ref[...]).astype(BF16)
    for c0 in range(0, D_FF, FF_CHUNK):
        a = jnp.maximum(_dot(hn, w1_ref[:, c0:c0 + FF_CHUNK]), 0.0)
        h = h + _dot((a * a).astype(BF16), w2_ref[c0:c0 + FF_CHUNK, :])
    if final_norm:
        h = _rms(h) * gf_ref[...]
    out_ref[...] = h


def _proj1_kernel(h_ref, g_ref, wd_ref, gq_ref, gkv_ref, wuq_ref, wuk_ref, wuvt_ref,
                  cq_ref, sq_ref, ck_ref, sk_ref, q_ref, k_ref, vt_ref):
    xn = (_rms(h_ref[...]) * g_ref[...]).astype(BF16)
    hd = _dot(xn, wd_ref[...])
    c_q = (_rms(hd[:, :C_Q_RANK]) * gq_ref[...]).astype(BF16)
    c_kv = (_rms(hd[:, C_Q_RANK:C_Q_RANK + C_KV_RANK]) * gkv_ref[...]).astype(BF16)

    def rope(y, cos, sin):
        return y * cos + pltpu.roll(y, LANES - C_ROPE_DIM, 1) * sin

    cq, sq = cq_ref[...], sq_ref[...]
    k_rope = rope(hd[:, C_Q_RANK + C_KV_RANK:], ck_ref[...], sk_ref[...])
    for c0 in range(0, C_HEADS * LANES, 2 * LANES):
        q2 = _dot(c_q, wuq_ref[:, c0:c0 + 2 * LANES])
        k2 = _dot(c_kv, wuk_ref[:, c0:c0 + 2 * LANES])
        for lo in (0, LANES):
            lanes = slice(c0 + lo, c0 + lo + LANES)
            q_ref[:, lanes] = rope(q2[:, lo:lo + LANES], cq, sq).astype(BF16)
            k_ref[:, lanes] = (k2[:, lo:lo + LANES] + k_rope).astype(BF16)
    vt_ref[...] = _dot_nt(wuvt_ref[...], c_kv).astype(BF16)


def _mla_kernel(q_ref, k_ref, vt_ref, o_ref):
    s_len = q_ref.shape[0]
    tq = ATTN_Q_TILE
    v_rows = lambda hh: slice(hh * C_V_DIM, (hh + 1) * C_V_DIM)
    vt_ones = [_with_ones(vt_ref[v_rows(hh), :]) for hh in range(2)]

    def scores(block):
        hh, i = block
        lanes = slice(hh * LANES, (hh + 1) * LANES)
        return _dot_nt(q_ref[i * tq:(i + 1) * tq, lanes], k_ref[:, lanes])

    def finish(block, s):
        hh, i = block
        o_ref[v_rows(hh), i * tq:(i + 1) * tq] = _softmax_pv(s, vt_ones[hh]).astype(o_ref.dtype)

    _pipelined([(hh, i) for hh in range(2) for i in range(s_len // tq)], scores, finish)


def _rope_angles(pos, dim):
    inv_freq = ROPE_THETA ** (-np.arange(0, dim, 2, dtype=np.float64) / dim)
    return pos.astype(np.float64)[:, None] * inv_freq[None, :]


def _axial_tables(s_len):
    t = np.arange(s_len)
    ang = np.concatenate([_rope_angles(t // GRID_W, HEAD_DIM // 2),
                          _rope_angles(t % GRID_W, HEAD_DIM // 2)], axis=-1)
    cos, sin, zero = np.cos(ang), np.sin(ang), np.zeros_like(ang)
    c = np.concatenate([cos, cos, cos, cos], axis=-1)
    sa = np.concatenate([-sin, zero, -sin, zero], axis=-1)
    sb = np.concatenate([zero, sin, zero, sin], axis=-1)
    return [jnp.asarray(a, F32) for a in (c, sa, sb)]


def _mla_tables(s_len, scale):
    ang = _rope_angles(np.arange(s_len), C_ROPE_DIM)
    cos, sin = np.cos(ang), np.sin(ang)
    ones = np.ones((s_len, C_NOPE_DIM))
    z_nope = np.zeros((s_len, C_NOPE_DIM))
    z_tail = np.zeros((s_len, LANES - C_QK_DIM))
    c = np.concatenate([ones, cos, cos, z_tail], axis=-1)
    sn = np.concatenate([z_nope, sin, sin, z_tail], axis=-1)
    return [jnp.asarray(a * scale, F32) for a in (c, sn)]


def _rotate_half_cols(w):
    half = w.shape[-1] // 2
    return jnp.concatenate([-w[..., half:], w[..., :half]], axis=-1)


def _t5_bucket(rel):
    nb = NUM_BUCKETS // 2
    max_exact = nb // 2
    base = np.where(rel > 0, nb, 0)
    n = np.abs(rel)
    nf = np.maximum(n, 1).astype(np.float32)
    large = max_exact + (np.log(nf / np.float32(max_exact)) / np.float32(math.log(REL_MAX_DISTANCE / max_exact))
                         * np.float32(nb - max_exact)).astype(np.int32)
    large = np.minimum(large, nb - 1)
    return base + np.where(n < max_exact, n, large)


def _bias_structure(delta, branches):
    mult = np.zeros(delta.shape, np.int32)
    for w, d in branches:
        n_side = w // (2 * d)
        mult += ((delta % d == 0) & (np.abs(delta) <= n_side * d)).astype(np.int32)
    log_mult = np.where(mult > 0, np.log(np.maximum(mult, 1).astype(np.float64)), NEG_INF)
    onehot = (_t5_bucket(delta)[None, :] == np.arange(NUM_BUCKETS)[:, None]) & (mult > 0)[None, :]
    return jnp.asarray(onehot, F32), jnp.asarray(log_mult[None, :], F32)


def _params(*sem):
    return pltpu.CompilerParams(dimension_semantics=sem, vmem_limit_bytes=VMEM_LIMIT_BYTES)


def _resident(shape):
    return pl.BlockSpec(shape, lambda *_: (0,) * len(shape), pipeline_mode=pl.Buffered(1))


def _row2(v):
    return v.reshape(1, -1).astype(F32)


def _token_specs(tm, s_len):
    seq_tiles = s_len // tm
    tok = lambda w: pl.BlockSpec((tm, w), lambda i: (i, 0))
    tok_t = lambda w: pl.BlockSpec((None, w, tm), lambda i: (i // seq_tiles, 0, i % seq_tiles))
    table = pl.BlockSpec((tm, LANES), lambda i: (i % seq_tiles, 0))
    return tok, tok_t, table


def _mlp_call(h, mixes_t, mixes_tok, s_len, wo, g, w1, w2, gf, name):
    n_tok = h.shape[0]
    tm = MLP_TOKEN_TILE
    tok, tok_t, _ = _token_specs(tm, s_len)
    in_specs = ([tok(D_MODEL)] + [tok_t(m.shape[1]) for m in mixes_t]
                + [tok(m.shape[1]) for m in mixes_tok]
                + [_resident(wo.shape), _resident((1, D_MODEL)), _resident(w1.shape),
                   _resident(w2.shape)])
    args = [h, *mixes_t, *mixes_tok, wo, _row2(g), w1, w2]
    if gf is not None:
        in_specs.append(_resident((1, D_MODEL)))
        args.append(_row2(gf))
    return pl.pallas_call(
        functools.partial(_mlp_kernel, len(mixes_t), len(mixes_tok), gf is not None),
        out_shape=jax.ShapeDtypeStruct((n_tok, D_MODEL), F32),
        grid=(n_tok // tm,),
        in_specs=in_specs,
        out_specs=tok(D_MODEL),
        compiler_params=_params("parallel"),
        name=name,
    )(*args)


def kernel(x, norm_mix_g, norm_mlp_g, ab_w_in, a_q_norm_g, a_k_norm_g, ab_w_out, rel_bias,
           c_w_down, c_q_norm_g, c_kv_norm_g, c_w_uq, c_w_ukv, c_w_out, mlp_w1, mlp_w2,
           final_norm_g):
    b, s, d = x.shape
    n_tok = b * s
    h0 = x.reshape(n_tok, d)
    tm = PROJ0_TOKEN_TILE
    tok, tok_t, tab_spec = _token_specs(tm, s)

    tables0 = _axial_tables(s)
    gq = _row2(jnp.tile(a_q_norm_g[0], LANES // HEAD_DIM))
    gk = _row2(jnp.tile(a_k_norm_g[0], LANES // HEAD_DIM))
    w_in = ab_w_in[0].astype(BF16)
    qka0, vat0, qb0, kb0, vb0 = pl.pallas_call(
        _proj0_kernel,
        out_shape=(jax.ShapeDtypeStruct((n_tok, COL_VA), BF16),
                   jax.ShapeDtypeStruct((b, A_KV_W, s), BF16),
                   jax.ShapeDtypeStruct((n_tok, B_W), BF16),
                   jax.ShapeDtypeStruct((n_tok, B_W), BF16),
                   jax.ShapeDtypeStruct((n_tok, B_W), BF16)),
        grid=(n_tok // tm,),
        in_specs=[tok(d), _resident((1, d)), _resident(w_in.shape),
                  _resident((1, LANES)), _resident((1, LANES)), tab_spec, tab_spec, tab_spec],
        out_specs=(tok(COL_VA), tok_t(A_KV_W), tok(B_W), tok(B_W), tok(B_W)),
        compiler_params=_params("parallel"),
        name="proj0",
    )(h0, _row2(norm_mix_g[0]), w_in, gq, gk, *tables0)

    tq = ATTN_Q_TILE
    q_tiles = s // tq
    o_a = pl.pallas_call(
        _gqa_kernel,
        out_shape=jax.ShapeDtypeStruct((b, A_Q_W, s), BF16),
        grid=(b, q_tiles),
        in_specs=[pl.BlockSpec((tq, A_Q_W), lambda bi, qi: (bi * q_tiles + qi, COL_QA // A_Q_W)),
                  pl.BlockSpec((s, A_KV_W), lambda bi, qi: (bi, COL_KA // A_KV_W)),
                  pl.BlockSpec((None, A_KV_W, s), lambda bi, qi: (bi, 0, 0))],
        out_specs=pl.BlockSpec((None, A_Q_W, tq), lambda bi, qi: (bi, 0, qi)),
        compiler_params=_params("parallel", "parallel"),
        name="gqa",
    )(qka0, qka0, vat0)

    pairs = B_W // LANES
    heads_per_pair = LANES // HEAD_DIM
    near_reach = max((w // (2 * dd)) * dd for w, dd in B_NEAR_BRANCHES)
    assert near_reach <= tq
    onehot_n, logm_n = _bias_structure(np.arange(4 * tq) - 2 * tq, B_NEAR_BRANCHES)
    rb = rel_bias.astype(F32)
    rb_pairs = rb.reshape(NUM_BUCKETS, pairs, heads_per_pair).transpose(1, 0, 2)
    pair_spec = pl.BlockSpec((s, LANES), lambda p, bi: (bi, p))
    o_near, lse_near = pl.pallas_call(
        _dilated_near_kernel,
        out_shape=(jax.ShapeDtypeStruct((n_tok, B_W), BF16),
                   jax.ShapeDtypeStruct((n_tok, B_W), F32)),
        grid=(pairs, b),
        in_specs=[pair_spec, pair_spec, pair_spec,
                  pl.BlockSpec((None, NUM_BUCKETS, heads_per_pair), lambda p, bi: (p, 0, 0)),
                  _resident(onehot_n.shape), _resident(logm_n.shape)],
        out_specs=(pair_spec, pair_spec),
        scratch_shapes=[pltpu.VMEM((heads_per_pair, tq, 3 * tq), F32)],
        compiler_params=_params("parallel", "arbitrary"),
        name="dilated_near",
    )(qb0, kb0, vb0, rb_pairs, onehot_n, logm_n)

    far_w, far_d = B_FAR_BRANCH
    cl = s // far_d
    assert (far_w // (2 * far_d)) * 2 == cl and 2 * cl == tq
    onehot_f, logm_f = _bias_structure((np.arange(2 * cl) - cl) * far_d, (B_FAR_BRANCH,))
    fold = lambda a: a.reshape(b, cl, far_d * a.shape[-1])
    fold_spec = pl.BlockSpec((None, cl, far_d * B_W), lambda bi: (bi, 0, 0))
    o_b = pl.pallas_call(
        _dilated_far_kernel,
        out_shape=jax.ShapeDtypeStruct((b, cl, far_d * B_W), BF16),
        grid=(b,),
        in_specs=[fold_spec, fold_spec, fold_spec, fold_spec, fold_spec,
                  _resident(rb.shape), _resident(onehot_f.shape), _resident(logm_f.shape)],
        out_specs=fold_spec,
        scratch_shapes=[pltpu.VMEM((B_HEADS, 2 * cl, 2 * cl), F32)],
        compiler_params=_params("arbitrary"),
        name="dilated_far",
    )(fold(qb0), fold(kb0), fold(vb0), fold(o_near), fold(lse_near), rb, onehot_f, logm_f)
    o_b = o_b.reshape(n_tok, B_W)

    h1 = _mlp_call(h0, [o_a], [o_b], s, ab_w_out[0].astype(BF16), norm_mlp_g[0],
                   mlp_w1[0].astype(BF16), mlp_w2[0].astype(BF16), None, "mix0_mlp0")

    scale_c = C_QK_DIM ** -0.5 * LOG2E
    tables_q = _mla_tables(s, scale_c)
    tables_k = _mla_tables(s, 1.0)
    zeros = lambda r, c: jnp.zeros((r, c), F32)
    wd = c_w_down[0]
    kv_end = C_Q_RANK + C_KV_RANK
    wd_p = jnp.concatenate([wd[:, :kv_end], zeros(d, C_NOPE_DIM), wd[:, kv_end:],
                            _rotate_half_cols(wd[:, kv_end:])], axis=1).astype(BF16)
    wuq = c_w_uq[0].reshape(C_Q_RANK, C_HEADS, C_QK_DIM)
    wuq_p = jnp.concatenate([wuq, _rotate_half_cols(wuq[:, :, C_NOPE_DIM:])], axis=-1).reshape(
        C_Q_RANK, C_HEADS * LANES).astype(BF16)
    wukv = c_w_ukv[0].reshape(C_KV_RANK, C_HEADS, C_NOPE_DIM + C_V_DIM)
    wuk_p = jnp.pad(wukv[:, :, :C_NOPE_DIM], ((0, 0), (0, 0), (0, LANES - C_NOPE_DIM))).reshape(
        C_KV_RANK, C_HEADS * LANES).astype(BF16)
    wuv_t = wukv[:, :, C_NOPE_DIM:].reshape(C_KV_RANK, C_HEADS * C_V_DIM).T.astype(BF16)

    qk_w = C_HEADS * LANES
    v_w = C_HEADS * C_V_DIM
    tm = PROJ1_TOKEN_TILE
    tok, tok_t, tab_spec = _token_specs(tm, s)
    q_c, k_c, vt_c = pl.pallas_call(
        _proj1_kernel,
        out_shape=(jax.ShapeDtypeStruct((n_tok, qk_w), BF16),
                   jax.ShapeDtypeStruct((n_tok, qk_w), BF16),
                   jax.ShapeDtypeStruct((b, v_w, s), BF16)),
        grid=(n_tok // tm,),
        in_specs=[tok(d), _resident((1, d)), _resident(wd_p.shape),
                  _resident((1, C_Q_RANK)), _resident((1, C_KV_RANK)),
                  _resident(wuq_p.shape), _resident(wuk_p.shape), _resident(wuv_t.shape)]
                 + [tab_spec] * 4,
        out_specs=(tok(qk_w), tok(qk_w), tok_t(v_w)),
        compiler_params=_params("parallel"),
        name="proj1",
    )(h1, _row2(norm_mix_g[1]), wd_p, _row2(c_q_norm_g[0]), _row2(c_kv_norm_g[0]),
      wuq_p, wuk_p, wuv_t, *tables_q, *tables_k)

    o_c = pl.pallas_call(
        _mla_kernel,
        out_shape=jax.ShapeDtypeStruct((b, v_w, s), BF16),
        grid=(b, C_HEADS // 2),
        in_specs=[pl.BlockSpec((s, 2 * LANES), lambda bi, p: (bi, p)),
                  pl.BlockSpec((s, 2 * LANES), lambda bi, p: (bi, p)),
                  pl.BlockSpec((None, 2 * C_V_DIM, s), lambda bi, p: (bi, p, 0))],
        out_specs=pl.BlockSpec((None, 2 * C_V_DIM, s), lambda bi, p: (bi, p, 0)),
        compiler_params=_params("parallel", "parallel"),
        name="mla",
    )(q_c, k_c, vt_c)

    out = _mlp_call(h1, [o_c], [], s, c_w_out[0].astype(BF16), norm_mlp_g[1],
                    mlp_w1[1].astype(BF16), mlp_w2[1].astype(BF16), final_norm_g, "mix1_mlp1")
    return out.reshape(b, s, d)
```

```python
import functools
import math

import jax
import jax.numpy as jnp
import numpy as np
from jax import lax
from jax.experimental import pallas as pl
from jax.experimental.pallas import tpu as pltpu

F32 = jnp.float32
BF16 = jnp.bfloat16

D_MODEL = 1024
GRID_W = 64
HEAD_DIM = 64
ROPE_THETA = 10000.0
EPS = 1e-6
NEG_INF = -1e30
LOG2E = math.log2(math.e)

A_HEADS = 8
A_KV_HEADS = 2
B_HEADS = 8
B_BRANCHES = ((128, 1), (512, 4), (2048, 16))
NUM_BUCKETS = 32
REL_MAX_DISTANCE = 1024

C_HEADS = 16
C_Q_RANK = 256
C_KV_RANK = 128
C_NOPE_DIM = 64
C_ROPE_DIM = 32
C_V_DIM = 64
C_QK_DIM = C_NOPE_DIM + C_ROPE_DIM
D_FF = 4 * D_MODEL

A_Q_W = A_HEADS * HEAD_DIM
A_KV_W = A_KV_HEADS * HEAD_DIM
B_W = B_HEADS * HEAD_DIM

LANES = 128
VMEM_LIMIT_BYTES = 56 * 1024 * 1024

PROJ0_TOKEN_TILE = 512
PROJ1_TOKEN_TILE = 1024
MLP_TOKEN_TILE = 1024
ATTN_Q_TILE = 256
FF_CHUNK = 1024

COL_QA = 0
COL_KA = COL_QA + A_Q_W
COL_VA = COL_KA + A_KV_W
COL_QB = COL_VA + A_KV_W
COL_KB = COL_QB + B_W
COL_VB = COL_KB + B_W
OUT_QB = COL_VA
OUT_KB = OUT_QB + B_W
QK0_W = OUT_KB + B_W
ROW_VA = 0
ROW_VB = A_KV_W
VT0_ROWS = ROW_VB + B_W


def _rms(x):
    return x * lax.rsqrt(jnp.mean(x * x, axis=-1, keepdims=True) + EPS)


def _dot(a, b):
    return jnp.dot(a, b, preferred_element_type=F32)


def _dot_nt(a, b):
    return lax.dot_general(a, b, (((1,), (1,)), ((), ())), preferred_element_type=F32)


def _dot_tn(a, b):
    return lax.dot_general(a, b, (((0,), (0,)), ((), ())), preferred_element_type=F32)


def _with_ones(vt):
    return jnp.concatenate([vt, jnp.ones_like(vt)], axis=0)


def _softmax_pv(s, vt_ones):
    dv = vt_ones.shape[0] // 2
    m = jnp.max(s, axis=-1, keepdims=True)
    p = jnp.exp2(s - m).astype(BF16)
    acc = _dot(vt_ones, p.T)
    return acc[:dv] / acc[dv:dv + 1]


def _pipelined(blocks, scores, finish):
    s_next = scores(blocks[0])
    for n, blk in enumerate(blocks):
        s_cur = s_next
        if n + 1 < len(blocks):
            s_next = scores(blocks[n + 1])
        finish(blk, s_cur)


def _proj0_kernel(x_ref, g_ref, w_ref, gq_ref, gk_ref, c_ref, sa_ref, sb_ref, qk_ref, vt_ref):
    tm = x_ref.shape[0]
    xn = (_rms(x_ref[...]) * g_ref[...]).astype(BF16)
    low = lax.broadcasted_iota(jnp.int32, (tm, LANES), 1) < HEAD_DIM
    cos, sin_a, sin_b = c_ref[...], sa_ref[...], sb_ref[...]

    def norm_rope(y, gain):
        sq = y * y
        s_all = jnp.sum(sq, axis=-1, keepdims=True)
        s_low = jnp.sum(jnp.where(low, sq, 0.0), axis=-1, keepdims=True)
        ms = jnp.where(low, s_low, s_all - s_low) * (1.0 / HEAD_DIM)
        y = y * lax.rsqrt(ms + EPS) * gain
        return (y * cos + pltpu.roll(y, LANES - HEAD_DIM // 2, 1) * sin_a
                + pltpu.roll(y, HEAD_DIM // 2, 1) * sin_b)

    scale = HEAD_DIM ** -0.5 * LOG2E
    y = _dot(xn, w_ref[:, COL_QA:COL_QB])
    for c0 in range(COL_QA, COL_KA, LANES):
        qk_ref[:, c0:c0 + LANES] = (norm_rope(y[:, c0:c0 + LANES], gq_ref[...]) * scale).astype(BF16)
    qk_ref[:, COL_KA:COL_VA] = norm_rope(y[:, COL_KA:COL_VA], gk_ref[...]).astype(BF16)
    vt_ref[ROW_VA:ROW_VB, :] = y[:, COL_VA:COL_QB].T.astype(BF16)
    qk_ref[:, OUT_QB:OUT_KB] = (_dot(xn, w_ref[:, COL_QB:COL_KB]) * scale).astype(BF16)
    qk_ref[:, OUT_KB:] = _dot(xn, w_ref[:, COL_KB:COL_VB]).astype(BF16)
    vt_ref[ROW_VB:, :] = _dot(xn, w_ref[:, COL_VB:]).T.astype(BF16)


def _gqa_kernel(q_ref, k_ref, vt_ref, o_ref):
    rep = A_HEADS // A_KV_HEADS
    head = lambda i: slice(i * HEAD_DIM, (i + 1) * HEAD_DIM)
    keys = [k_ref[:, head(g)] for g in range(A_KV_HEADS)]
    vt_ones = [_with_ones(vt_ref[head(g), :]) for g in range(A_KV_HEADS)]

    def scores(h):
        return _dot_nt(q_ref[:, head(h)], keys[h // rep])

    def finish(h, s):
        o_ref[head(h), :] = _softmax_pv(s, vt_ones[h // rep]).astype(o_ref.dtype)

    _pipelined(list(range(A_HEADS)), scores, finish)


def _dilated_kernel(q_ref, k_ref, vt_ref, rb_ref, onehot_ref, logm_ref, o_ref, e_ref):
    s_len = q_ref.shape[0]
    blk = ATTN_Q_TILE
    n_blk = s_len // blk
    reach = -(-max((w // (2 * d)) * d for w, d in B_BRANCHES) // blk)
    heads = LANES // HEAD_DIM
    e_cols = (2 * n_blk - 1) * blk

    @pl.when(pl.program_id(1) == 0)
    def _():
        for hh in range(heads):
            per_delta = jnp.sum(onehot_ref[...] * rb_ref[:, hh:hh + 1], axis=0, keepdims=True)
            per_delta = (per_delta + logm_ref[...]) * LOG2E
            spread = jnp.broadcast_to(per_delta, (blk, 2 * s_len))
            toeplitz = pltpu.roll(spread, 0, 1, stride=1, stride_axis=0)
            e_ref[hh] = toeplitz[:, blk:blk + e_cols]

    head = lambda hh: slice(hh * HEAD_DIM, (hh + 1) * HEAD_DIM)
    vt_ones = [_with_ones(vt_ref[head(hh), :]) for hh in range(heads)]

    def key_range(i):
        return max(0, i - reach) * blk, min(n_blk, i + reach + 1) * blk

    def scores(block):
        hh, i = block
        k_lo, k_hi = key_range(i)
        off = (n_blk - 1 - i) * blk
        s = _dot_nt(q_ref[i * blk:(i + 1) * blk, head(hh)], k_ref[k_lo:k_hi, head(hh)])
        return s + e_ref[hh, :, off + k_lo:off + k_hi]

    def finish(block, s):
        hh, i = block
        k_lo, k_hi = key_range(i)
        o = _softmax_pv(s, vt_ones[hh][:, k_lo:k_hi])
        o_ref[head(hh), i * blk:(i + 1) * blk] = o.astype(o_ref.dtype)

    _pipelined([(hh, i) for hh in range(heads) for i in range(n_blk)], scores, finish)


def _mlp_kernel(n_mix, final_norm, *refs):
    h_ref = refs[0]
    mix_refs = refs[1:1 + n_mix]
    wo_ref, g_ref, w1_ref, w2_ref = refs[1 + n_mix:5 + n_mix]
    gf_ref = refs[5 + n_mix] if final_norm else None
    out_ref = refs[-1]

    mix_t = jnp.concatenate([m_ref[...] for m_ref in mix_refs], axis=0)
    h = h_ref[...] + _dot_tn(mix_t, wo_ref[...])
    hn = (_rms(h) * g_ref[...]).astype(BF16)
    for c0 in range(0, D_FF, FF_CHUNK):
        a = jnp.maximum(_dot(hn, w1_ref[:, c0:c0 + FF_CHUNK]), 0.0)
        h = h + _dot((a * a).astype(BF16), w2_ref[c0:c0 + FF_CHUNK, :])
    if final_norm:
        h = _rms(h) * gf_ref[...]
    out_ref[...] = h


def _proj1_kernel(h_ref, g_ref, wd_ref, gq_ref, gkv_ref, wuq_ref, wuk_ref, wuvt_ref,
                  cq_ref, sq_ref, ck_ref, sk_ref, q_ref, k_ref, vt_ref):
    xn = (_rms(h_ref[...]) * g_ref[...]).astype(BF16)
    hd = _dot(xn, wd_ref[...])
    c_q = (_rms(hd[:, :C_Q_RANK]) * gq_ref[...]).astype(BF16)
    c_kv = (_rms(hd[:, C_Q_RANK:C_Q_RANK + C_KV_RANK]) * gkv_ref[...]).astype(BF16)

    def rope(y, cos, sin):
        return y * cos + pltpu.roll(y, LANES - C_ROPE_DIM, 1) * sin

    cq, sq = cq_ref[...], sq_ref[...]
    k_rope = rope(hd[:, C_Q_RANK + C_KV_RANK:], ck_ref[...], sk_ref[...])
    for c0 in range(0, C_HEADS * LANES, 2 * LANES):
        q2 = _dot(c_q, wuq_ref[:, c0:c0 + 2 * LANES])
        k2 = _dot(c_kv, wuk_ref[:, c0:c0 + 2 * LANES])
        for lo in (0, LANES):
            lanes = slice(c0 + lo, c0 + lo + LANES)
            q_ref[:, lanes] = rope(q2[:, lo:lo + LANES], cq, sq).astype(BF16)
            k_ref[:, lanes] = (k2[:, lo:lo + LANES] + k_rope).astype(BF16)
    vt_ref[...] = _dot_nt(wuvt_ref[...], c_kv).astype(BF16)


def _mla_kernel(q_ref, k_ref, vt_ref, o_ref):
    s_len = q_ref.shape[0]
    tq = ATTN_Q_TILE
    v_rows = lambda hh: slice(hh * C_V_DIM, (hh + 1) * C_V_DIM)
    vt_ones = [_with_ones(vt_ref[v_rows(hh), :]) for hh in range(2)]

    def scores(block):
        hh, i = block
        lanes = slice(hh * LANES, (hh + 1) * LANES)
        return _dot_nt(q_ref[i * tq:(i + 1) * tq, lanes], k_ref[:, lanes])

    def finish(block, s):
        hh, i = block
        o_ref[v_rows(hh), i * tq:(i + 1) * tq] = _softmax_pv(s, vt_ones[hh]).astype(o_ref.dtype)

    _pipelined([(hh, i) for hh in range(2) for i in range(s_len // tq)], scores, finish)


def _rope_angles(pos, dim):
    inv_freq = ROPE_THETA ** (-np.arange(0, dim, 2, dtype=np.float64) / dim)
    return pos.astype(np.float64)[:, None] * inv_freq[None, :]


def _axial_tables(s_len):
    t = np.arange(s_len)
    ang = np.concatenate([_rope_angles(t // GRID_W, HEAD_DIM // 2),
                          _rope_angles(t % GRID_W, HEAD_DIM // 2)], axis=-1)
    cos, sin, zero = np.cos(ang), np.sin(ang), np.zeros_like(ang)
    c = np.concatenate([cos, cos, cos, cos], axis=-1)
    sa = np.concatenate([-sin, zero, -sin, zero], axis=-1)
    sb = np.concatenate([zero, sin, zero, sin], axis=-1)
    return [jnp.asarray(a, F32) for a in (c, sa, sb)]


def _mla_tables(s_len, scale):
    ang = _rope_angles(np.arange(s_len), C_ROPE_DIM)
    cos, sin = np.cos(ang), np.sin(ang)
    ones = np.ones((s_len, C_NOPE_DIM))
    z_nope = np.zeros((s_len, C_NOPE_DIM))
    z_tail = np.zeros((s_len, LANES - C_QK_DIM))
    c = np.concatenate([ones, cos, cos, z_tail], axis=-1)
    sn = np.concatenate([z_nope, sin, sin, z_tail], axis=-1)
    return [jnp.asarray(a * scale, F32) for a in (c, sn)]


def _rotate_half_cols(w):
    half = w.shape[-1] // 2
    return jnp.concatenate([-w[..., half:], w[..., :half]], axis=-1)


def _t5_bucket(rel):
    nb = NUM_BUCKETS // 2
    max_exact = nb // 2
    base = np.where(rel > 0, nb, 0)
    n = np.abs(rel)
    nf = np.maximum(n, 1).astype(np.float32)
    large = max_exact + (np.log(nf / np.float32(max_exact)) / np.float32(math.log(REL_MAX_DISTANCE / max_exact))
                         * np.float32(nb - max_exact)).astype(np.int32)
    large = np.minimum(large, nb - 1)
    return base + np.where(n < max_exact, n, large)


def _dilated_structure(s_len):
    delta = np.arange(2 * s_len) - s_len
    mult = np.zeros(delta.shape, np.int32)
    for w, d in B_BRANCHES:
        n_side = w // (2 * d)
        mult += ((delta % d == 0) & (np.abs(delta) <= n_side * d)).astype(np.int32)
    log_mult = np.where(mult > 0, np.log(np.maximum(mult, 1).astype(np.float64)), NEG_INF)
    onehot = (_t5_bucket(delta)[None, :] == np.arange(NUM_BUCKETS)[:, None]) & (mult > 0)[None, :]
    return jnp.asarray(onehot, F32), jnp.asarray(log_mult[None, :], F32)


def _params(*sem):
    return pltpu.CompilerParams(dimension_semantics=sem, vmem_limit_bytes=VMEM_LIMIT_BYTES)


def _resident(shape):
    return pl.BlockSpec(shape, lambda *_: (0,) * len(shape), pipeline_mode=pl.Buffered(1))


def _row2(v):
    return v.reshape(1, -1).astype(F32)


def _token_specs(tm, s_len):
    seq_tiles = s_len // tm
    tok = lambda w: pl.BlockSpec((tm, w), lambda i: (i, 0))
    tok_t = lambda w: pl.BlockSpec((None, w, tm), lambda i: (i // seq_tiles, 0, i % seq_tiles))
    table = pl.BlockSpec((tm, LANES), lambda i: (i % seq_tiles, 0))
    return tok, tok_t, table


def _mlp_call(h, mixes_t, wo, g, w1, w2, gf, name):
    n_tok = h.shape[0]
    tm = MLP_TOKEN_TILE
    tok, tok_t, _ = _token_specs(tm, mixes_t[0].shape[2])
    in_specs = [tok(D_MODEL)] + [tok_t(m.shape[1]) for m in mixes_t] + [
        _resident(wo.shape), _resident((1, D_MODEL)), _resident(w1.shape), _resident(w2.shape)]
    args = [h, *mixes_t, wo, _row2(g), w1, w2]
    if gf is not None:
        in_specs.append(_resident((1, D_MODEL)))
        args.append(_row2(gf))
    return pl.pallas_call(
        functools.partial(_mlp_kernel, len(mixes_t), gf is not None),
        out_shape=jax.ShapeDtypeStruct((n_tok, D_MODEL), F32),
        grid=(n_tok // tm,),
        in_specs=in_specs,
        out_specs=tok(D_MODEL),
        compiler_params=_params("parallel"),
        name=name,
    )(*args)


def kernel(x, norm_mix_g, norm_mlp_g, ab_w_in, a_q_norm_g, a_k_norm_g, ab_w_out, rel_bias,
           c_w_down, c_q_norm_g, c_kv_norm_g, c_w_uq, c_w_ukv, c_w_out, mlp_w1, mlp_w2,
           final_norm_g):
    b, s, d = x.shape
    n_tok = b * s
    h0 = x.reshape(n_tok, d)
    tm = PROJ0_TOKEN_TILE
    tok, tok_t, tab_spec = _token_specs(tm, s)

    tables0 = _axial_tables(s)
    gq = _row2(jnp.tile(a_q_norm_g[0], LANES // HEAD_DIM))
    gk = _row2(jnp.tile(a_k_norm_g[0], LANES // HEAD_DIM))
    w_in = ab_w_in[0].astype(BF16)
    qk0, vt0 = pl.pallas_call(
        _proj0_kernel,
        out_shape=(jax.ShapeDtypeStruct((n_tok, QK0_W), BF16),
                   jax.ShapeDtypeStruct((b, VT0_ROWS, s), BF16)),
        grid=(n_tok // tm,),
        in_specs=[tok(d), _resident((1, d)), _resident(w_in.shape),
                  _resident((1, LANES)), _resident((1, LANES)), tab_spec, tab_spec, tab_spec],
        out_specs=(tok(QK0_W), tok_t(VT0_ROWS)),
        compiler_params=_params("parallel"),
        name="proj0",
    )(h0, _row2(norm_mix_g[0]), w_in, gq, gk, *tables0)

    tq = ATTN_Q_TILE
    q_tiles = s // tq
    o_a = pl.pallas_call(
        _gqa_kernel,
        out_shape=jax.ShapeDtypeStruct((b, A_Q_W, s), BF16),
        grid=(b, q_tiles),
        in_specs=[pl.BlockSpec((tq, A_Q_W), lambda bi, qi: (bi * q_tiles + qi, COL_QA // A_Q_W)),
                  pl.BlockSpec((s, A_KV_W), lambda bi, qi: (bi, COL_KA // A_KV_W)),
                  pl.BlockSpec((None, A_KV_W, s), lambda bi, qi: (bi, ROW_VA // A_KV_W, 0))],
        out_specs=pl.BlockSpec((None, A_Q_W, tq), lambda bi, qi: (bi, 0, qi)),
        compiler_params=_params("parallel", "parallel"),
        name="gqa",
    )(qk0, qk0, vt0)

    onehot, log_mult = _dilated_structure(s)
    pairs = B_W // LANES
    heads_per_pair = LANES // HEAD_DIM
    rb = rel_bias.astype(F32).reshape(NUM_BUCKETS, pairs, heads_per_pair).transpose(1, 0, 2)
    o_b = pl.pallas_call(
        _dilated_kernel,
        out_shape=jax.ShapeDtypeStruct((b, B_W, s), BF16),
        grid=(pairs, b),
        in_specs=[pl.BlockSpec((s, LANES), lambda p, bi: (bi, OUT_QB // LANES + p)),
                  pl.BlockSpec((s, LANES), lambda p, bi: (bi, OUT_KB // LANES + p)),
                  pl.BlockSpec((None, LANES, s), lambda p, bi: (bi, ROW_VB // LANES + p, 0)),
                  pl.BlockSpec((None, NUM_BUCKETS, heads_per_pair), lambda p, bi: (p, 0, 0)),
                  _resident(onehot.shape), _resident(log_mult.shape)],
        out_specs=pl.BlockSpec((None, LANES, s), lambda p, bi: (bi, p, 0)),
        scratch_shapes=[pltpu.VMEM((heads_per_pair, tq, (2 * q_tiles - 1) * tq), F32)],
        compiler_params=_params("parallel", "arbitrary"),
        name="dilated",
    )(qk0, qk0, vt0, rb, onehot, log_mult)

    h1 = _mlp_call(h0, [o_a, o_b], ab_w_out[0].astype(BF16), norm_mlp_g[0],
                   mlp_w1[0].astype(BF16), mlp_w2[0].astype(BF16), None, "mix0_mlp0")

    scale_c = C_QK_DIM ** -0.5 * LOG2E
    tables_q = _mla_tables(s, scale_c)
    tables_k = _mla_tables(s, 1.0)
    zeros = lambda r, c: jnp.zeros((r, c), F32)
    wd = c_w_down[0]
    kv_end = C_Q_RANK + C_KV_RANK
    wd_p = jnp.concatenate([wd[:, :kv_end], zeros(d, C_NOPE_DIM), wd[:, kv_end:],
                            _rotate_half_cols(wd[:, kv_end:])], axis=1).astype(BF16)
    wuq = c_w_uq[0].reshape(C_Q_RANK, C_HEADS, C_QK_DIM)
    wuq_p = jnp.concatenate([wuq, _rotate_half_cols(wuq[:, :, C_NOPE_DIM:])], axis=-1).reshape(
        C_Q_RANK, C_HEADS * LANES).astype(BF16)
    wukv = c_w_ukv[0].reshape(C_KV_RANK, C_HEADS, C_NOPE_DIM + C_V_DIM)
    wuk_p = jnp.pad(wukv[:, :, :C_NOPE_DIM], ((0, 0), (0, 0), (0, LANES - C_NOPE_DIM))).reshape(
        C_KV_RANK, C_HEADS * LANES).astype(BF16)
    wuv_t = wukv[:, :, C_NOPE_DIM:].reshape(C_KV_RANK, C_HEADS * C_V_DIM).T.astype(BF16)

    qk_w = C_HEADS * LANES
    v_w = C_HEADS * C_V_DIM
    tm = PROJ1_TOKEN_TILE
    tok, tok_t, tab_spec = _token_specs(tm, s)
    q_c, k_c, vt_c = pl.pallas_call(
        _proj1_kernel,
        out_shape=(jax.ShapeDtypeStruct((n_tok, qk_w), BF16),
                   jax.ShapeDtypeStruct((n_tok, qk_w), BF16),
                   jax.ShapeDtypeStruct((b, v_w, s), BF16)),
        grid=(n_tok // tm,),
        in_specs=[tok(d), _resident((1, d)), _resident(wd_p.shape),
                  _resident((1, C_Q_RANK)), _resident((1, C_KV_RANK)),
                  _resident(wuq_p.shape), _resident(wuk_p.shape), _resident(wuv_t.shape)]
                 + [tab_spec] * 4,
        out_specs=(tok(qk_w), tok(qk_w), tok_t(v_w)),
        compiler_params=_params("parallel"),
        name="proj1",
    )(h1, _row2(norm_mix_g[1]), wd_p, _row2(c_q_norm_g[0]), _row2(c_kv_norm_g[0]),
      wuq_p, wuk_p, wuv_t, *tables_q, *tables_k)

    o_c = pl.pallas_call(
        _mla_kernel,
        out_shape=jax.ShapeDtypeStruct((b, v_w, s), BF16),
        grid=(b, C_HEADS // 2),
        in_specs=[pl.BlockSpec((s, 2 * LANES), lambda bi, p: (bi, p)),
                  pl.BlockSpec((s, 2 * LANES), lambda bi, p: (bi, p)),
                  pl.BlockSpec((None, 2 * C_V_DIM, s), lambda bi, p: (bi, p, 0))],
        out_specs=pl.BlockSpec((None, 2 * C_V_DIM, s), lambda bi, p: (bi, p, 0)),
        compiler_params=_params("parallel", "parallel"),
        name="mla",
    )(q_c, k_c, vt_c)

    out = _mlp_call(h1, [o_c], c_w_out[0].astype(BF16), norm_mlp_g[1],
                    mlp_w1[1].astype(BF16), mlp_w2[1].astype(BF16), final_norm_g, "mix1_mlp1")
    return out.reshape(b, s, d)
```

```python
import functools
import math

import jax
import jax.numpy as jnp
import numpy as np
from jax import lax
from jax.experimental import pallas as pl
from jax.experimental.pallas import tpu as pltpu

F32 = jnp.float32
BF16 = jnp.bfloat16

D_MODEL = 1024
GRID_W = 64
HEAD_DIM = 64
ROPE_THETA = 10000.0
EPS = 1e-6
NEG_INF = -1e30
LOG2E = math.log2(math.e)

A_HEADS = 8
A_KV_HEADS = 2
B_HEADS = 8
B_BRANCHES = ((128, 1), (512, 4), (2048, 16))
NUM_BUCKETS = 32
REL_MAX_DISTANCE = 1024

C_HEADS = 16
C_Q_RANK = 256
C_KV_RANK = 128
C_NOPE_DIM = 64
C_ROPE_DIM = 32
C_V_DIM = 64
C_QK_DIM = C_NOPE_DIM + C_ROPE_DIM
D_FF = 4 * D_MODEL

A_Q_W = A_HEADS * HEAD_DIM
A_KV_W = A_KV_HEADS * HEAD_DIM
B_W = B_HEADS * HEAD_DIM

LANES = 128
VMEM_LIMIT_BYTES = 56 * 1024 * 1024

PROJ0_TOKEN_TILE = 512
PROJ1_TOKEN_TILE = 1024
MLP_TOKEN_TILE = 1024
ATTN_Q_TILE = 256
GQA_Q_ROWS = 512
MLA_HEADS_PER_STEP = 4
FF_CHUNK = 1024

COL_QA = 0
COL_KA = COL_QA + A_Q_W
COL_VA = COL_KA + A_KV_W
COL_QB = COL_VA + A_KV_W
COL_KB = COL_QB + B_W
COL_VB = COL_KB + B_W
OUT_QA = 0
OUT_QB = OUT_QA + A_Q_W
OUT_KB = OUT_QB + B_W
OUT_KA = OUT_KB + B_W
QK0_W = OUT_KA + A_KV_W
ROW_VB = 0
ROW_VA = ROW_VB + B_W
VT0_ROWS = ROW_VA + A_KV_W
DIL_HEADS_PER_STEP = 4


def _rms(x):
    return x * lax.rsqrt(jnp.mean(x * x, axis=-1, keepdims=True) + EPS)


def _dot(a, b):
    return jnp.dot(a, b, preferred_element_type=F32)


def _dot_nt(a, b):
    return lax.dot_general(a, b, (((1,), (1,)), ((), ())), preferred_element_type=F32)


def _dot_tn(a, b):
    return lax.dot_general(a, b, (((0,), (0,)), ((), ())), preferred_element_type=F32)


def _with_ones(vt):
    return jnp.concatenate([vt, jnp.ones_like(vt)], axis=0)


def _softmax_pv(s, vt_ones):
    dv = vt_ones.shape[0] // 2
    m = jnp.max(s, axis=-1, keepdims=True)
    p = jnp.exp2(s - m).astype(BF16)
    acc = _dot(vt_ones, p.T)
    return acc[:dv] / acc[dv:dv + 1]


def _pipelined(blocks, scores, finish):
    s_next = scores(blocks[0])
    for n, blk in enumerate(blocks):
        s_cur = s_next
        if n + 1 < len(blocks):
            s_next = scores(blocks[n + 1])
        finish(blk, s_cur)


def _proj0_kernel(x_ref, g_ref, w_ref, gq_ref, gk_ref, c_ref, sa_ref, sb_ref, qk_ref, vt_ref):
    tm = x_ref.shape[0]
    xn = (_rms(x_ref[...]) * g_ref[...]).astype(BF16)
    low = lax.broadcasted_iota(jnp.int32, (tm, LANES), 1) < HEAD_DIM
    cos, sin_a, sin_b = c_ref[...], sa_ref[...], sb_ref[...]

    def norm_rope(y, gain):
        sq = y * y
        s_all = jnp.sum(sq, axis=-1, keepdims=True)
        s_low = jnp.sum(jnp.where(low, sq, 0.0), axis=-1, keepdims=True)
        ms = jnp.where(low, s_low, s_all - s_low) * (1.0 / HEAD_DIM)
        y = y * lax.rsqrt(ms + EPS) * gain
        return (y * cos + pltpu.roll(y, LANES - HEAD_DIM // 2, 1) * sin_a
                + pltpu.roll(y, HEAD_DIM // 2, 1) * sin_b)

    scale = HEAD_DIM ** -0.5 * LOG2E
    y = _dot(xn, w_ref[:, COL_QA:COL_QB])
    for c0 in range(0, A_Q_W, LANES):
        q = norm_rope(y[:, COL_QA + c0:COL_QA + c0 + LANES], gq_ref[...]) * scale
        qk_ref[:, OUT_QA + c0:OUT_QA + c0 + LANES] = q.astype(BF16)
    qk_ref[:, OUT_KA:] = norm_rope(y[:, COL_KA:COL_VA], gk_ref[...]).astype(BF16)
    vt_ref[ROW_VA:, :] = y[:, COL_VA:COL_QB].T.astype(BF16)
    qk_ref[:, OUT_QB:OUT_KB] = (_dot(xn, w_ref[:, COL_QB:COL_KB]) * scale).astype(BF16)
    qk_ref[:, OUT_KB:OUT_KA] = _dot(xn, w_ref[:, COL_KB:COL_VB]).astype(BF16)
    vt_ref[ROW_VB:ROW_VA, :] = _dot(xn, w_ref[:, COL_VB:]).T.astype(BF16)


def _gqa_kernel(q_ref, k_ref, vt_ref, o_ref):
    rep = A_HEADS // A_KV_HEADS
    head = lambda i: slice(i * HEAD_DIM, (i + 1) * HEAD_DIM)
    keys = [k_ref[:, head(g)] for g in range(A_KV_HEADS)]
    vt_ones = [_with_ones(vt_ref[head(g), :]) for g in range(A_KV_HEADS)]

    tq = ATTN_Q_TILE
    tile = lambda i: slice(i * tq, (i + 1) * tq)

    def scores(block):
        i, h = block
        return _dot_nt(q_ref[tile(i), head(h)], keys[h // rep])

    def finish(block, s):
        i, h = block
        o_ref[head(h), tile(i)] = _softmax_pv(s, vt_ones[h // rep]).astype(o_ref.dtype)

    _pipelined([(i, h) for i in range(q_ref.shape[0] // tq) for h in range(A_HEADS)],
               scores, finish)


def _dilated_kernel(q_ref, k_ref, vt_ref, rb_ref, onehot_ref, logm_ref, o_ref, e_ref):
    s_len = q_ref.shape[0]
    blk = ATTN_Q_TILE
    n_blk = s_len // blk
    reach = -(-max((w // (2 * d)) * d for w, d in B_BRANCHES) // blk)
    heads = q_ref.shape[1] // HEAD_DIM
    e_cols = (2 * n_blk - 1) * blk

    @pl.when(pl.program_id(1) == 0)
    def _():
        for hh in range(heads):
            per_delta = jnp.sum(onehot_ref[...] * rb_ref[:, hh:hh + 1], axis=0, keepdims=True)
            per_delta = (per_delta + logm_ref[...]) * LOG2E
            spread = jnp.broadcast_to(per_delta, (blk, 2 * s_len))
            toeplitz = pltpu.roll(spread, 0, 1, stride=1, stride_axis=0)
            e_ref[hh] = toeplitz[:, blk:blk + e_cols]

    head = lambda hh: slice(hh * HEAD_DIM, (hh + 1) * HEAD_DIM)
    vt_ones = [_with_ones(vt_ref[head(hh), :]) for hh in range(heads)]

    def key_range(i):
        return max(0, i - reach) * blk, min(n_blk, i + reach + 1) * blk

    def scores(block):
        hh, i = block
        k_lo, k_hi = key_range(i)
        off = (n_blk - 1 - i) * blk
        s = _dot_nt(q_ref[i * blk:(i + 1) * blk, head(hh)], k_ref[k_lo:k_hi, head(hh)])
        return s + e_ref[hh, :, off + k_lo:off + k_hi]

    def finish(block, s):
        hh, i = block
        k_lo, k_hi = key_range(i)
        o = _softmax_pv(s, vt_ones[hh][:, k_lo:k_hi])
        o_ref[head(hh), i * blk:(i + 1) * blk] = o.astype(o_ref.dtype)

    _pipelined([(hh, i) for hh in range(heads) for i in range(n_blk)], scores, finish)


def _mlp_kernel(n_mix, final_norm, *refs):
    h_ref = refs[0]
    mix_refs = refs[1:1 + n_mix]
    wo_ref, g_ref, w1_ref, w2_ref = refs[1 + n_mix:5 + n_mix]
    gf_ref = refs[5 + n_mix] if final_norm else None
    out_ref = refs[-1]

    mix_t = jnp.concatenate([m_ref[...] for m_ref in mix_refs], axis=0)
    h = h_ref[...] + _dot_tn(mix_t, wo_ref[...])
    hn = (_rms(h) * g_ref[...]).astype(BF16)
    for c0 in range(0, D_FF, FF_CHUNK):
        a = jnp.maximum(_dot(hn, w1_ref[:, c0:c0 + FF_CHUNK]), 0.0)
        h = h + _dot((a * a).astype(BF16), w2_ref[c0:c0 + FF_CHUNK, :])
    if final_norm:
        h = _rms(h) * gf_ref[...]
    out_ref[...] = h


def _proj1_kernel(h_ref, g_ref, wd_ref, gq_ref, gkv_ref, wuq_ref, wuk_ref, wuvt_ref,
                  cq_ref, sq_ref, ck_ref, sk_ref, q_ref, k_ref, vt_ref):
    xn = (_rms(h_ref[...]) * g_ref[...]).astype(BF16)
    hd = _dot(xn, wd_ref[...])
    c_q = (_rms(hd[:, :C_Q_RANK]) * gq_ref[...]).astype(BF16)
    c_kv = (_rms(hd[:, C_Q_RANK:C_Q_RANK + C_KV_RANK]) * gkv_ref[...]).astype(BF16)

    def rope(y, cos, sin):
        return y * cos + pltpu.roll(y, LANES - C_ROPE_DIM, 1) * sin

    cq, sq = cq_ref[...], sq_ref[...]
    k_rope = rope(hd[:, C_Q_RANK + C_KV_RANK:], ck_ref[...], sk_ref[...])
    for c0 in range(0, C_HEADS * LANES, 2 * LANES):
        q2 = _dot(c_q, wuq_ref[:, c0:c0 + 2 * LANES])
        k2 = _dot(c_kv, wuk_ref[:, c0:c0 + 2 * LANES])
        for lo in (0, LANES):
            lanes = slice(c0 + lo, c0 + lo + LANES)
            q_ref[:, lanes] = rope(q2[:, lo:lo + LANES], cq, sq).astype(BF16)
            k_ref[:, lanes] = (k2[:, lo:lo + LANES] + k_rope).astype(BF16)
    vt_ref[...] = _dot_nt(wuvt_ref[...], c_kv).astype(BF16)


def _mla_kernel(q_ref, k_ref, vt_ref, o_ref):
    s_len = q_ref.shape[0]
    tq = ATTN_Q_TILE
    heads = MLA_HEADS_PER_STEP
    v_rows = lambda hh: slice(hh * C_V_DIM, (hh + 1) * C_V_DIM)
    vt_ones = [_with_ones(vt_ref[v_rows(hh), :]) for hh in range(heads)]

    def scores(block):
        hh, i = block
        lanes = slice(hh * LANES, (hh + 1) * LANES)
        return _dot_nt(q_ref[i * tq:(i + 1) * tq, lanes], k_ref[:, lanes])

    def finish(block, s):
        hh, i = block
        o_ref[v_rows(hh), i * tq:(i + 1) * tq] = _softmax_pv(s, vt_ones[hh]).astype(o_ref.dtype)

    _pipelined([(hh, i) for hh in range(heads) for i in range(s_len // tq)], scores, finish)


def _rope_angles(pos, dim):
    inv_freq = ROPE_THETA ** (-np.arange(0, dim, 2, dtype=np.float64) / dim)
    return pos.astype(np.float64)[:, None] * inv_freq[None, :]


def _axial_tables(s_len):
    t = np.arange(s_len)
    ang = np.concatenate([_rope_angles(t // GRID_W, HEAD_DIM // 2),
                          _rope_angles(t % GRID_W, HEAD_DIM // 2)], axis=-1)
    cos, sin, zero = np.cos(ang), np.sin(ang), np.zeros_like(ang)
    c = np.concatenate([cos, cos, cos, cos], axis=-1)
    sa = np.concatenate([-sin, zero, -sin, zero], axis=-1)
    sb = np.concatenate([zero, sin, zero, sin], axis=-1)
    return [jnp.asarray(a, F32) for a in (c, sa, sb)]


def _mla_tables(s_len, scale):
    ang = _rope_angles(np.arange(s_len), C_ROPE_DIM)
    cos, sin = np.cos(ang), np.sin(ang)
    ones = np.ones((s_len, C_NOPE_DIM))
    z_nope = np.zeros((s_len, C_NOPE_DIM))
    z_tail = np.zeros((s_len, LANES - C_QK_DIM))
    c = np.concatenate([ones, cos, cos, z_tail], axis=-1)
    sn = np.concatenate([z_nope, sin, sin, z_tail], axis=-1)
    return [jnp.asarray(a * scale, F32) for a in (c, sn)]


def _rotate_half_cols(w):
    half = w.shape[-1] // 2
    return jnp.concatenate([-w[..., half:], w[..., :half]], axis=-1)


def _t5_bucket(rel):
    nb = NUM_BUCKETS // 2
    max_exact = nb // 2
    base = np.where(rel > 0, nb, 0)
    n = np.abs(rel)
    nf = np.maximum(n, 1).astype(np.float32)
    large = max_exact + (np.log(nf / np.float32(max_exact)) / np.float32(math.log(REL_MAX_DISTANCE / max_exact))
                         * np.float32(nb - max_exact)).astype(np.int32)
    large = np.minimum(large, nb - 1)
    return base + np.where(n < max_exact, n, large)


def _dilated_structure(s_len):
    delta = np.arange(2 * s_len) - s_len
    mult = np.zeros(delta.shape, np.int32)
    for w, d in B_BRANCHES:
        n_side = w // (2 * d)
        mult += ((delta % d == 0) & (np.abs(delta) <= n_side * d)).astype(np.int32)
    log_mult = np.where(mult > 0, np.log(np.maximum(mult, 1).astype(np.float64)), NEG_INF)
    onehot = (_t5_bucket(delta)[None, :] == np.arange(NUM_BUCKETS)[:, None]) & (mult > 0)[None, :]
    return jnp.asarray(onehot, F32), jnp.asarray(log_mult[None, :], F32)


def _params(*sem):
    return pltpu.CompilerParams(dimension_semantics=sem, vmem_limit_bytes=VMEM_LIMIT_BYTES)


def _resident(shape):
    return pl.BlockSpec(shape, lambda *_: (0,) * len(shape), pipeline_mode=pl.Buffered(1))


def _row2(v):
    return v.reshape(1, -1).astype(F32)


def _token_specs(tm, s_len):
    seq_tiles = s_len // tm
    tok = lambda w: pl.BlockSpec((tm, w), lambda i: (i, 0))
    tok_t = lambda w: pl.BlockSpec((None, w, tm), lambda i: (i // seq_tiles, 0, i % seq_tiles))
    table = pl.BlockSpec((tm, LANES), lambda i: (i % seq_tiles, 0))
    return tok, tok_t, table


def _mlp_call(h, mixes_t, wo, g, w1, w2, gf, name):
    n_tok = h.shape[0]
    tm = MLP_TOKEN_TILE
    tok, tok_t, _ = _token_specs(tm, mixes_t[0].shape[2])
    in_specs = [tok(D_MODEL)] + [tok_t(m.shape[1]) for m in mixes_t] + [
        _resident(wo.shape), _resident((1, D_MODEL)), _resident(w1.shape), _resident(w2.shape)]
    args = [h, *mixes_t, wo, _row2(g), w1, w2]
    if gf is not None:
        in_specs.append(_resident((1, D_MODEL)))
        args.append(_row2(gf))
    return pl.pallas_call(
        functools.partial(_mlp_kernel, len(mixes_t), gf is not None),
        out_shape=jax.ShapeDtypeStruct((n_tok, D_MODEL), F32),
        grid=(n_tok // tm,),
        in_specs=in_specs,
        out_specs=tok(D_MODEL),
        compiler_params=_params("parallel"),
        name=name,
    )(*args)


def kernel(x, norm_mix_g, norm_mlp_g, ab_w_in, a_q_norm_g, a_k_norm_g, ab_w_out, rel_bias,
           c_w_down, c_q_norm_g, c_kv_norm_g, c_w_uq, c_w_ukv, c_w_out, mlp_w1, mlp_w2,
           final_norm_g):
    b, s, d = x.shape
    n_tok = b * s
    h0 = x.reshape(n_tok, d)
    tm = PROJ0_TOKEN_TILE
    tok, tok_t, tab_spec = _token_specs(tm, s)

    tables0 = _axial_tables(s)
    gq = _row2(jnp.tile(a_q_norm_g[0], LANES // HEAD_DIM))
    gk = _row2(jnp.tile(a_k_norm_g[0], LANES // HEAD_DIM))
    w_in = ab_w_in[0].astype(BF16)
    qk0, vt0 = pl.pallas_call(
        _proj0_kernel,
        out_shape=(jax.ShapeDtypeStruct((n_tok, QK0_W), BF16),
                   jax.ShapeDtypeStruct((b, VT0_ROWS, s), BF16)),
        grid=(n_tok // tm,),
        in_specs=[tok(d), _resident((1, d)), _resident(w_in.shape),
                  _resident((1, LANES)), _resident((1, LANES)), tab_spec, tab_spec, tab_spec],
        out_specs=(tok(QK0_W), tok_t(VT0_ROWS)),
        compiler_params=_params("parallel"),
        name="proj0",
    )(h0, _row2(norm_mix_g[0]), w_in, gq, gk, *tables0)

    tq = ATTN_Q_TILE
    q_tiles = s // tq
    gq_rows = GQA_Q_ROWS
    gq_steps = s // gq_rows
    o_a = pl.pallas_call(
        _gqa_kernel,
        out_shape=jax.ShapeDtypeStruct((b, A_Q_W, s), BF16),
        grid=(b, gq_steps),
        in_specs=[pl.BlockSpec((gq_rows, A_Q_W),
                               lambda bi, qi: (bi * gq_steps + qi, OUT_QA // A_Q_W)),
                  pl.BlockSpec((s, A_KV_W), lambda bi, qi: (bi, OUT_KA // A_KV_W)),
                  pl.BlockSpec((None, A_KV_W, s), lambda bi, qi: (bi, ROW_VA // A_KV_W, 0))],
        out_specs=pl.BlockSpec((None, A_Q_W, gq_rows), lambda bi, qi: (bi, 0, qi)),
        compiler_params=_params("parallel", "parallel"),
        name="gqa",
    )(qk0, qk0, vt0)

    onehot, log_mult = _dilated_structure(s)
    dil_heads = DIL_HEADS_PER_STEP
    dil_w = dil_heads * HEAD_DIM
    groups = B_HEADS // dil_heads
    rb = rel_bias.astype(F32).reshape(NUM_BUCKETS, groups, dil_heads).transpose(1, 0, 2)
    o_b = pl.pallas_call(
        _dilated_kernel,
        out_shape=jax.ShapeDtypeStruct((b, B_W, s), BF16),
        grid=(groups, b),
        in_specs=[pl.BlockSpec((s, dil_w), lambda p, bi: (bi, OUT_QB // dil_w + p)),
                  pl.BlockSpec((s, dil_w), lambda p, bi: (bi, OUT_KB // dil_w + p)),
                  pl.BlockSpec((None, dil_w, s), lambda p, bi: (bi, ROW_VB // dil_w + p, 0)),
                  pl.BlockSpec((None, NUM_BUCKETS, dil_heads), lambda p, bi: (p, 0, 0)),
                  _resident(onehot.shape), _resident(log_mult.shape)],
        out_specs=pl.BlockSpec((None, dil_w, s), lambda p, bi: (bi, p, 0)),
        scratch_shapes=[pltpu.VMEM((dil_heads, tq, (2 * q_tiles - 1) * tq), F32)],
        compiler_params=_params("parallel", "arbitrary"),
        name="dilated",
    )(qk0, qk0, vt0, rb, onehot, log_mult)

    h1 = _mlp_call(h0, [o_a, o_b], ab_w_out[0].astype(BF16), norm_mlp_g[0],
                   mlp_w1[0].astype(BF16), mlp_w2[0].astype(BF16), None, "mix0_mlp0")

    scale_c = C_QK_DIM ** -0.5 * LOG2E
    tables_q = _mla_tables(s, scale_c)
    tables_k = _mla_tables(s, 1.0)
    zeros = lambda r, c: jnp.zeros((r, c), F32)
    wd = c_w_down[0]
    kv_end = C_Q_RANK + C_KV_RANK
    wd_p = jnp.concatenate([wd[:, :kv_end], zeros(d, C_NOPE_DIM), wd[:, kv_end:],
                            _rotate_half_cols(wd[:, kv_end:])], axis=1).astype(BF16)
    wuq = c_w_uq[0].reshape(C_Q_RANK, C_HEADS, C_QK_DIM)
    wuq_p = jnp.concatenate([wuq, _rotate_half_cols(wuq[:, :, C_NOPE_DIM:])], axis=-1).reshape(
        C_Q_RANK, C_HEADS * LANES).astype(BF16)
    wukv = c_w_ukv[0].reshape(C_KV_RANK, C_HEADS, C_NOPE_DIM + C_V_DIM)
    wuk_p = jnp.pad(wukv[:, :, :C_NOPE_DIM], ((0, 0), (0, 0), (0, LANES - C_NOPE_DIM))).reshape(
        C_KV_RANK, C_HEADS * LANES).astype(BF16)
    wuv_t = wukv[:, :, C_NOPE_DIM:].reshape(C_KV_RANK, C_HEADS * C_V_DIM).T.astype(BF16)

    qk_w = C_HEADS * LANES
    v_w = C_HEADS * C_V_DIM
    tm = PROJ1_TOKEN_TILE
    tok, tok_t, tab_spec = _token_specs(tm, s)
    q_c, k_c, vt_c = pl.pallas_call(
        _proj1_kernel,
        out_shape=(jax.ShapeDtypeStruct((n_tok, qk_w), BF16),
                   jax.ShapeDtypeStruct((n_tok, qk_w), BF16),
                   jax.ShapeDtypeStruct((b, v_w, s), BF16)),
        grid=(n_tok // tm,),
        in_specs=[tok(d), _resident((1, d)), _resident(wd_p.shape),
                  _resident((1, C_Q_RANK)), _resident((1, C_KV_RANK)),
                  _resident(wuq_p.shape), _resident(wuk_p.shape), _resident(wuv_t.shape)]
                 + [tab_spec] * 4,
        out_specs=(tok(qk_w), tok(qk_w), tok_t(v_w)),
        compiler_params=_params("parallel"),
        name="proj1",
    )(h1, _row2(norm_mix_g[1]), wd_p, _row2(c_q_norm_g[0]), _row2(c_kv_norm_g[0]),
      wuq_p, wuk_p, wuv_t, *tables_q, *tables_k)

    o_c = pl.pallas_call(
        _mla_kernel,
        out_shape=jax.ShapeDtypeStruct((b, v_w, s), BF16),
        grid=(b, C_HEADS // MLA_HEADS_PER_STEP),
        in_specs=[pl.BlockSpec((s, MLA_HEADS_PER_STEP * LANES), lambda bi, p: (bi, p)),
                  pl.BlockSpec((s, MLA_HEADS_PER_STEP * LANES), lambda bi, p: (bi, p)),
                  pl.BlockSpec((None, MLA_HEADS_PER_STEP * C_V_DIM, s), lambda bi, p: (bi, p, 0))],
        out_specs=pl.BlockSpec((None, MLA_HEADS_PER_STEP * C_V_DIM, s), lambda bi, p: (bi, p, 0)),
        compiler_params=_params("parallel", "parallel"),
        name="mla",
    )(q_c, k_c, vt_c)

    out = _mlp_call(h1, [o_c], c_w_out[0].astype(BF16), norm_mlp_g[1],
                    mlp_w1[1].astype(BF16), mlp_w2[1].astype(BF16), final_norm_g, "mix1_mlp1")
    return out.reshape(b, s, d)
```

```python
import functools
import math

import jax
import jax.numpy as jnp
import numpy as np
from jax import lax
from jax.experimental import pallas as pl
from jax.experimental.pallas import tpu as pltpu

F32 = jnp.float32
BF16 = jnp.bfloat16

D_MODEL = 1024
GRID_W = 64
HEAD_DIM = 64
ROPE_THETA = 10000.0
EPS = 1e-6
NEG_INF = -1e30
LOG2E = math.log2(math.e)

A_HEADS = 8
A_KV_HEADS = 2
B_HEADS = 8
B_BRANCHES = ((128, 1), (512, 4), (2048, 16))
NUM_BUCKETS = 32
REL_MAX_DISTANCE = 1024

C_HEADS = 16
C_Q_RANK = 256
C_KV_RANK = 128
C_NOPE_DIM = 64
C_ROPE_DIM = 32
C_V_DIM = 64
C_QK_DIM = C_NOPE_DIM + C_ROPE_DIM
D_FF = 4 * D_MODEL

A_Q_W = A_HEADS * HEAD_DIM
A_KV_W = A_KV_HEADS * HEAD_DIM
B_W = B_HEADS * HEAD_DIM

LANES = 128
VMEM_LIMIT_BYTES = 56 * 1024 * 1024

PROJ0_TOKEN_TILE = 512
PROJ1_TOKEN_TILE = 1024
MLP_TOKEN_TILE = 1024
ATTN_Q_TILE = 256
GQA_Q_ROWS = 512
MLA_HEADS_PER_STEP = 4
FF_CHUNK = 1024

COL_QA = 0
COL_KA = COL_QA + A_Q_W
COL_VA = COL_KA + A_KV_W
COL_QB = COL_VA + A_KV_W
COL_KB = COL_QB + B_W
COL_VB = COL_KB + B_W
OUT_QA = 0
OUT_QB = OUT_QA + A_Q_W
OUT_KB = OUT_QB + B_W
OUT_KA = OUT_KB + B_W
QK0_W = OUT_KA + A_KV_W
ROW_VB = 0
ROW_VA = ROW_VB + B_W
VT0_ROWS = ROW_VA + A_KV_W
DIL_HEADS_PER_STEP = 4


def _rms(x):
    return x * lax.rsqrt(jnp.mean(x * x, axis=-1, keepdims=True) + EPS)


def _dot(a, b):
    return jnp.dot(a, b, preferred_element_type=F32)


def _dot_nt(a, b):
    return lax.dot_general(a, b, (((1,), (1,)), ((), ())), preferred_element_type=F32)


def _dot_tn(a, b):
    return lax.dot_general(a, b, (((0,), (0,)), ((), ())), preferred_element_type=F32)


def _with_ones(vt):
    return jnp.concatenate([vt, jnp.ones_like(vt)], axis=0)


def _softmax_pv(s, vt_ones):
    dv = vt_ones.shape[0] // 2
    m = jnp.max(s, axis=-1, keepdims=True)
    p = jnp.exp2(s - m).astype(BF16)
    acc = _dot(vt_ones, p.T)
    return acc[:dv] / acc[dv:dv + 1]


def _pipelined(blocks, scores, finish):
    s_next = scores(blocks[0])
    for n, blk in enumerate(blocks):
        s_cur = s_next
        if n + 1 < len(blocks):
            s_next = scores(blocks[n + 1])
        finish(blk, s_cur)


def _proj0_kernel(x_ref, g_ref, w_ref, gq_ref, gk_ref, c_ref, sa_ref, sb_ref, qk_ref, vt_ref):
    tm = x_ref.shape[0]
    xn = (_rms(x_ref[...]) * g_ref[...]).astype(BF16)
    low = lax.broadcasted_iota(jnp.int32, (tm, LANES), 1) < HEAD_DIM
    cos, sin_a, sin_b = c_ref[...], sa_ref[...], sb_ref[...]

    def norm_rope(y, gain):
        sq = y * y
        s_all = jnp.sum(sq, axis=-1, keepdims=True)
        s_low = jnp.sum(jnp.where(low, sq, 0.0), axis=-1, keepdims=True)
        ms = jnp.where(low, s_low, s_all - s_low) * (1.0 / HEAD_DIM)
        y = y * lax.rsqrt(ms + EPS) * gain
        return (y * cos + pltpu.roll(y, LANES - HEAD_DIM // 2, 1) * sin_a
                + pltpu.roll(y, HEAD_DIM // 2, 1) * sin_b)

    scale = HEAD_DIM ** -0.5 * LOG2E
    y = _dot(xn, w_ref[:, COL_QA:COL_QB])
    for c0 in range(0, A_Q_W, LANES):
        q = norm_rope(y[:, COL_QA + c0:COL_QA + c0 + LANES], gq_ref[...]) * scale
        qk_ref[:, OUT_QA + c0:OUT_QA + c0 + LANES] = q.astype(BF16)
    qk_ref[:, OUT_KA:] = norm_rope(y[:, COL_KA:COL_VA], gk_ref[...]).astype(BF16)
    vt_ref[ROW_VA:, :] = y[:, COL_VA:COL_QB].T.astype(BF16)
    qk_ref[:, OUT_QB:OUT_KB] = (_dot(xn, w_ref[:, COL_QB:COL_KB]) * scale).astype(BF16)
    qk_ref[:, OUT_KB:OUT_KA] = _dot(xn, w_ref[:, COL_KB:COL_VB]).astype(BF16)
    vt_ref[ROW_VB:ROW_VA, :] = _dot(xn, w_ref[:, COL_VB:]).T.astype(BF16)


def _gqa_kernel(q_ref, k_ref, vt_ref, o_ref):
    rep = A_HEADS // A_KV_HEADS
    head = lambda i: slice(i * HEAD_DIM, (i + 1) * HEAD_DIM)
    keys = [k_ref[:, head(g)] for g in range(A_KV_HEADS)]
    vt_ones = [_with_ones(vt_ref[head(g), :]) for g in range(A_KV_HEADS)]

    tq = ATTN_Q_TILE
    tile = lambda i: slice(i * tq, (i + 1) * tq)

    def scores(block):
        i, h = block
        return _dot_nt(q_ref[tile(i), head(h)], keys[h // rep])

    def finish(block, s):
        i, h = block
        o_ref[head(h), tile(i)] = _softmax_pv(s, vt_ones[h // rep]).astype(o_ref.dtype)

    _pipelined([(i, h) for i in range(q_ref.shape[0] // tq) for h in range(A_HEADS)],
               scores, finish)


def _dilated_kernel(q_ref, k_ref, vt_ref, rb_ref, onehot_ref, logm_ref, o_ref, e_ref):
    s_len = q_ref.shape[0]
    blk = ATTN_Q_TILE
    n_blk = s_len // blk
    reach = -(-max((w // (2 * d)) * d for w, d in B_BRANCHES) // blk)
    heads = q_ref.shape[1] // HEAD_DIM
    e_cols = (2 * n_blk - 1) * blk

    @pl.when(pl.program_id(1) == 0)
    def _():
        for hh in range(heads):
            per_delta = jnp.sum(onehot_ref[...] * rb_ref[:, hh:hh + 1], axis=0, keepdims=True)
            per_delta = (per_delta + logm_ref[...]) * LOG2E
            spread = jnp.broadcast_to(per_delta, (blk, 2 * s_len))
            toeplitz = pltpu.roll(spread, 0, 1, stride=1, stride_axis=0)
            e_ref[hh] = toeplitz[:, blk:blk + e_cols]

    head = lambda hh: slice(hh * HEAD_DIM, (hh + 1) * HEAD_DIM)
    vt_ones = [_with_ones(vt_ref[head(hh), :]) for hh in range(heads)]

    def key_range(i):
        return max(0, i - reach) * blk, min(n_blk, i + reach + 1) * blk

    def scores(block):
        hh, i = block
        k_lo, k_hi = key_range(i)
        off = (n_blk - 1 - i) * blk
        s = _dot_nt(q_ref[i * blk:(i + 1) * blk, head(hh)], k_ref[k_lo:k_hi, head(hh)])
        return s + e_ref[hh, :, off + k_lo:off + k_hi]

    def finish(block, s):
        hh, i = block
        k_lo, k_hi = key_range(i)
        o = _softmax_pv(s, vt_ones[hh][:, k_lo:k_hi])
        o_ref[head(hh), i * blk:(i + 1) * blk] = o.astype(o_ref.dtype)

    _pipelined([(hh, i) for hh in range(heads) for i in range(n_blk)], scores, finish)


def _mlp_kernel(n_mix, final_norm, *refs):
    h_ref = refs[0]
    mix_refs = refs[1:1 + n_mix]
    wo_ref, g_ref, w1_ref, w2_ref = refs[1 + n_mix:5 + n_mix]
    gf_ref = refs[5 + n_mix] if final_norm else None
    out_ref = refs[-1]

    mix_t = jnp.concatenate([m_ref[...] for m_ref in mix_refs], axis=0)
    h = h_ref[...] + _dot_tn(mix_t, wo_ref[...])
    hn = (_rms(h) * g_ref[...]).astype(BF16)
    for c0 in range(0, D_FF, FF_CHUNK):
        a = jnp.maximum(_dot(hn, w1_ref[:, c0:c0 + FF_CHUNK]), 0.0)
        h = h + _dot((a * a).astype(BF16), w2_ref[c0:c0 + FF_CHUNK, :])
    if final_norm:
        h = _rms(h) * gf_ref[...]
    out_ref[...] = h


def _proj1_kernel(h_ref, g_ref, wd_ref, gq_ref, gkv_ref, wuq_ref, wuk_ref, wuvt_ref,
                  cq_ref, sq_ref, ck_ref, sk_ref, q_ref, k_ref, vt_ref):
    xn = (_rms(h_ref[...]) * g_ref[...]).astype(BF16)
    hd = _dot(xn, wd_ref[...])
    c_q = (_rms(hd[:, :C_Q_RANK]) * gq_ref[...]).astype(BF16)
    c_kv = (_rms(hd[:, C_Q_RANK:C_Q_RANK + C_KV_RANK]) * gkv_ref[...]).astype(BF16)

    def rope(y, cos, sin):
        return y * cos + pltpu.roll(y, LANES - C_ROPE_DIM, 1) * sin

    cq, sq = cq_ref[...], sq_ref[...]
    k_rope = rope(hd[:, C_Q_RANK + C_KV_RANK:], ck_ref[...], sk_ref[...])
    for c0 in range(0, C_HEADS * LANES, 2 * LANES):
        q2 = _dot(c_q, wuq_ref[:, c0:c0 + 2 * LANES])
        k2 = _dot(c_kv, wuk_ref[:, c0:c0 + 2 * LANES])
        for lo in (0, LANES):
            lanes = slice(c0 + lo, c0 + lo + LANES)
            q_ref[:, lanes] = rope(q2[:, lo:lo + LANES], cq, sq).astype(BF16)
            k_ref[:, lanes] = (k2[:, lo:lo + LANES] + k_rope).astype(BF16)
    vt_ref[...] = _dot_nt(wuvt_ref[...], c_kv).astype(BF16)


def _mla_kernel(q_ref, k_ref, vt_ref, o_ref):
    s_len = q_ref.shape[0]
    tq = ATTN_Q_TILE
    heads = MLA_HEADS_PER_STEP
    v_rows = lambda hh: slice(hh * C_V_DIM, (hh + 1) * C_V_DIM)
    vt_ones = [_with_ones(vt_ref[v_rows(hh), :]) for hh in range(heads)]

    def scores(block):
        hh, i = block
        lanes = slice(hh * LANES, (hh + 1) * LANES)
        return _dot_nt(q_ref[i * tq:(i + 1) * tq, lanes], k_ref[:, lanes])

    def finish(block, s):
        hh, i = block
        o_ref[v_rows(hh), i * tq:(i + 1) * tq] = _softmax_pv(s, vt_ones[hh]).astype(o_ref.dtype)

    _pipelined([(hh, i) for hh in range(heads) for i in range(s_len // tq)], scores, finish)


def _rope_angles(pos, dim):
    inv_freq = ROPE_THETA ** (-np.arange(0, dim, 2, dtype=np.float64) / dim)
    return pos.astype(np.float64)[:, None] * inv_freq[None, :]


def _axial_tables(s_len):
    t = np.arange(s_len)
    ang = np.concatenate([_rope_angles(t // GRID_W, HEAD_DIM // 2),
                          _rope_angles(t % GRID_W, HEAD_DIM // 2)], axis=-1)
    cos, sin, zero = np.cos(ang), np.sin(ang), np.zeros_like(ang)
    c = np.concatenate([cos, cos, cos, cos], axis=-1)
    sa = np.concatenate([-sin, zero, -sin, zero], axis=-1)
    sb = np.concatenate([zero, sin, zero, sin], axis=-1)
    return [jnp.asarray(a, F32) for a in (c, sa, sb)]


def _mla_tables(s_len, scale):
    ang = _rope_angles(np.arange(s_len), C_ROPE_DIM)
    cos, sin = np.cos(ang), np.sin(ang)
    ones = np.ones((s_len, C_NOPE_DIM))
    z_nope = np.zeros((s_len, C_NOPE_DIM))
    z_tail = np.zeros((s_len, LANES - C_QK_DIM))
    c = np.concatenate([ones, cos, cos, z_tail], axis=-1)
    sn = np.concatenate([z_nope, sin, sin, z_tail], axis=-1)
    return [jnp.asarray(a * scale, F32) for a in (c, sn)]


def _rotate_half_cols(w):
    half = w.shape[-1] // 2
    return jnp.concatenate([-w[..., half:], w[..., :half]], axis=-1)


def _t5_bucket(rel):
    nb = NUM_BUCKETS // 2
    max_exact = nb // 2
    base = np.where(rel > 0, nb, 0)
    n = np.abs(rel)
    nf = np.maximum(n, 1).astype(np.float32)
    large = max_exact + (np.log(nf / np.float32(max_exact)) / np.float32(math.log(REL_MAX_DISTANCE / max_exact))
                         * np.float32(nb - max_exact)).astype(np.int32)
    large = np.minimum(large, nb - 1)
    return base + np.where(n < max_exact, n, large)


def _dilated_structure(s_len):
    delta = np.arange(2 * s_len) - s_len
    mult = np.zeros(delta.shape, np.int32)
    for w, d in B_BRANCHES:
        n_side = w // (2 * d)
        mult += ((delta % d == 0) & (np.abs(delta) <= n_side * d)).astype(np.int32)
    log_mult = np.where(mult > 0, np.log(np.maximum(mult, 1).astype(np.float64)), NEG_INF)
    onehot = (_t5_bucket(delta)[None, :] == np.arange(NUM_BUCKETS)[:, None]) & (mult > 0)[None, :]
    return jnp.asarray(onehot, F32), jnp.asarray(log_mult[None, :], F32)


def _params(*sem):
    return pltpu.CompilerParams(dimension_semantics=sem, vmem_limit_bytes=VMEM_LIMIT_BYTES)


def _resident(shape):
    return pl.BlockSpec(shape, lambda *_: (0,) * len(shape), pipeline_mode=pl.Buffered(1))


def _row2(v):
    return v.reshape(1, -1).astype(F32)


def _token_specs(tm, s_len):
    seq_tiles = s_len // tm
    tok = lambda w: pl.BlockSpec((tm, w), lambda i: (i, 0))
    tok_t = lambda w: pl.BlockSpec((None, w, tm), lambda i: (i // seq_tiles, 0, i % seq_tiles))
    table = pl.BlockSpec((tm, LANES), lambda i: (i % seq_tiles, 0))
    return tok, tok_t, table


def _mlp_call(h, mixes_t, wo, g, w1_all, w2_all, layer, gf, name):
    n_tok = h.shape[0]
    tm = MLP_TOKEN_TILE
    tok, tok_t, _ = _token_specs(tm, mixes_t[0].shape[2])
    layer_slab = lambda w: pl.BlockSpec((None,) + w.shape[1:], lambda i: (layer, 0, 0),
                                        pipeline_mode=pl.Buffered(1))
    in_specs = [tok(D_MODEL)] + [tok_t(m.shape[1]) for m in mixes_t] + [
        _resident(wo.shape), _resident((1, D_MODEL)), layer_slab(w1_all), layer_slab(w2_all)]
    args = [h, *mixes_t, wo, _row2(g), w1_all, w2_all]
    if gf is not None:
        in_specs.append(_resident((1, D_MODEL)))
        args.append(_row2(gf))
    return pl.pallas_call(
        functools.partial(_mlp_kernel, len(mixes_t), gf is not None),
        out_shape=jax.ShapeDtypeStruct((n_tok, D_MODEL), F32),
        grid=(n_tok // tm,),
        in_specs=in_specs,
        out_specs=tok(D_MODEL),
        compiler_params=_params("parallel"),
        name=name,
    )(*args)


def kernel(x, norm_mix_g, norm_mlp_g, ab_w_in, a_q_norm_g, a_k_norm_g, ab_w_out, rel_bias,
           c_w_down, c_q_norm_g, c_kv_norm_g, c_w_uq, c_w_ukv, c_w_out, mlp_w1, mlp_w2,
           final_norm_g):
    b, s, d = x.shape
    n_tok = b * s
    h0 = x.reshape(n_tok, d)
    tm = PROJ0_TOKEN_TILE
    tok, tok_t, tab_spec = _token_specs(tm, s)

    tables0 = _axial_tables(s)
    gq = _row2(jnp.tile(a_q_norm_g[0], LANES // HEAD_DIM))
    gk = _row2(jnp.tile(a_k_norm_g[0], LANES // HEAD_DIM))
    w_in = ab_w_in[0].astype(BF16)
    qk0, vt0 = pl.pallas_call(
        _proj0_kernel,
        out_shape=(jax.ShapeDtypeStruct((n_tok, QK0_W), BF16),
                   jax.ShapeDtypeStruct((b, VT0_ROWS, s), BF16)),
        grid=(n_tok // tm,),
        in_specs=[tok(d), _resident((1, d)), _resident(w_in.shape),
                  _resident((1, LANES)), _resident((1, LANES)), tab_spec, tab_spec, tab_spec],
        out_specs=(tok(QK0_W), tok_t(VT0_ROWS)),
        compiler_params=_params("parallel"),
        name="proj0",
    )(h0, _row2(norm_mix_g[0]), w_in, gq, gk, *tables0)

    tq = ATTN_Q_TILE
    q_tiles = s // tq
    gq_rows = GQA_Q_ROWS
    gq_steps = s // gq_rows
    o_a = pl.pallas_call(
        _gqa_kernel,
        out_shape=jax.ShapeDtypeStruct((b, A_Q_W, s), BF16),
        grid=(b, gq_steps),
        in_specs=[pl.BlockSpec((gq_rows, A_Q_W),
                               lambda bi, qi: (bi * gq_steps + qi, OUT_QA // A_Q_W)),
                  pl.BlockSpec((s, A_KV_W), lambda bi, qi: (bi, OUT_KA // A_KV_W)),
                  pl.BlockSpec((None, A_KV_W, s), lambda bi, qi: (bi, ROW_VA // A_KV_W, 0))],
        out_specs=pl.BlockSpec((None, A_Q_W, gq_rows), lambda bi, qi: (bi, 0, qi)),
        compiler_params=_params("parallel", "parallel"),
        name="gqa",
    )(qk0, qk0, vt0)

    onehot, log_mult = _dilated_structure(s)
    dil_heads = DIL_HEADS_PER_STEP
    dil_w = dil_heads * HEAD_DIM
    groups = B_HEADS // dil_heads
    rb = rel_bias.astype(F32).reshape(NUM_BUCKETS, groups, dil_heads).transpose(1, 0, 2)
    o_b = pl.pallas_call(
        _dilated_kernel,
        out_shape=jax.ShapeDtypeStruct((b, B_W, s), BF16),
        grid=(groups, b),
        in_specs=[pl.BlockSpec((s, dil_w), lambda p, bi: (bi, OUT_QB // dil_w + p)),
                  pl.BlockSpec((s, dil_w), lambda p, bi: (bi, OUT_KB // dil_w + p)),
                  pl.BlockSpec((None, dil_w, s), lambda p, bi: (bi, ROW_VB // dil_w + p, 0)),
                  pl.BlockSpec((None, NUM_BUCKETS, dil_heads), lambda p, bi: (p, 0, 0)),
                  _resident(onehot.shape), _resident(log_mult.shape)],
        out_specs=pl.BlockSpec((None, dil_w, s), lambda p, bi: (bi, p, 0)),
        scratch_shapes=[pltpu.VMEM((dil_heads, tq, (2 * q_tiles - 1) * tq), F32)],
        compiler_params=_params("parallel", "arbitrary"),
        name="dilated",
    )(qk0, qk0, vt0, rb, onehot, log_mult)

    w1_all, w2_all = mlp_w1.astype(BF16), mlp_w2.astype(BF16)
    h1 = _mlp_call(h0, [o_a, o_b], ab_w_out[0].astype(BF16), norm_mlp_g[0],
                   w1_all, w2_all, 0, None, "mix0_mlp0")

    scale_c = C_QK_DIM ** -0.5 * LOG2E
    tables_q = _mla_tables(s, scale_c)
    tables_k = _mla_tables(s, 1.0)
    zeros = lambda r, c: jnp.zeros((r, c), F32)
    wd = c_w_down[0]
    kv_end = C_Q_RANK + C_KV_RANK
    wd_p = jnp.concatenate([wd[:, :kv_end], zeros(d, C_NOPE_DIM), wd[:, kv_end:],
                            _rotate_half_cols(wd[:, kv_end:])], axis=1).astype(BF16)
    wuq = c_w_uq[0].reshape(C_Q_RANK, C_HEADS, C_QK_DIM)
    wuq_p = jnp.concatenate([wuq, _rotate_half_cols(wuq[:, :, C_NOPE_DIM:])], axis=-1).reshape(
        C_Q_RANK, C_HEADS * LANES).astype(BF16)
    wukv = c_w_ukv[0].reshape(C_KV_RANK, C_HEADS, C_NOPE_DIM + C_V_DIM)
    wuk_p = jnp.pad(wukv[:, :, :C_NOPE_DIM], ((0, 0), (0, 0), (0, LANES - C_NOPE_DIM))).reshape(
        C_KV_RANK, C_HEADS * LANES).astype(BF16)
    wuv_t = wukv[:, :, C_NOPE_DIM:].reshape(C_KV_RANK, C_HEADS * C_V_DIM).T.astype(BF16)

    qk_w = C_HEADS * LANES
    v_w = C_HEADS * C_V_DIM
    tm = PROJ1_TOKEN_TILE
    tok, tok_t, tab_spec = _token_specs(tm, s)
    q_c, k_c, vt_c = pl.pallas_call(
        _proj1_kernel,
        out_shape=(jax.ShapeDtypeStruct((n_tok, qk_w), BF16),
                   jax.ShapeDtypeStruct((n_tok, qk_w), BF16),
                   jax.ShapeDtypeStruct((b, v_w, s), BF16)),
        grid=(n_tok // tm,),
        in_specs=[tok(d), _resident((1, d)), _resident(wd_p.shape),
                  _resident((1, C_Q_RANK)), _resident((1, C_KV_RANK)),
                  _resident(wuq_p.shape), _resident(wuk_p.shape), _resident(wuv_t.shape)]
                 + [tab_spec] * 4,
        out_specs=(tok(qk_w), tok(qk_w), tok_t(v_w)),
        compiler_params=_params("parallel"),
        name="proj1",
    )(h1, _row2(norm_mix_g[1]), wd_p, _row2(c_q_norm_g[0]), _row2(c_kv_norm_g[0]),
      wuq_p, wuk_p, wuv_t, *tables_q, *tables_k)

    o_c = pl.pallas_call(
        _mla_kernel,
        out_shape=jax.ShapeDtypeStruct((b, v_w, s), BF16),
        grid=(b, C_HEADS // MLA_HEADS_PER_STEP),
        in_specs=[pl.BlockSpec((s, MLA_HEADS_PER_STEP * LANES), lambda bi, p: (bi, p)),
                  pl.BlockSpec((s, MLA_HEADS_PER_STEP * LANES), lambda bi, p: (bi, p)),
                  pl.BlockSpec((None, MLA_HEADS_PER_STEP * C_V_DIM, s), lambda bi, p: (bi, p, 0))],
        out_specs=pl.BlockSpec((None, MLA_HEADS_PER_STEP * C_V_DIM, s), lambda bi, p: (bi, p, 0)),
        compiler_params=_params("parallel", "parallel"),
        name="mla",
    )(q_c, k_c, vt_c)

    out = _mlp_call(h1, [o_c], c_w_out[0].astype(BF16), norm_mlp_g[1],
                    w1_all, w2_all, 1, final_norm_g, "mix1_mlp1")
    return out.reshape(b, s, d)
```

```python
import functools
import math

import jax
import jax.numpy as jnp
import numpy as np
from jax import lax
from jax.experimental import pallas as pl
from jax.experimental.pallas import tpu as pltpu

F32 = jnp.float32
BF16 = jnp.bfloat16

D_MODEL = 1024
GRID_W = 64
HEAD_DIM = 64
ROPE_THETA = 10000.0
EPS = 1e-6
NEG_INF = -1e30
LOG2E = math.log2(math.e)

A_HEADS = 8
A_KV_HEADS = 2
B_HEADS = 8
B_BRANCHES = ((128, 1), (512, 4), (2048, 16))
NUM_BUCKETS = 32
REL_MAX_DISTANCE = 1024

C_HEADS = 16
C_Q_RANK = 256
C_KV_RANK = 128
C_NOPE_DIM = 64
C_ROPE_DIM = 32
C_V_DIM = 64
C_QK_DIM = C_NOPE_DIM + C_ROPE_DIM
D_FF = 4 * D_MODEL

A_Q_W = A_HEADS * HEAD_DIM
A_KV_W = A_KV_HEADS * HEAD_DIM
B_W = B_HEADS * HEAD_DIM

LANES = 128
VMEM_LIMIT_BYTES = 56 * 1024 * 1024

PROJ0_TOKEN_TILE = 512
PROJ1_TOKEN_TILE = 1024
MLP_TOKEN_TILE = 1024
ATTN_Q_TILE = 256
GQA_Q_ROWS = 512
MLA_HEADS_PER_STEP = 4
FF_CHUNK = 1024
WEIGHT_STAGE_CHUNK = (512, 1024)

COL_QA = 0
COL_KA = COL_QA + A_Q_W
COL_VA = COL_KA + A_KV_W
COL_QB = COL_VA + A_KV_W
COL_KB = COL_QB + B_W
COL_VB = COL_KB + B_W
OUT_QA = 0
OUT_QB = OUT_QA + A_Q_W
OUT_KB = OUT_QB + B_W
OUT_KA = OUT_KB + B_W
QK0_W = OUT_KA + A_KV_W
ROW_VB = 0
ROW_VA = ROW_VB + B_W
VT0_ROWS = ROW_VA + A_KV_W
DIL_HEADS_PER_STEP = 4


def _rms(x):
    return x * lax.rsqrt(jnp.mean(x * x, axis=-1, keepdims=True) + EPS)


def _dot(a, b):
    return jnp.dot(a, b, preferred_element_type=F32)


def _dot_nt(a, b):
    return lax.dot_general(a, b, (((1,), (1,)), ((), ())), preferred_element_type=F32)


def _dot_tn(a, b):
    return lax.dot_general(a, b, (((0,), (0,)), ((), ())), preferred_element_type=F32)


def _with_ones(vt):
    return jnp.concatenate([vt, jnp.ones_like(vt)], axis=0)


def _softmax_pv(s, vt_ones):
    dv = vt_ones.shape[0] // 2
    m = jnp.max(s, axis=-1, keepdims=True)
    p = jnp.exp2(s - m).astype(BF16)
    acc = _dot(vt_ones, p.T)
    return acc[:dv] / acc[dv:dv + 1]


def _pipelined(blocks, scores, finish):
    s_next = scores(blocks[0])
    for n, blk in enumerate(blocks):
        s_cur = s_next
        if n + 1 < len(blocks):
            s_next = scores(blocks[n + 1])
        finish(blk, s_cur)


def _proj0_kernel(x_ref, g_ref, w_ref, gq_ref, gk_ref, c_ref, sa_ref, sb_ref, qk_ref, vt_ref):
    tm = x_ref.shape[0]
    xn = (_rms(x_ref[...]) * g_ref[...]).astype(BF16)
    low = lax.broadcasted_iota(jnp.int32, (tm, LANES), 1) < HEAD_DIM
    cos, sin_a, sin_b = c_ref[...], sa_ref[...], sb_ref[...]

    def norm_rope(y, gain):
        sq = y * y
        s_all = jnp.sum(sq, axis=-1, keepdims=True)
        s_low = jnp.sum(jnp.where(low, sq, 0.0), axis=-1, keepdims=True)
        ms = jnp.where(low, s_low, s_all - s_low) * (1.0 / HEAD_DIM)
        y = y * lax.rsqrt(ms + EPS) * gain
        return (y * cos + pltpu.roll(y, LANES - HEAD_DIM // 2, 1) * sin_a
                + pltpu.roll(y, HEAD_DIM // 2, 1) * sin_b)

    scale = HEAD_DIM ** -0.5 * LOG2E
    y = _dot(xn, w_ref[:, COL_QA:COL_QB])
    for c0 in range(0, A_Q_W, LANES):
        q = norm_rope(y[:, COL_QA + c0:COL_QA + c0 + LANES], gq_ref[...]) * scale
        qk_ref[:, OUT_QA + c0:OUT_QA + c0 + LANES] = q.astype(BF16)
    qk_ref[:, OUT_KA:] = norm_rope(y[:, COL_KA:COL_VA], gk_ref[...]).astype(BF16)
    vt_ref[ROW_VA:, :] = y[:, COL_VA:COL_QB].T.astype(BF16)
    qk_ref[:, OUT_QB:OUT_KB] = (_dot(xn, w_ref[:, COL_QB:COL_KB]) * scale).astype(BF16)
    qk_ref[:, OUT_KB:OUT_KA] = _dot(xn, w_ref[:, COL_KB:COL_VB]).astype(BF16)
    vt_ref[ROW_VB:ROW_VA, :] = _dot(xn, w_ref[:, COL_VB:]).T.astype(BF16)


def _gqa_kernel(q_ref, k_ref, vt_ref, o_ref):
    rep = A_HEADS // A_KV_HEADS
    head = lambda i: slice(i * HEAD_DIM, (i + 1) * HEAD_DIM)
    keys = [k_ref[:, head(g)] for g in range(A_KV_HEADS)]
    vt_ones = [_with_ones(vt_ref[head(g), :]) for g in range(A_KV_HEADS)]

    tq = ATTN_Q_TILE
    tile = lambda i: slice(i * tq, (i + 1) * tq)

    def scores(block):
        i, h = block
        return _dot_nt(q_ref[tile(i), head(h)], keys[h // rep])

    def finish(block, s):
        i, h = block
        o_ref[head(h), tile(i)] = _softmax_pv(s, vt_ones[h // rep]).astype(o_ref.dtype)

    _pipelined([(i, h) for i in range(q_ref.shape[0] // tq) for h in range(A_HEADS)],
               scores, finish)


def _dilated_kernel(q_ref, k_ref, vt_ref, rb_ref, onehot_ref, logm_ref, o_ref, e_ref):
    s_len = q_ref.shape[0]
    blk = ATTN_Q_TILE
    n_blk = s_len // blk
    reach = -(-max((w // (2 * d)) * d for w, d in B_BRANCHES) // blk)
    heads = q_ref.shape[1] // HEAD_DIM
    e_cols = (2 * n_blk - 1) * blk

    @pl.when(pl.program_id(1) == 0)
    def _():
        for hh in range(heads):
            per_delta = jnp.sum(onehot_ref[...] * rb_ref[:, hh:hh + 1], axis=0, keepdims=True)
            per_delta = (per_delta + logm_ref[...]) * LOG2E
            spread = jnp.broadcast_to(per_delta, (blk, 2 * s_len))
            toeplitz = pltpu.roll(spread, 0, 1, stride=1, stride_axis=0)
            e_ref[hh] = toeplitz[:, blk:blk + e_cols]

    head = lambda hh: slice(hh * HEAD_DIM, (hh + 1) * HEAD_DIM)
    vt_ones = [_with_ones(vt_ref[head(hh), :]) for hh in range(heads)]

    def key_range(i):
        return max(0, i - reach) * blk, min(n_blk, i + reach + 1) * blk

    def scores(block):
        hh, i = block
        k_lo, k_hi = key_range(i)
        off = (n_blk - 1 - i) * blk
        s = _dot_nt(q_ref[i * blk:(i + 1) * blk, head(hh)], k_ref[k_lo:k_hi, head(hh)])
        return s + e_ref[hh, :, off + k_lo:off + k_hi]

    def finish(block, s):
        hh, i = block
        k_lo, k_hi = key_range(i)
        o = _softmax_pv(s, vt_ones[hh][:, k_lo:k_hi])
        o_ref[head(hh), i * blk:(i + 1) * blk] = o.astype(o_ref.dtype)

    _pipelined([(hh, i) for hh in range(heads) for i in range(n_blk)], scores, finish)


def _stage_weights(chunks, stage_ref, sem_ref):
    def copy(c):
        return pltpu.make_async_copy(chunks[c][0], stage_ref.at[c % 2], sem_ref.at[c % 2])

    copy(0).start()
    for c, (_, dst) in enumerate(chunks):
        if c + 1 < len(chunks):
            copy(c + 1).start()
        copy(c).wait()
        dst[...] = stage_ref[c % 2].astype(BF16)


def _mlp_kernel(n_mix, final_norm, layer, *refs):
    h_ref = refs[0]
    mix_refs = refs[1:1 + n_mix]
    wo_hbm, g_ref, w1_hbm, w2_hbm = refs[1 + n_mix:5 + n_mix]
    gf_ref = refs[5 + n_mix] if final_norm else None
    out_ref, wo_ref, w1_ref, w2_ref, stage_ref, sem_ref = refs[-6:]

    @pl.when(pl.program_id(0) == 0)
    def _():
        rows, cols = stage_ref.shape[1:]
        chunks = []
        wo_layer, mlp_layer = layer
        for hbm, idx, vmem in ((wo_hbm, wo_layer, wo_ref), (w1_hbm, mlp_layer, w1_ref),
                               (w2_hbm, mlp_layer, w2_ref)):
            for r0 in range(0, vmem.shape[0], rows):
                for c0 in range(0, vmem.shape[1], cols):
                    chunks.append((hbm.at[idx, r0:r0 + rows, c0:c0 + cols],
                                   vmem.at[r0:r0 + rows, c0:c0 + cols]))
        _stage_weights(chunks, stage_ref, sem_ref)

    mix_t = jnp.concatenate([m_ref[...] for m_ref in mix_refs], axis=0)
    h = h_ref[...] + _dot_tn(mix_t, wo_ref[...])
    hn = (_rms(h) * g_ref[...]).astype(BF16)
    for c0 in range(0, D_FF, FF_CHUNK):
        a = jnp.maximum(_dot(hn, w1_ref[:, c0:c0 + FF_CHUNK]), 0.0)
        h = h + _dot((a * a).astype(BF16), w2_ref[c0:c0 + FF_CHUNK, :])
    if final_norm:
        h = _rms(h) * gf_ref[...]
    out_ref[...] = h


def _proj1_kernel(h_ref, g_ref, wd_ref, gq_ref, gkv_ref, wuq_ref, wuk_ref, wuvt_ref,
                  cq_ref, sq_ref, ck_ref, sk_ref, q_ref, k_ref, vt_ref):
    xn = (_rms(h_ref[...]) * g_ref[...]).astype(BF16)
    hd = _dot(xn, wd_ref[...])
    c_q = (_rms(hd[:, :C_Q_RANK]) * gq_ref[...]).astype(BF16)
    c_kv = (_rms(hd[:, C_Q_RANK:C_Q_RANK + C_KV_RANK]) * gkv_ref[...]).astype(BF16)

    def rope(y, cos, sin):
        return y * cos + pltpu.roll(y, LANES - C_ROPE_DIM, 1) * sin

    cq, sq = cq_ref[...], sq_ref[...]
    k_rope = rope(hd[:, C_Q_RANK + C_KV_RANK:], ck_ref[...], sk_ref[...])
    for c0 in range(0, C_HEADS * LANES, 2 * LANES):
        q2 = _dot(c_q, wuq_ref[:, c0:c0 + 2 * LANES])
        k2 = _dot(c_kv, wuk_ref[:, c0:c0 + 2 * LANES])
        for lo in (0, LANES):
            lanes = slice(c0 + lo, c0 + lo + LANES)
            q_ref[:, lanes] = rope(q2[:, lo:lo + LANES], cq, sq).astype(BF16)
            k_ref[:, lanes] = (k2[:, lo:lo + LANES] + k_rope).astype(BF16)
    vt_ref[...] = _dot_nt(wuvt_ref[...], c_kv).astype(BF16)


def _mla_kernel(q_ref, k_ref, vt_ref, o_ref):
    s_len = q_ref.shape[0]
    tq = ATTN_Q_TILE
    heads = MLA_HEADS_PER_STEP
    v_rows = lambda hh: slice(hh * C_V_DIM, (hh + 1) * C_V_DIM)
    vt_ones = [_with_ones(vt_ref[v_rows(hh), :]) for hh in range(heads)]

    def scores(block):
        hh, i = block
        lanes = slice(hh * LANES, (hh + 1) * LANES)
        return _dot_nt(q_ref[i * tq:(i + 1) * tq, lanes], k_ref[:, lanes])

    def finish(block, s):
        hh, i = block
        o_ref[v_rows(hh), i * tq:(i + 1) * tq] = _softmax_pv(s, vt_ones[hh]).astype(o_ref.dtype)

    _pipelined([(hh, i) for hh in range(heads) for i in range(s_len // tq)], scores, finish)


def _rope_angles(pos, dim):
    inv_freq = ROPE_THETA ** (-np.arange(0, dim, 2, dtype=np.float64) / dim)
    return pos.astype(np.float64)[:, None] * inv_freq[None, :]


def _axial_tables(s_len):
    t = np.arange(s_len)
    ang = np.concatenate([_rope_angles(t // GRID_W, HEAD_DIM // 2),
                          _rope_angles(t % GRID_W, HEAD_DIM // 2)], axis=-1)
    cos, sin, zero = np.cos(ang), np.sin(ang), np.zeros_like(ang)
    c = np.concatenate([cos, cos, cos, cos], axis=-1)
    sa = np.concatenate([-sin, zero, -sin, zero], axis=-1)
    sb = np.concatenate([zero, sin, zero, sin], axis=-1)
    return [jnp.asarray(a, F32) for a in (c, sa, sb)]


def _mla_tables(s_len, scale):
    ang = _rope_angles(np.arange(s_len), C_ROPE_DIM)
    cos, sin = np.cos(ang), np.sin(ang)
    ones = np.ones((s_len, C_NOPE_DIM))
    z_nope = np.zeros((s_len, C_NOPE_DIM))
    z_tail = np.zeros((s_len, LANES - C_QK_DIM))
    c = np.concatenate([ones, cos, cos, z_tail], axis=-1)
    sn = np.concatenate([z_nope, sin, sin, z_tail], axis=-1)
    return [jnp.asarray(a * scale, F32) for a in (c, sn)]


def _rotate_half_cols(w):
    half = w.shape[-1] // 2
    return jnp.concatenate([-w[..., half:], w[..., :half]], axis=-1)


def _t5_bucket(rel):
    nb = NUM_BUCKETS // 2
    max_exact = nb // 2
    base = np.where(rel > 0, nb, 0)
    n = np.abs(rel)
    nf = np.maximum(n, 1).astype(np.float32)
    large = max_exact + (np.log(nf / np.float32(max_exact)) / np.float32(math.log(REL_MAX_DISTANCE / max_exact))
                         * np.float32(nb - max_exact)).astype(np.int32)
    large = np.minimum(large, nb - 1)
    return base + np.where(n < max_exact, n, large)


def _dilated_structure(s_len):
    delta = np.arange(2 * s_len) - s_len
    mult = np.zeros(delta.shape, np.int32)
    for w, d in B_BRANCHES:
        n_side = w // (2 * d)
        mult += ((delta % d == 0) & (np.abs(delta) <= n_side * d)).astype(np.int32)
    log_mult = np.where(mult > 0, np.log(np.maximum(mult, 1).astype(np.float64)), NEG_INF)
    onehot = (_t5_bucket(delta)[None, :] == np.arange(NUM_BUCKETS)[:, None]) & (mult > 0)[None, :]
    return jnp.asarray(onehot, F32), jnp.asarray(log_mult[None, :], F32)


def _params(*sem):
    return pltpu.CompilerParams(dimension_semantics=sem, vmem_limit_bytes=VMEM_LIMIT_BYTES)


def _resident(shape):
    return pl.BlockSpec(shape, lambda *_: (0,) * len(shape), pipeline_mode=pl.Buffered(1))


def _row2(v):
    return v.reshape(1, -1).astype(F32)


def _token_specs(tm, s_len):
    seq_tiles = s_len // tm
    tok = lambda w: pl.BlockSpec((tm, w), lambda i: (i, 0))
    tok_t = lambda w: pl.BlockSpec((None, w, tm), lambda i: (i // seq_tiles, 0, i % seq_tiles))
    table = pl.BlockSpec((tm, LANES), lambda i: (i % seq_tiles, 0))
    return tok, tok_t, table


def _mlp_call(h, mixes_t, wo_all, g, w1_all, w2_all, layer, gf, name):
    n_tok = h.shape[0]
    tm = MLP_TOKEN_TILE
    tok, tok_t, _ = _token_specs(tm, mixes_t[0].shape[2])
    in_hbm = pl.BlockSpec(memory_space=pl.ANY)
    in_specs = [tok(D_MODEL)] + [tok_t(m.shape[1]) for m in mixes_t] + [
        in_hbm, _resident((1, D_MODEL)), in_hbm, in_hbm]
    args = [h, *mixes_t, wo_all, _row2(g), w1_all, w2_all]
    if gf is not None:
        in_specs.append(_resident((1, D_MODEL)))
        args.append(_row2(gf))
    return pl.pallas_call(
        functools.partial(_mlp_kernel, len(mixes_t), gf is not None, layer),
        out_shape=jax.ShapeDtypeStruct((n_tok, D_MODEL), F32),
        grid=(n_tok // tm,),
        in_specs=in_specs,
        out_specs=tok(D_MODEL),
        scratch_shapes=[pltpu.VMEM(wo_all.shape[1:], BF16), pltpu.VMEM(w1_all.shape[1:], BF16),
                        pltpu.VMEM(w2_all.shape[1:], BF16),
                        pltpu.VMEM((2,) + WEIGHT_STAGE_CHUNK, F32),
                        pltpu.SemaphoreType.DMA((2,))],
        compiler_params=_params("arbitrary"),
        name=name,
    )(*args)


def kernel(x, norm_mix_g, norm_mlp_g, ab_w_in, a_q_norm_g, a_k_norm_g, ab_w_out, rel_bias,
           c_w_down, c_q_norm_g, c_kv_norm_g, c_w_uq, c_w_ukv, c_w_out, mlp_w1, mlp_w2,
           final_norm_g):
    b, s, d = x.shape
    n_tok = b * s
    h0 = x.reshape(n_tok, d)
    tm = PROJ0_TOKEN_TILE
    tok, tok_t, tab_spec = _token_specs(tm, s)

    tables0 = _axial_tables(s)
    gq = _row2(jnp.tile(a_q_norm_g[0], LANES // HEAD_DIM))
    gk = _row2(jnp.tile(a_k_norm_g[0], LANES // HEAD_DIM))
    w_in = ab_w_in[0].astype(BF16)
    qk0, vt0 = pl.pallas_call(
        _proj0_kernel,
        out_shape=(jax.ShapeDtypeStruct((n_tok, QK0_W), BF16),
                   jax.ShapeDtypeStruct((b, VT0_ROWS, s), BF16)),
        grid=(n_tok // tm,),
        in_specs=[tok(d), _resident((1, d)), _resident(w_in.shape),
                  _resident((1, LANES)), _resident((1, LANES)), tab_spec, tab_spec, tab_spec],
        out_specs=(tok(QK0_W), tok_t(VT0_ROWS)),
        compiler_params=_params("parallel"),
        name="proj0",
    )(h0, _row2(norm_mix_g[0]), w_in, gq, gk, *tables0)

    tq = ATTN_Q_TILE
    q_tiles = s // tq
    gq_rows = GQA_Q_ROWS
    gq_steps = s // gq_rows
    o_a = pl.pallas_call(
        _gqa_kernel,
        out_shape=jax.ShapeDtypeStruct((b, A_Q_W, s), BF16),
        grid=(b, gq_steps),
        in_specs=[pl.BlockSpec((gq_rows, A_Q_W),
                               lambda bi, qi: (bi * gq_steps + qi, OUT_QA // A_Q_W)),
                  pl.BlockSpec((s, A_KV_W), lambda bi, qi: (bi, OUT_KA // A_KV_W)),
                  pl.BlockSpec((None, A_KV_W, s), lambda bi, qi: (bi, ROW_VA // A_KV_W, 0))],
        out_specs=pl.BlockSpec((None, A_Q_W, gq_rows), lambda bi, qi: (bi, 0, qi)),
        compiler_params=_params("parallel", "parallel"),
        name="gqa",
    )(qk0, qk0, vt0)

    onehot, log_mult = _dilated_structure(s)
    dil_heads = DIL_HEADS_PER_STEP
    dil_w = dil_heads * HEAD_DIM
    groups = B_HEADS // dil_heads
    rb = rel_bias.astype(F32).reshape(NUM_BUCKETS, groups, dil_heads).transpose(1, 0, 2)
    o_b = pl.pallas_call(
        _dilated_kernel,
        out_shape=jax.ShapeDtypeStruct((b, B_W, s), BF16),
        grid=(groups, b),
        in_specs=[pl.BlockSpec((s, dil_w), lambda p, bi: (bi, OUT_QB // dil_w + p)),
                  pl.BlockSpec((s, dil_w), lambda p, bi: (bi, OUT_KB // dil_w + p)),
                  pl.BlockSpec((None, dil_w, s), lambda p, bi: (bi, ROW_VB // dil_w + p, 0)),
                  pl.BlockSpec((None, NUM_BUCKETS, dil_heads), lambda p, bi: (p, 0, 0)),
                  _resident(onehot.shape), _resident(log_mult.shape)],
        out_specs=pl.BlockSpec((None, dil_w, s), lambda p, bi: (bi, p, 0)),
        scratch_shapes=[pltpu.VMEM((dil_heads, tq, (2 * q_tiles - 1) * tq), F32)],
        compiler_params=_params("parallel", "arbitrary"),
        name="dilated",
    )(qk0, qk0, vt0, rb, onehot, log_mult)

    h1 = _mlp_call(h0, [o_a, o_b], ab_w_out, norm_mlp_g[0], mlp_w1, mlp_w2, (0, 0), None,
                   "mix0_mlp0")

    scale_c = C_QK_DIM ** -0.5 * LOG2E
    tables_q = _mla_tables(s, scale_c)
    tables_k = _mla_tables(s, 1.0)
    zeros = lambda r, c: jnp.zeros((r, c), F32)
    wd = c_w_down[0]
    kv_end = C_Q_RANK + C_KV_RANK
    wd_p = jnp.concatenate([wd[:, :kv_end], zeros(d, C_NOPE_DIM), wd[:, kv_end:],
                            _rotate_half_cols(wd[:, kv_end:])], axis=1).astype(BF16)
    wuq = c_w_uq[0].reshape(C_Q_RANK, C_HEADS, C_QK_DIM)
    wuq_p = jnp.concatenate([wuq, _rotate_half_cols(wuq[:, :, C_NOPE_DIM:])], axis=-1).reshape(
        C_Q_RANK, C_HEADS * LANES).astype(BF16)
    wukv = c_w_ukv[0].reshape(C_KV_RANK, C_HEADS, C_NOPE_DIM + C_V_DIM)
    wuk_p = jnp.pad(wukv[:, :, :C_NOPE_DIM], ((0, 0), (0, 0), (0, LANES - C_NOPE_DIM))).reshape(
        C_KV_RANK, C_HEADS * LANES).astype(BF16)
    wuv_t = wukv[:, :, C_NOPE_DIM:].reshape(C_KV_RANK, C_HEADS * C_V_DIM).T.astype(BF16)

    qk_w = C_HEADS * LANES
    v_w = C_HEADS * C_V_DIM
    tm = PROJ1_TOKEN_TILE
    tok, tok_t, tab_spec = _token_specs(tm, s)
    q_c, k_c, vt_c = pl.pallas_call(
        _proj1_kernel,
        out_shape=(jax.ShapeDtypeStruct((n_tok, qk_w), BF16),
                   jax.ShapeDtypeStruct((n_tok, qk_w), BF16),
                   jax.ShapeDtypeStruct((b, v_w, s), BF16)),
        grid=(n_tok // tm,),
        in_specs=[tok(d), _resident((1, d)), _resident(wd_p.shape),
                  _resident((1, C_Q_RANK)), _resident((1, C_KV_RANK)),
                  _resident(wuq_p.shape), _resident(wuk_p.shape), _resident(wuv_t.shape)]
                 + [tab_spec] * 4,
        out_specs=(tok(qk_w), tok(qk_w), tok_t(v_w)),
        compiler_params=_params("parallel"),
        name="proj1",
    )(h1, _row2(norm_mix_g[1]), wd_p, _row2(c_q_norm_g[0]), _row2(c_kv_norm_g[0]),
      wuq_p, wuk_p, wuv_t, *tables_q, *tables_k)

    o_c = pl.pallas_call(
        _mla_kernel,
        out_shape=jax.ShapeDtypeStruct((b, v_w, s), BF16),
        grid=(b, C_HEADS // MLA_HEADS_PER_STEP),
        in_specs=[pl.BlockSpec((s, MLA_HEADS_PER_STEP * LANES), lambda bi, p: (bi, p)),
                  pl.BlockSpec((s, MLA_HEADS_PER_STEP * LANES), lambda bi, p: (bi, p)),
                  pl.BlockSpec((None, MLA_HEADS_PER_STEP * C_V_DIM, s), lambda bi, p: (bi, p, 0))],
        out_specs=pl.BlockSpec((None, MLA_HEADS_PER_STEP * C_V_DIM, s), lambda bi, p: (bi, p, 0)),
        compiler_params=_params("parallel", "parallel"),
        name="mla",
    )(q_c, k_c, vt_c)

    out = _mlp_call(h1, [o_c], c_w_out, norm_mlp_g[1], mlp_w1, mlp_w2, (0, 1), final_norm_g,
                    "mix1_mlp1")
    return out.reshape(b, s, d)
```

```python
import functools
import math

import jax
import jax.numpy as jnp
import numpy as np
from jax import lax
from jax.experimental import pallas as pl
from jax.experimental.pallas import tpu as pltpu

F32 = jnp.float32
BF16 = jnp.bfloat16

D_MODEL = 1024
GRID_W = 64
HEAD_DIM = 64
ROPE_THETA = 10000.0
EPS = 1e-6
NEG_INF = -1e30
LOG2E = math.log2(math.e)

A_HEADS = 8
A_KV_HEADS = 2
B_HEADS = 8
B_BRANCHES = ((128, 1), (512, 4), (2048, 16))
NUM_BUCKETS = 32
REL_MAX_DISTANCE = 1024

C_HEADS = 16
C_Q_RANK = 256
C_KV_RANK = 128
C_NOPE_DIM = 64
C_ROPE_DIM = 32
C_V_DIM = 64
C_QK_DIM = C_NOPE_DIM + C_ROPE_DIM
D_FF = 4 * D_MODEL

A_Q_W = A_HEADS * HEAD_DIM
A_KV_W = A_KV_HEADS * HEAD_DIM
B_W = B_HEADS * HEAD_DIM

LANES = 128
VMEM_LIMIT_BYTES = 56 * 1024 * 1024

PROJ0_TOKEN_TILE = 512
PROJ1_TOKEN_TILE = 1024
MLP_TOKEN_TILE = 1024
ATTN_Q_TILE = 256
GQA_Q_ROWS = 512
MLA_HEADS_PER_STEP = 4
FF_CHUNK = 1024
WEIGHT_STAGE_CHUNK = (256, 1024)
WEIGHT_STAGE_SLOTS = 4

COL_QA = 0
COL_KA = COL_QA + A_Q_W
COL_VA = COL_KA + A_KV_W
COL_QB = COL_VA + A_KV_W
COL_KB = COL_QB + B_W
COL_VB = COL_KB + B_W
OUT_QA = 0
OUT_QB = OUT_QA + A_Q_W
OUT_KB = OUT_QB + B_W
OUT_KA = OUT_KB + B_W
QK0_W = OUT_KA + A_KV_W
ROW_VB = 0
ROW_VA = ROW_VB + B_W
VT0_ROWS = ROW_VA + A_KV_W
DIL_HEADS_PER_STEP = 4


def _rms(x):
    return x * lax.rsqrt(jnp.mean(x * x, axis=-1, keepdims=True) + EPS)


def _dot(a, b):
    return jnp.dot(a, b, preferred_element_type=F32)


def _dot_nt(a, b):
    return lax.dot_general(a, b, (((1,), (1,)), ((), ())), preferred_element_type=F32)


def _dot_tn(a, b):
    return lax.dot_general(a, b, (((0,), (0,)), ((), ())), preferred_element_type=F32)


def _with_ones(vt):
    return jnp.concatenate([vt, jnp.ones_like(vt)], axis=0)


def _softmax_pv(s, vt_ones):
    dv = vt_ones.shape[0] // 2
    m = jnp.max(s, axis=-1, keepdims=True)
    p = jnp.exp2(s - m).astype(BF16)
    acc = _dot(vt_ones, p.T)
    return acc[:dv] / acc[dv:dv + 1]


def _pipelined(blocks, scores, finish):
    s_next = scores(blocks[0])
    for n, blk in enumerate(blocks):
        s_cur = s_next
        if n + 1 < len(blocks):
            s_next = scores(blocks[n + 1])
        finish(blk, s_cur)


def _proj0_kernel(x_ref, g_ref, w_ref, gq_ref, gk_ref, c_ref, sa_ref, sb_ref, qk_ref, vt_ref):
    tm = x_ref.shape[0]
    xn = (_rms(x_ref[...]) * g_ref[...]).astype(BF16)
    low = lax.broadcasted_iota(jnp.int32, (tm, LANES), 1) < HEAD_DIM
    cos, sin_a, sin_b = c_ref[...], sa_ref[...], sb_ref[...]

    def norm_rope(y, gain):
        sq = y * y
        s_all = jnp.sum(sq, axis=-1, keepdims=True)
        s_low = jnp.sum(jnp.where(low, sq, 0.0), axis=-1, keepdims=True)
        ms = jnp.where(low, s_low, s_all - s_low) * (1.0 / HEAD_DIM)
        y = y * lax.rsqrt(ms + EPS) * gain
        return (y * cos + pltpu.roll(y, LANES - HEAD_DIM // 2, 1) * sin_a
                + pltpu.roll(y, HEAD_DIM // 2, 1) * sin_b)

    scale = HEAD_DIM ** -0.5 * LOG2E
    y = _dot(xn, w_ref[:, COL_QA:COL_QB])
    for c0 in range(0, A_Q_W, LANES):
        q = norm_rope(y[:, COL_QA + c0:COL_QA + c0 + LANES], gq_ref[...]) * scale
        qk_ref[:, OUT_QA + c0:OUT_QA + c0 + LANES] = q.astype(BF16)
    qk_ref[:, OUT_KA:] = norm_rope(y[:, COL_KA:COL_VA], gk_ref[...]).astype(BF16)
    vt_ref[ROW_VA:, :] = y[:, COL_VA:COL_QB].T.astype(BF16)
    qk_ref[:, OUT_QB:OUT_KB] = (_dot(xn, w_ref[:, COL_QB:COL_KB]) * scale).astype(BF16)
    qk_ref[:, OUT_KB:OUT_KA] = _dot(xn, w_ref[:, COL_KB:COL_VB]).astype(BF16)
    vt_ref[ROW_VB:ROW_VA, :] = _dot(xn, w_ref[:, COL_VB:]).T.astype(BF16)


def _gqa_kernel(q_ref, k_ref, vt_ref, o_ref):
    rep = A_HEADS // A_KV_HEADS
    head = lambda i: slice(i * HEAD_DIM, (i + 1) * HEAD_DIM)
    keys = [k_ref[:, head(g)] for g in range(A_KV_HEADS)]
    vt_ones = [_with_ones(vt_ref[head(g), :]) for g in range(A_KV_HEADS)]

    tq = ATTN_Q_TILE
    tile = lambda i: slice(i * tq, (i + 1) * tq)

    def scores(block):
        i, h = block
        return _dot_nt(q_ref[tile(i), head(h)], keys[h // rep])

    def finish(block, s):
        i, h = block
        o_ref[head(h), tile(i)] = _softmax_pv(s, vt_ones[h // rep]).astype(o_ref.dtype)

    _pipelined([(i, h) for i in range(q_ref.shape[0] // tq) for h in range(A_HEADS)],
               scores, finish)


def _dilated_kernel(q_ref, k_ref, vt_ref, rb_ref, onehot_ref, logm_ref, o_ref, e_ref):
    s_len = q_ref.shape[0]
    blk = ATTN_Q_TILE
    n_blk = s_len // blk
    reach = -(-max((w // (2 * d)) * d for w, d in B_BRANCHES) // blk)
    heads = q_ref.shape[1] // HEAD_DIM
    e_cols = (2 * n_blk - 1) * blk

    @pl.when(pl.program_id(1) == 0)
    def _():
        for hh in range(heads):
            per_delta = jnp.sum(onehot_ref[...] * rb_ref[:, hh:hh + 1], axis=0, keepdims=True)
            per_delta = (per_delta + logm_ref[...]) * LOG2E
            spread = jnp.broadcast_to(per_delta, (blk, 2 * s_len))
            toeplitz = pltpu.roll(spread, 0, 1, stride=1, stride_axis=0)
            e_ref[hh] = toeplitz[:, blk:blk + e_cols]

    head = lambda hh: slice(hh * HEAD_DIM, (hh + 1) * HEAD_DIM)
    vt_ones = [_with_ones(vt_ref[head(hh), :]) for hh in range(heads)]

    def key_range(i):
        return max(0, i - reach) * blk, min(n_blk, i + reach + 1) * blk

    def scores(block):
        hh, i = block
        k_lo, k_hi = key_range(i)
        off = (n_blk - 1 - i) * blk
        s = _dot_nt(q_ref[i * blk:(i + 1) * blk, head(hh)], k_ref[k_lo:k_hi, head(hh)])
        return s + e_ref[hh, :, off + k_lo:off + k_hi]

    def finish(block, s):
        hh, i = block
        k_lo, k_hi = key_range(i)
        o = _softmax_pv(s, vt_ones[hh][:, k_lo:k_hi])
        o_ref[head(hh), i * blk:(i + 1) * blk] = o.astype(o_ref.dtype)

    _pipelined([(hh, i) for hh in range(heads) for i in range(n_blk)], scores, finish)


def _stage_weights(chunks, stage_ref, sem_ref):
    slots = stage_ref.shape[0]

    def copy(c):
        return pltpu.make_async_copy(chunks[c][0], stage_ref.at[c % slots], sem_ref.at[c % slots])

    for c in range(min(slots - 1, len(chunks))):
        copy(c).start()
    for c, (_, dst) in enumerate(chunks):
        if c + slots - 1 < len(chunks):
            copy(c + slots - 1).start()
        copy(c).wait()
        dst[...] = stage_ref[c % slots].astype(BF16)


def _mlp_kernel(n_mix, final_norm, layer, *refs):
    h_ref = refs[0]
    mix_refs = refs[1:1 + n_mix]
    wo_hbm, g_ref, w1_hbm, w2_hbm = refs[1 + n_mix:5 + n_mix]
    gf_ref = refs[5 + n_mix] if final_norm else None
    out_ref, wo_ref, w1_ref, w2_ref, stage_ref, sem_ref = refs[-6:]

    @pl.when(pl.program_id(0) == 0)
    def _():
        rows, cols = stage_ref.shape[1:]
        chunks = []
        wo_layer, mlp_layer = layer
        for hbm, idx, vmem in ((wo_hbm, wo_layer, wo_ref), (w1_hbm, mlp_layer, w1_ref),
                               (w2_hbm, mlp_layer, w2_ref)):
            for r0 in range(0, vmem.shape[0], rows):
                for c0 in range(0, vmem.shape[1], cols):
                    chunks.append((hbm.at[idx, r0:r0 + rows, c0:c0 + cols],
                                   vmem.at[r0:r0 + rows, c0:c0 + cols]))
        _stage_weights(chunks, stage_ref, sem_ref)

    mix_t = jnp.concatenate([m_ref[...] for m_ref in mix_refs], axis=0)
    h = h_ref[...] + _dot_tn(mix_t, wo_ref[...])
    hn = (_rms(h) * g_ref[...]).astype(BF16)
    for c0 in range(0, D_FF, FF_CHUNK):
        a = jnp.maximum(_dot(hn, w1_ref[:, c0:c0 + FF_CHUNK]), 0.0)
        h = h + _dot((a * a).astype(BF16), w2_ref[c0:c0 + FF_CHUNK, :])
    if final_norm:
        h = _rms(h) * gf_ref[...]
    out_ref[...] = h


def _proj1_kernel(h_ref, g_ref, wd_ref, gq_ref, gkv_ref, wuq_ref, wuk_ref, wuvt_ref,
                  cq_ref, sq_ref, ck_ref, sk_ref, q_ref, k_ref, vt_ref):
    xn = (_rms(h_ref[...]) * g_ref[...]).astype(BF16)
    hd = _dot(xn, wd_ref[...])
    c_q = (_rms(hd[:, :C_Q_RANK]) * gq_ref[...]).astype(BF16)
    c_kv = (_rms(hd[:, C_Q_RANK:C_Q_RANK + C_KV_RANK]) * gkv_ref[...]).astype(BF16)

    def rope(y, cos, sin):
        return y * cos + pltpu.roll(y, LANES - C_ROPE_DIM, 1) * sin

    cq, sq = cq_ref[...], sq_ref[...]
    k_rope = rope(hd[:, C_Q_RANK + C_KV_RANK:], ck_ref[...], sk_ref[...])
    for c0 in range(0, C_HEADS * LANES, 2 * LANES):
        q2 = _dot(c_q, wuq_ref[:, c0:c0 + 2 * LANES])
        k2 = _dot(c_kv, wuk_ref[:, c0:c0 + 2 * LANES])
        for lo in (0, LANES):
            lanes = slice(c0 + lo, c0 + lo + LANES)
            q_ref[:, lanes] = rope(q2[:, lo:lo + LANES], cq, sq).astype(BF16)
            k_ref[:, lanes] = (k2[:, lo:lo + LANES] + k_rope).astype(BF16)
    vt_ref[...] = _dot_nt(wuvt_ref[...], c_kv).astype(BF16)


def _mla_kernel(q_ref, k_ref, vt_ref, o_ref):
    s_len = q_ref.shape[0]
    tq = ATTN_Q_TILE
    heads = MLA_HEADS_PER_STEP
    v_rows = lambda hh: slice(hh * C_V_DIM, (hh + 1) * C_V_DIM)
    vt_ones = [_with_ones(vt_ref[v_rows(hh), :]) for hh in range(heads)]

    def scores(block):
        hh, i = block
        lanes = slice(hh * LANES, (hh + 1) * LANES)
        return _dot_nt(q_ref[i * tq:(i + 1) * tq, lanes], k_ref[:, lanes])

    def finish(block, s):
        hh, i = block
        o_ref[v_rows(hh), i * tq:(i + 1) * tq] = _softmax_pv(s, vt_ones[hh]).astype(o_ref.dtype)

    _pipelined([(hh, i) for hh in range(heads) for i in range(s_len // tq)], scores, finish)


def _rope_angles(pos, dim):
    inv_freq = ROPE_THETA ** (-np.arange(0, dim, 2, dtype=np.float64) / dim)
    return pos.astype(np.float64)[:, None] * inv_freq[None, :]


def _axial_tables(s_len):
    t = np.arange(s_len)
    ang = np.concatenate([_rope_angles(t // GRID_W, HEAD_DIM // 2),
                          _rope_angles(t % GRID_W, HEAD_DIM // 2)], axis=-1)
    cos, sin, zero = np.cos(ang), np.sin(ang), np.zeros_like(ang)
    c = np.concatenate([cos, cos, cos, cos], axis=-1)
    sa = np.concatenate([-sin, zero, -sin, zero], axis=-1)
    sb = np.concatenate([zero, sin, zero, sin], axis=-1)
    return [jnp.asarray(a, F32) for a in (c, sa, sb)]


def _mla_tables(s_len, scale):
    ang = _rope_angles(np.arange(s_len), C_ROPE_DIM)
    cos, sin = np.cos(ang), np.sin(ang)
    ones = np.ones((s_len, C_NOPE_DIM))
    z_nope = np.zeros((s_len, C_NOPE_DIM))
    z_tail = np.zeros((s_len, LANES - C_QK_DIM))
    c = np.concatenate([ones, cos, cos, z_tail], axis=-1)
    sn = np.concatenate([z_nope, sin, sin, z_tail], axis=-1)
    return [jnp.asarray(a * scale, F32) for a in (c, sn)]


def _rotate_half_cols(w):
    half = w.shape[-1] // 2
    return jnp.concatenate([-w[..., half:], w[..., :half]], axis=-1)


def _t5_bucket(rel):
    nb = NUM_BUCKETS // 2
    max_exact = nb // 2
    base = np.where(rel > 0, nb, 0)
    n = np.abs(rel)
    nf = np.maximum(n, 1).astype(np.float32)
    large = max_exact + (np.log(nf / np.float32(max_exact)) / np.float32(math.log(REL_MAX_DISTANCE / max_exact))
                         * np.float32(nb - max_exact)).astype(np.int32)
    large = np.minimum(large, nb - 1)
    return base + np.where(n < max_exact, n, large)


def _dilated_structure(s_len):
    delta = np.arange(2 * s_len) - s_len
    mult = np.zeros(delta.shape, np.int32)
    for w, d in B_BRANCHES:
        n_side = w // (2 * d)
        mult += ((delta % d == 0) & (np.abs(delta) <= n_side * d)).astype(np.int32)
    log_mult = np.where(mult > 0, np.log(np.maximum(mult, 1).astype(np.float64)), NEG_INF)
    onehot = (_t5_bucket(delta)[None, :] == np.arange(NUM_BUCKETS)[:, None]) & (mult > 0)[None, :]
    return jnp.asarray(onehot, F32), jnp.asarray(log_mult[None, :], F32)


def _params(*sem):
    return pltpu.CompilerParams(dimension_semantics=sem, vmem_limit_bytes=VMEM_LIMIT_BYTES)


def _resident(shape):
    return pl.BlockSpec(shape, lambda *_: (0,) * len(shape), pipeline_mode=pl.Buffered(1))


def _row2(v):
    return v.reshape(1, -1).astype(F32)


def _token_specs(tm, s_len):
    seq_tiles = s_len // tm
    tok = lambda w: pl.BlockSpec((tm, w), lambda i: (i, 0))
    tok_t = lambda w: pl.BlockSpec((None, w, tm), lambda i: (i // seq_tiles, 0, i % seq_tiles))
    table = pl.BlockSpec((tm, LANES), lambda i: (i % seq_tiles, 0))
    return tok, tok_t, table


def _mlp_call(h, mixes_t, wo_all, g, w1_all, w2_all, layer, gf, name):
    n_tok = h.shape[0]
    tm = MLP_TOKEN_TILE
    tok, tok_t, _ = _token_specs(tm, mixes_t[0].shape[2])
    in_hbm = pl.BlockSpec(memory_space=pl.ANY)
    in_specs = [tok(D_MODEL)] + [tok_t(m.shape[1]) for m in mixes_t] + [
        in_hbm, _resident((1, D_MODEL)), in_hbm, in_hbm]
    args = [h, *mixes_t, wo_all, _row2(g), w1_all, w2_all]
    if gf is not None:
        in_specs.append(_resident((1, D_MODEL)))
        args.append(_row2(gf))
    return pl.pallas_call(
        functools.partial(_mlp_kernel, len(mixes_t), gf is not None, layer),
        out_shape=jax.ShapeDtypeStruct((n_tok, D_MODEL), F32),
        grid=(n_tok // tm,),
        in_specs=in_specs,
        out_specs=tok(D_MODEL),
        scratch_shapes=[pltpu.VMEM(wo_all.shape[1:], BF16), pltpu.VMEM(w1_all.shape[1:], BF16),
                        pltpu.VMEM(w2_all.shape[1:], BF16),
                        pltpu.VMEM((WEIGHT_STAGE_SLOTS,) + WEIGHT_STAGE_CHUNK, F32),
                        pltpu.SemaphoreType.DMA((WEIGHT_STAGE_SLOTS,))],
        compiler_params=_params("arbitrary"),
        name=name,
    )(*args)


def kernel(x, norm_mix_g, norm_mlp_g, ab_w_in, a_q_norm_g, a_k_norm_g, ab_w_out, rel_bias,
           c_w_down, c_q_norm_g, c_kv_norm_g, c_w_uq, c_w_ukv, c_w_out, mlp_w1, mlp_w2,
           final_norm_g):
    b, s, d = x.shape
    n_tok = b * s
    h0 = x.reshape(n_tok, d)
    tm = PROJ0_TOKEN_TILE
    tok, tok_t, tab_spec = _token_specs(tm, s)

    tables0 = _axial_tables(s)
    gq = _row2(jnp.tile(a_q_norm_g[0], LANES // HEAD_DIM))
    gk = _row2(jnp.tile(a_k_norm_g[0], LANES // HEAD_DIM))
    w_in = ab_w_in[0].astype(BF16)
    qk0, vt0 = pl.pallas_call(
        _proj0_kernel,
        out_shape=(jax.ShapeDtypeStruct((n_tok, QK0_W), BF16),
                   jax.ShapeDtypeStruct((b, VT0_ROWS, s), BF16)),
        grid=(n_tok // tm,),
        in_specs=[tok(d), _resident((1, d)), _resident(w_in.shape),
                  _resident((1, LANES)), _resident((1, LANES)), tab_spec, tab_spec, tab_spec],
        out_specs=(tok(QK0_W), tok_t(VT0_ROWS)),
        compiler_params=_params("parallel"),
        name="proj0",
    )(h0, _row2(norm_mix_g[0]), w_in, gq, gk, *tables0)

    tq = ATTN_Q_TILE
    q_tiles = s // tq
    gq_rows = GQA_Q_ROWS
    gq_steps = s // gq_rows
    o_a = pl.pallas_call(
        _gqa_kernel,
        out_shape=jax.ShapeDtypeStruct((b, A_Q_W, s), BF16),
        grid=(b, gq_steps),
        in_specs=[pl.BlockSpec((gq_rows, A_Q_W),
                               lambda bi, qi: (bi * gq_steps + qi, OUT_QA // A_Q_W)),
                  pl.BlockSpec((s, A_KV_W), lambda bi, qi: (bi, OUT_KA // A_KV_W)),
                  pl.BlockSpec((None, A_KV_W, s), lambda bi, qi: (bi, ROW_VA // A_KV_W, 0))],
        out_specs=pl.BlockSpec((None, A_Q_W, gq_rows), lambda bi, qi: (bi, 0, qi)),
        compiler_params=_params("parallel", "parallel"),
        name="gqa",
    )(qk0, qk0, vt0)

    onehot, log_mult = _dilated_structure(s)
    dil_heads = DIL_HEADS_PER_STEP
    dil_w = dil_heads * HEAD_DIM
    groups = B_HEADS // dil_heads
    rb = rel_bias.astype(F32).reshape(NUM_BUCKETS, groups, dil_heads).transpose(1, 0, 2)
    o_b = pl.pallas_call(
        _dilated_kernel,
        out_shape=jax.ShapeDtypeStruct((b, B_W, s), BF16),
        grid=(groups, b),
        in_specs=[pl.BlockSpec((s, dil_w), lambda p, bi: (bi, OUT_QB // dil_w + p)),
                  pl.BlockSpec((s, dil_w), lambda p, bi: (bi, OUT_KB // dil_w + p)),
                  pl.BlockSpec((None, dil_w, s), lambda p, bi: (bi, ROW_VB // dil_w + p, 0)),
                  pl.BlockSpec((None, NUM_BUCKETS, dil_heads), lambda p, bi: (p, 0, 0)),
                  _resident(onehot.shape), _resident(log_mult.shape)],
        out_specs=pl.BlockSpec((None, dil_w, s), lambda p, bi: (bi, p, 0)),
        scratch_shapes=[pltpu.VMEM((dil_heads, tq, (2 * q_tiles - 1) * tq), F32)],
        compiler_params=_params("parallel", "arbitrary"),
        name="dilated",
    )(qk0, qk0, vt0, rb, onehot, log_mult)

    h1 = _mlp_call(h0, [o_a, o_b], ab_w_out, norm_mlp_g[0], mlp_w1, mlp_w2, (0, 0), None,
                   "mix0_mlp0")

    scale_c = C_QK_DIM ** -0.5 * LOG2E
    tables_q = _mla_tables(s, scale_c)
    tables_k = _mla_tables(s, 1.0)
    zeros = lambda r, c: jnp.zeros((r, c), F32)
    wd = c_w_down[0]
    kv_end = C_Q_RANK + C_KV_RANK
    wd_p = jnp.concatenate([wd[:, :kv_end], zeros(d, C_NOPE_DIM), wd[:, kv_end:],
                            _rotate_half_cols(wd[:, kv_end:])], axis=1).astype(BF16)
    wuq = c_w_uq[0].reshape(C_Q_RANK, C_HEADS, C_QK_DIM)
    wuq_p = jnp.concatenate([wuq, _rotate_half_cols(wuq[:, :, C_NOPE_DIM:])], axis=-1).reshape(
        C_Q_RANK, C_HEADS * LANES).astype(BF16)
    wukv = c_w_ukv[0].reshape(C_KV_RANK, C_HEADS, C_NOPE_DIM + C_V_DIM)
    wuk_p = jnp.pad(wukv[:, :, :C_NOPE_DIM], ((0, 0), (0, 0), (0, LANES - C_NOPE_DIM))).reshape(
        C_KV_RANK, C_HEADS * LANES).astype(BF16)
    wuv_t = wukv[:, :, C_NOPE_DIM:].reshape(C_KV_RANK, C_HEADS * C_V_DIM).T.astype(BF16)

    qk_w = C_HEADS * LANES
    v_w = C_HEADS * C_V_DIM
    tm = PROJ1_TOKEN_TILE
    tok, tok_t, tab_spec = _token_specs(tm, s)
    q_c, k_c, vt_c = pl.pallas_call(
        _proj1_kernel,
        out_shape=(jax.ShapeDtypeStruct((n_tok, qk_w), BF16),
                   jax.ShapeDtypeStruct((n_tok, qk_w), BF16),
                   jax.ShapeDtypeStruct((b, v_w, s), BF16)),
        grid=(n_tok // tm,),
        in_specs=[tok(d), _resident((1, d)), _resident(wd_p.shape),
                  _resident((1, C_Q_RANK)), _resident((1, C_KV_RANK)),
                  _resident(wuq_p.shape), _resident(wuk_p.shape), _resident(wuv_t.shape)]
                 + [tab_spec] * 4,
        out_specs=(tok(qk_w), tok(qk_w), tok_t(v_w)),
        compiler_params=_params("parallel"),
        name="proj1",
    )(h1, _row2(norm_mix_g[1]), wd_p, _row2(c_q_norm_g[0]), _row2(c_kv_norm_g[0]),
      wuq_p, wuk_p, wuv_t, *tables_q, *tables_k)

    o_c = pl.pallas_call(
        _mla_kernel,
        out_shape=jax.ShapeDtypeStruct((b, v_w, s), BF16),
        grid=(b, C_HEADS // MLA_HEADS_PER_STEP),
        in_specs=[pl.BlockSpec((s, MLA_HEADS_PER_STEP * LANES), lambda bi, p: (bi, p)),
                  pl.BlockSpec((s, MLA_HEADS_PER_STEP * LANES), lambda bi, p: (bi, p)),
                  pl.BlockSpec((None, MLA_HEADS_PER_STEP * C_V_DIM, s), lambda bi, p: (bi, p, 0))],
        out_specs=pl.BlockSpec((None, MLA_HEADS_PER_STEP * C_V_DIM, s), lambda bi, p: (bi, p, 0)),
        compiler_params=_params("parallel", "parallel"),
        name="mla",
    )(q_c, k_c, vt_c)

    out = _mlp_call(h1, [o_c], c_w_out, norm_mlp_g[1], mlp_w1, mlp_w2, (0, 1), final_norm_g,
                    "mix1_mlp1")
    return out.reshape(b, s, d)
```

```python
import functools
import math

import jax
import jax.numpy as jnp
import numpy as np
from jax import lax
from jax.experimental import pallas as pl
from jax.experimental.pallas import tpu as pltpu

F32 = jnp.float32
BF16 = jnp.bfloat16

D_MODEL = 1024
GRID_W = 64
HEAD_DIM = 64
ROPE_THETA = 10000.0
EPS = 1e-6
NEG_INF = -1e30
LOG2E = math.log2(math.e)

A_HEADS = 8
A_KV_HEADS = 2
B_HEADS = 8
B_BRANCHES = ((128, 1), (512, 4), (2048, 16))
NUM_BUCKETS = 32
REL_MAX_DISTANCE = 1024

C_HEADS = 16
C_Q_RANK = 256
C_KV_RANK = 128
C_NOPE_DIM = 64
C_ROPE_DIM = 32
C_V_DIM = 64
C_QK_DIM = C_NOPE_DIM + C_ROPE_DIM
D_FF = 4 * D_MODEL

A_Q_W = A_HEADS * HEAD_DIM
A_KV_W = A_KV_HEADS * HEAD_DIM
B_W = B_HEADS * HEAD_DIM

LANES = 128
VMEM_LIMIT_BYTES = 62 * 1024 * 1024

PROJ0_TOKEN_TILE = 512
PROJ1_TOKEN_TILE = 1024
MLP_TOKEN_TILE = 1024
ATTN_Q_TILE = 256
GQA_Q_ROWS = 512
MLA_HEADS_PER_STEP = 4
FF_CHUNK = 1024
WEIGHT_STAGE_CHUNK = (256, 1024)
WEIGHT_STAGE_SLOTS = 4

COL_QA = 0
COL_KA = COL_QA + A_Q_W
COL_VA = COL_KA + A_KV_W
COL_QB = COL_VA + A_KV_W
COL_KB = COL_QB + B_W
COL_VB = COL_KB + B_W
OUT_QA = 0
OUT_QB = OUT_QA + A_Q_W
OUT_KB = OUT_QB + B_W
OUT_KA = OUT_KB + B_W
QK0_W = OUT_KA + A_KV_W
ROW_VB = 0
ROW_VA = ROW_VB + B_W
VT0_ROWS = ROW_VA + A_KV_W
DIL_HEADS_PER_STEP = 4


def _rms(x):
    return x * lax.rsqrt(jnp.mean(x * x, axis=-1, keepdims=True) + EPS)


def _dot(a, b):
    return jnp.dot(a, b, preferred_element_type=F32)


def _dot_nt(a, b):
    return lax.dot_general(a, b, (((1,), (1,)), ((), ())), preferred_element_type=F32)


def _dot_tn(a, b):
    return lax.dot_general(a, b, (((0,), (0,)), ((), ())), preferred_element_type=F32)


def _with_ones(vt):
    return jnp.concatenate([vt, jnp.ones_like(vt)], axis=0)


def _softmax_pv(s, vt_ones):
    dv = vt_ones.shape[0] // 2
    m = jnp.max(s, axis=-1, keepdims=True)
    p = jnp.exp2(s - m).astype(BF16)
    acc = _dot(vt_ones, p.T)
    return acc[:dv] / acc[dv:dv + 1]


def _pipelined(blocks, scores, finish):
    s_next = scores(blocks[0])
    for n, blk in enumerate(blocks):
        s_cur = s_next
        if n + 1 < len(blocks):
            s_next = scores(blocks[n + 1])
        finish(blk, s_cur)


def _proj0_kernel(x_ref, g_ref, w_ref, gq_ref, gk_ref, c_ref, sa_ref, sb_ref, qk_ref, vt_ref):
    tm = x_ref.shape[0]
    xn = (_rms(x_ref[...]) * g_ref[...]).astype(BF16)
    low = lax.broadcasted_iota(jnp.int32, (tm, LANES), 1) < HEAD_DIM
    cos, sin_a, sin_b = c_ref[...], sa_ref[...], sb_ref[...]

    def norm_rope(y, gain):
        sq = y * y
        s_all = jnp.sum(sq, axis=-1, keepdims=True)
        s_low = jnp.sum(jnp.where(low, sq, 0.0), axis=-1, keepdims=True)
        ms = jnp.where(low, s_low, s_all - s_low) * (1.0 / HEAD_DIM)
        y = y * lax.rsqrt(ms + EPS) * gain
        return (y * cos + pltpu.roll(y, LANES - HEAD_DIM // 2, 1) * sin_a
                + pltpu.roll(y, HEAD_DIM // 2, 1) * sin_b)

    scale = HEAD_DIM ** -0.5 * LOG2E
    y = _dot(xn, w_ref[:, COL_QA:COL_QB])
    for c0 in range(0, A_Q_W, LANES):
        q = norm_rope(y[:, COL_QA + c0:COL_QA + c0 + LANES], gq_ref[...]) * scale
        qk_ref[:, OUT_QA + c0:OUT_QA + c0 + LANES] = q.astype(BF16)
    qk_ref[:, OUT_KA:] = norm_rope(y[:, COL_KA:COL_VA], gk_ref[...]).astype(BF16)
    vt_ref[ROW_VA:, :] = y[:, COL_VA:COL_QB].T.astype(BF16)
    qk_ref[:, OUT_QB:OUT_KB] = (_dot(xn, w_ref[:, COL_QB:COL_KB]) * scale).astype(BF16)
    qk_ref[:, OUT_KB:OUT_KA] = _dot(xn, w_ref[:, COL_KB:COL_VB]).astype(BF16)
    vt_ref[ROW_VB:ROW_VA, :] = _dot(xn, w_ref[:, COL_VB:]).T.astype(BF16)


def _gqa_kernel(q_ref, k_ref, vt_ref, o_ref):
    rep = A_HEADS // A_KV_HEADS
    head = lambda i: slice(i * HEAD_DIM, (i + 1) * HEAD_DIM)
    keys = [k_ref[:, head(g)] for g in range(A_KV_HEADS)]
    vt_ones = [_with_ones(vt_ref[head(g), :]) for g in range(A_KV_HEADS)]

    tq = ATTN_Q_TILE
    tile = lambda i: slice(i * tq, (i + 1) * tq)

    def scores(block):
        i, h = block
        return _dot_nt(q_ref[tile(i), head(h)], keys[h // rep])

    def finish(block, s):
        i, h = block
        o_ref[head(h), tile(i)] = _softmax_pv(s, vt_ones[h // rep]).astype(o_ref.dtype)

    _pipelined([(i, h) for i in range(q_ref.shape[0] // tq) for h in range(A_HEADS)],
               scores, finish)


def _dilated_kernel(q_ref, k_ref, vt_ref, rb_ref, onehot_ref, logm_ref, o_ref, e_ref):
    s_len = q_ref.shape[0]
    blk = ATTN_Q_TILE
    n_blk = s_len // blk
    reach = -(-max((w // (2 * d)) * d for w, d in B_BRANCHES) // blk)
    heads = q_ref.shape[1] // HEAD_DIM
    e_cols = (2 * n_blk - 1) * blk

    @pl.when(pl.program_id(1) == 0)
    def _():
        for hh in range(heads):
            per_delta = jnp.sum(onehot_ref[...] * rb_ref[:, hh:hh + 1], axis=0, keepdims=True)
            per_delta = (per_delta + logm_ref[...]) * LOG2E
            spread = jnp.broadcast_to(per_delta, (blk, 2 * s_len))
            toeplitz = pltpu.roll(spread, 0, 1, stride=1, stride_axis=0)
            e_ref[hh] = toeplitz[:, blk:blk + e_cols]

    head = lambda hh: slice(hh * HEAD_DIM, (hh + 1) * HEAD_DIM)
    vt_ones = [_with_ones(vt_ref[head(hh), :]) for hh in range(heads)]

    def key_range(i):
        return max(0, i - reach) * blk, min(n_blk, i + reach + 1) * blk

    def scores(block):
        hh, i = block
        k_lo, k_hi = key_range(i)
        off = (n_blk - 1 - i) * blk
        s = _dot_nt(q_ref[i * blk:(i + 1) * blk, head(hh)], k_ref[k_lo:k_hi, head(hh)])
        return s + e_ref[hh, :, off + k_lo:off + k_hi]

    def finish(block, s):
        hh, i = block
        k_lo, k_hi = key_range(i)
        o = _softmax_pv(s, vt_ones[hh][:, k_lo:k_hi])
        o_ref[head(hh), i * blk:(i + 1) * blk] = o.astype(o_ref.dtype)

    _pipelined([(hh, i) for hh in range(heads) for i in range(n_blk)], scores, finish)


def _weight_stager(chunks, stage_ref, sem_ref):
    slots = stage_ref.shape[0]
    done = [0]

    def copy(c):
        return pltpu.make_async_copy(chunks[c][0], stage_ref.at[c % slots], sem_ref.at[c % slots])

    for c in range(min(slots - 1, len(chunks))):
        copy(c).start()

    def advance(n):
        for c in range(done[0], done[0] + n):
            if c + slots - 1 < len(chunks):
                copy(c + slots - 1).start()
            copy(c).wait()
            chunks[c][1][...] = stage_ref[c % slots].astype(BF16)
        done[0] += n

    return advance


def _mlp_kernel(n_mix, final_norm, layer, *refs):
    h_ref = refs[0]
    mix_refs = refs[1:1 + n_mix]
    wo_hbm, g_ref, w1_hbm, w2_hbm = refs[1 + n_mix:5 + n_mix]
    gf_ref = refs[5 + n_mix] if final_norm else None
    out_ref, wo_ref, w1_ref, w2_ref, stage_ref, sem_ref = refs[-6:]
    rows, cols = stage_ref.shape[1:]
    wo_layer, mlp_layer = layer

    def views(hbm, idx, vmem, row_range, col_range):
        return [(hbm.at[idx, r0:r0 + rows, c0:c0 + cols], vmem.at[r0:r0 + rows, c0:c0 + cols])
                for r0 in range(*row_range, rows) for c0 in range(*col_range, cols)]

    wo_chunks = views(wo_hbm, wo_layer, wo_ref, (0, wo_ref.shape[0]), (0, wo_ref.shape[1]))
    ff_chunks = [views(w1_hbm, mlp_layer, w1_ref, (0, w1_ref.shape[0]), (c0, c0 + FF_CHUNK))
                 + views(w2_hbm, mlp_layer, w2_ref, (c0, c0 + FF_CHUNK), (0, w2_ref.shape[1]))
                 for c0 in range(0, D_FF, FF_CHUNK)]

    first = pl.program_id(0) == 0
    stager = {}

    @pl.when(first)
    def _():
        stager["advance"] = _weight_stager(wo_chunks + sum(ff_chunks, []), stage_ref, sem_ref)
        stager["advance"](len(wo_chunks))

    mix_t = jnp.concatenate([m_ref[...] for m_ref in mix_refs], axis=0)
    h = h_ref[...] + _dot_tn(mix_t, wo_ref[...])
    hn = (_rms(h) * g_ref[...]).astype(BF16)
    for n, c0 in enumerate(range(0, D_FF, FF_CHUNK)):
        pl.when(first)(functools.partial(stager["advance"], len(ff_chunks[n])))
        a = jnp.maximum(_dot(hn, w1_ref[:, c0:c0 + FF_CHUNK]), 0.0)
        h = h + _dot((a * a).astype(BF16), w2_ref[c0:c0 + FF_CHUNK, :])
    if final_norm:
        h = _rms(h) * gf_ref[...]
    out_ref[...] = h


def _proj1_kernel(h_ref, g_ref, wd_ref, gq_ref, gkv_ref, wuq_ref, wuk_ref, wuvt_ref,
                  cq_ref, sq_ref, ck_ref, sk_ref, q_ref, k_ref, vt_ref):
    xn = (_rms(h_ref[...]) * g_ref[...]).astype(BF16)
    hd = _dot(xn, wd_ref[...])
    c_q = (_rms(hd[:, :C_Q_RANK]) * gq_ref[...]).astype(BF16)
    c_kv = (_rms(hd[:, C_Q_RANK:C_Q_RANK + C_KV_RANK]) * gkv_ref[...]).astype(BF16)

    def rope(y, cos, sin):
        return y * cos + pltpu.roll(y, LANES - C_ROPE_DIM, 1) * sin

    cq, sq = cq_ref[...], sq_ref[...]
    k_rope = rope(hd[:, C_Q_RANK + C_KV_RANK:], ck_ref[...], sk_ref[...])
    for c0 in range(0, C_HEADS * LANES, 2 * LANES):
        q2 = _dot(c_q, wuq_ref[:, c0:c0 + 2 * LANES])
        k2 = _dot(c_kv, wuk_ref[:, c0:c0 + 2 * LANES])
        for lo in (0, LANES):
            lanes = slice(c0 + lo, c0 + lo + LANES)
            q_ref[:, lanes] = rope(q2[:, lo:lo + LANES], cq, sq).astype(BF16)
            k_ref[:, lanes] = (k2[:, lo:lo + LANES] + k_rope).astype(BF16)
    vt_ref[...] = _dot_nt(wuvt_ref[...], c_kv).astype(BF16)


def _mla_kernel(q_ref, k_ref, vt_ref, o_ref):
    s_len = q_ref.shape[0]
    tq = ATTN_Q_TILE
    heads = MLA_HEADS_PER_STEP
    v_rows = lambda hh: slice(hh * C_V_DIM, (hh + 1) * C_V_DIM)
    vt_ones = [_with_ones(vt_ref[v_rows(hh), :]) for hh in range(heads)]

    def scores(block):
        hh, i = block
        lanes = slice(hh * LANES, (hh + 1) * LANES)
        return _dot_nt(q_ref[i * tq:(i + 1) * tq, lanes], k_ref[:, lanes])

    def finish(block, s):
        hh, i = block
        o_ref[v_rows(hh), i * tq:(i + 1) * tq] = _softmax_pv(s, vt_ones[hh]).astype(o_ref.dtype)

    _pipelined([(hh, i) for hh in range(heads) for i in range(s_len // tq)], scores, finish)


def _rope_angles(pos, dim):
    inv_freq = ROPE_THETA ** (-np.arange(0, dim, 2, dtype=np.float64) / dim)
    return pos.astype(np.float64)[:, None] * inv_freq[None, :]


def _axial_tables(s_len):
    t = np.arange(s_len)
    ang = np.concatenate([_rope_angles(t // GRID_W, HEAD_DIM // 2),
                          _rope_angles(t % GRID_W, HEAD_DIM // 2)], axis=-1)
    cos, sin, zero = np.cos(ang), np.sin(ang), np.zeros_like(ang)
    c = np.concatenate([cos, cos, cos, cos], axis=-1)
    sa = np.concatenate([-sin, zero, -sin, zero], axis=-1)
    sb = np.concatenate([zero, sin, zero, sin], axis=-1)
    return [jnp.asarray(a, F32) for a in (c, sa, sb)]


def _mla_tables(s_len, scale):
    ang = _rope_angles(np.arange(s_len), C_ROPE_DIM)
    cos, sin = np.cos(ang), np.sin(ang)
    ones = np.ones((s_len, C_NOPE_DIM))
    z_nope = np.zeros((s_len, C_NOPE_DIM))
    z_tail = np.zeros((s_len, LANES - C_QK_DIM))
    c = np.concatenate([ones, cos, cos, z_tail], axis=-1)
    sn = np.concatenate([z_nope, sin, sin, z_tail], axis=-1)
    return [jnp.asarray(a * scale, F32) for a in (c, sn)]


def _rotate_half_cols(w):
    half = w.shape[-1] // 2
    return jnp.concatenate([-w[..., half:], w[..., :half]], axis=-1)


def _t5_bucket(rel):
    nb = NUM_BUCKETS // 2
    max_exact = nb // 2
    base = np.where(rel > 0, nb, 0)
    n = np.abs(rel)
    nf = np.maximum(n, 1).astype(np.float32)
    large = max_exact + (np.log(nf / np.float32(max_exact)) / np.float32(math.log(REL_MAX_DISTANCE / max_exact))
                         * np.float32(nb - max_exact)).astype(np.int32)
    large = np.minimum(large, nb - 1)
    return base + np.where(n < max_exact, n, large)


def _dilated_structure(s_len):
    delta = np.arange(2 * s_len) - s_len
    mult = np.zeros(delta.shape, np.int32)
    for w, d in B_BRANCHES:
        n_side = w // (2 * d)
        mult += ((delta % d == 0) & (np.abs(delta) <= n_side * d)).astype(np.int32)
    log_mult = np.where(mult > 0, np.log(np.maximum(mult, 1).astype(np.float64)), NEG_INF)
    onehot = (_t5_bucket(delta)[None, :] == np.arange(NUM_BUCKETS)[:, None]) & (mult > 0)[None, :]
    return jnp.asarray(onehot, F32), jnp.asarray(log_mult[None, :], F32)


def _params(*sem):
    return pltpu.CompilerParams(dimension_semantics=sem, vmem_limit_bytes=VMEM_LIMIT_BYTES)


def _resident(shape):
    return pl.BlockSpec(shape, lambda *_: (0,) * len(shape), pipeline_mode=pl.Buffered(1))


def _row2(v):
    return v.reshape(1, -1).astype(F32)


def _token_specs(tm, s_len):
    seq_tiles = s_len // tm
    tok = lambda w: pl.BlockSpec((tm, w), lambda i: (i, 0))
    tok_t = lambda w: pl.BlockSpec((None, w, tm), lambda i: (i // seq_tiles, 0, i % seq_tiles))
    table = pl.BlockSpec((tm, LANES), lambda i: (i % seq_tiles, 0))
    return tok, tok_t, table


def _mlp_call(h, mixes_t, wo_all, g, w1_all, w2_all, layer, gf, name):
    n_tok = h.shape[0]
    tm = MLP_TOKEN_TILE
    tok, tok_t, _ = _token_specs(tm, mixes_t[0].shape[2])
    in_hbm = pl.BlockSpec(memory_space=pl.ANY)
    in_specs = [tok(D_MODEL)] + [tok_t(m.shape[1]) for m in mixes_t] + [
        in_hbm, _resident((1, D_MODEL)), in_hbm, in_hbm]
    args = [h, *mixes_t, wo_all, _row2(g), w1_all, w2_all]
    if gf is not None:
        in_specs.append(_resident((1, D_MODEL)))
        args.append(_row2(gf))
    return pl.pallas_call(
        functools.partial(_mlp_kernel, len(mixes_t), gf is not None, layer),
        out_shape=jax.ShapeDtypeStruct((n_tok, D_MODEL), F32),
        grid=(n_tok // tm,),
        in_specs=in_specs,
        out_specs=tok(D_MODEL),
        scratch_shapes=[pltpu.VMEM(wo_all.shape[1:], BF16), pltpu.VMEM(w1_all.shape[1:], BF16),
                        pltpu.VMEM(w2_all.shape[1:], BF16),
                        pltpu.VMEM((WEIGHT_STAGE_SLOTS,) + WEIGHT_STAGE_CHUNK, F32),
                        pltpu.SemaphoreType.DMA((WEIGHT_STAGE_SLOTS,))],
        compiler_params=_params("arbitrary"),
        name=name,
    )(*args)


def kernel(x, norm_mix_g, norm_mlp_g, ab_w_in, a_q_norm_g, a_k_norm_g, ab_w_out, rel_bias,
           c_w_down, c_q_norm_g, c_kv_norm_g, c_w_uq, c_w_ukv, c_w_out, mlp_w1, mlp_w2,
           final_norm_g):
    b, s, d = x.shape
    n_tok = b * s
    h0 = x.reshape(n_tok, d)
    tm = PROJ0_TOKEN_TILE
    tok, tok_t, tab_spec = _token_specs(tm, s)

    tables0 = _axial_tables(s)
    gq = _row2(jnp.tile(a_q_norm_g[0], LANES // HEAD_DIM))
    gk = _row2(jnp.tile(a_k_norm_g[0], LANES // HEAD_DIM))
    w_in = ab_w_in[0].astype(BF16)
    qk0, vt0 = pl.pallas_call(
        _proj0_kernel,
        out_shape=(jax.ShapeDtypeStruct((n_tok, QK0_W), BF16),
                   jax.ShapeDtypeStruct((b, VT0_ROWS, s), BF16)),
        grid=(n_tok // tm,),
        in_specs=[tok(d), _resident((1, d)), _resident(w_in.shape),
                  _resident((1, LANES)), _resident((1, LANES)), tab_spec, tab_spec, tab_spec],
        out_specs=(tok(QK0_W), tok_t(VT0_ROWS)),
        compiler_params=_params("parallel"),
        name="proj0",
    )(h0, _row2(norm_mix_g[0]), w_in, gq, gk, *tables0)

    tq = ATTN_Q_TILE
    q_tiles = s // tq
    gq_rows = GQA_Q_ROWS
    gq_steps = s // gq_rows
    o_a = pl.pallas_call(
        _gqa_kernel,
        out_shape=jax.ShapeDtypeStruct((b, A_Q_W, s), BF16),
        grid=(b, gq_steps),
        in_specs=[pl.BlockSpec((gq_rows, A_Q_W),
                               lambda bi, qi: (bi * gq_steps + qi, OUT_QA // A_Q_W)),
                  pl.BlockSpec((s, A_KV_W), lambda bi, qi: (bi, OUT_KA // A_KV_W)),
                  pl.BlockSpec((None, A_KV_W, s), lambda bi, qi: (bi, ROW_VA // A_KV_W, 0))],
        out_specs=pl.BlockSpec((None, A_Q_W, gq_rows), lambda bi, qi: (bi, 0, qi)),
        compiler_params=_params("parallel", "parallel"),
        name="gqa",
    )(qk0, qk0, vt0)

    onehot, log_mult = _dilated_structure(s)
    dil_heads = DIL_HEADS_PER_STEP
    dil_w = dil_heads * HEAD_DIM
    groups = B_HEADS // dil_heads
    rb = rel_bias.astype(F32).reshape(NUM_BUCKETS, groups, dil_heads).transpose(1, 0, 2)
    o_b = pl.pallas_call(
        _dilated_kernel,
        out_shape=jax.ShapeDtypeStruct((b, B_W, s), BF16),
        grid=(groups, b),
        in_specs=[pl.BlockSpec((s, dil_w), lambda p, bi: (bi, OUT_QB // dil_w + p)),
                  pl.BlockSpec((s, dil_w), lambda p, bi: (bi, OUT_KB // dil_w + p)),
                  pl.BlockSpec((None, dil_w, s), lambda p, bi: (bi, ROW_VB // dil_w + p, 0)),
                  pl.BlockSpec((None, NUM_BUCKETS, dil_heads), lambda p, bi: (p, 0, 0)),
                  _resident(onehot.shape), _resident(log_mult.shape)],
        out_specs=pl.BlockSpec((None, dil_w, s), lambda p, bi: (bi, p, 0)),
        scratch_shapes=[pltpu.VMEM((dil_heads, tq, (2 * q_tiles - 1) * tq), F32)],
        compiler_params=_params("parallel", "arbitrary"),
        name="dilated",
    )(qk0, qk0, vt0, rb, onehot, log_mult)

    h1 = _mlp_call(h0, [o_a, o_b], ab_w_out, norm_mlp_g[0], mlp_w1, mlp_w2, (0, 0), None,
                   "mix0_mlp0")

    scale_c = C_QK_DIM ** -0.5 * LOG2E
    tables_q = _mla_tables(s, scale_c)
    tables_k = _mla_tables(s, 1.0)
    zeros = lambda r, c: jnp.zeros((r, c), F32)
    wd = c_w_down[0]
    kv_end = C_Q_RANK + C_KV_RANK
    wd_p = jnp.concatenate([wd[:, :kv_end], zeros(d, C_NOPE_DIM), wd[:, kv_end:],
                            _rotate_half_cols(wd[:, kv_end:])], axis=1).astype(BF16)
    wuq = c_w_uq[0].reshape(C_Q_RANK, C_HEADS, C_QK_DIM)
    wuq_p = jnp.concatenate([wuq, _rotate_half_cols(wuq[:, :, C_NOPE_DIM:])], axis=-1).reshape(
        C_Q_RANK, C_HEADS * LANES).astype(BF16)
    wukv = c_w_ukv[0].reshape(C_KV_RANK, C_HEADS, C_NOPE_DIM + C_V_DIM)
    wuk_p = jnp.pad(wukv[:, :, :C_NOPE_DIM], ((0, 0), (0, 0), (0, LANES - C_NOPE_DIM))).reshape(
        C_KV_RANK, C_HEADS * LANES).astype(BF16)
    wuv_t = wukv[:, :, C_NOPE_DIM:].reshape(C_KV_RANK, C_HEADS * C_V_DIM).T.astype(BF16)

    qk_w = C_HEADS * LANES
    v_w = C_HEADS * C_V_DIM
    tm = PROJ1_TOKEN_TILE
    tok, tok_t, tab_spec = _token_specs(tm, s)
    q_c, k_c, vt_c = pl.pallas_call(
        _proj1_kernel,
        out_shape=(jax.ShapeDtypeStruct((n_tok, qk_w), BF16),
                   jax.ShapeDtypeStruct((n_tok, qk_w), BF16),
                   jax.ShapeDtypeStruct((b, v_w, s), BF16)),
        grid=(n_tok // tm,),
        in_specs=[tok(d), _resident((1, d)), _resident(wd_p.shape),
                  _resident((1, C_Q_RANK)), _resident((1, C_KV_RANK)),
                  _resident(wuq_p.shape), _resident(wuk_p.shape), _resident(wuv_t.shape)]
                 + [tab_spec] * 4,
        out_specs=(tok(qk_w), tok(qk_w), tok_t(v_w)),
        compiler_params=_params("parallel"),
        name="proj1",
    )(h1, _row2(norm_mix_g[1]), wd_p, _row2(c_q_norm_g[0]), _row2(c_kv_norm_g[0]),
      wuq_p, wuk_p, wuv_t, *tables_q, *tables_k)

    o_c = pl.pallas_call(
        _mla_kernel,
        out_shape=jax.ShapeDtypeStruct((b, v_w, s), BF16),
        grid=(b, C_HEADS // MLA_HEADS_PER_STEP),
        in_specs=[pl.BlockSpec((s, MLA_HEADS_PER_STEP * LANES), lambda bi, p: (bi, p)),
                  pl.BlockSpec((s, MLA_HEADS_PER_STEP * LANES), lambda bi, p: (bi, p)),
                  pl.BlockSpec((None, MLA_HEADS_PER_STEP * C_V_DIM, s), lambda bi, p: (bi, p, 0))],
        out_specs=pl.BlockSpec((None, MLA_HEADS_PER_STEP * C_V_DIM, s), lambda bi, p: (bi, p, 0)),
        compiler_params=_params("parallel", "parallel"),
        name="mla",
    )(q_c, k_c, vt_c)

    out = _mlp_call(h1, [o_c], c_w_out, norm_mlp_g[1], mlp_w1, mlp_w2, (0, 1), final_norm_g,
                    "mix1_mlp1")
    return out.reshape(b, s, d)
```

```python
import functools
import math

import jax
import jax.numpy as jnp
import numpy as np
from jax import lax
from jax.experimental import pallas as pl
from jax.experimental.pallas import tpu as pltpu

F32 = jnp.float32
BF16 = jnp.bfloat16

D_MODEL = 1024
GRID_W = 64
HEAD_DIM = 64
ROPE_THETA = 10000.0
EPS = 1e-6
NEG_INF = -1e30
LOG2E = math.log2(math.e)

A_HEADS = 8
A_KV_HEADS = 2
B_HEADS = 8
B_BRANCHES = ((128, 1), (512, 4), (2048, 16))
NUM_BUCKETS = 32
REL_MAX_DISTANCE = 1024

C_HEADS = 16
C_Q_RANK = 256
C_KV_RANK = 128
C_NOPE_DIM = 64
C_ROPE_DIM = 32
C_V_DIM = 64
C_QK_DIM = C_NOPE_DIM + C_ROPE_DIM
D_FF = 4 * D_MODEL

A_Q_W = A_HEADS * HEAD_DIM
A_KV_W = A_KV_HEADS * HEAD_DIM
B_W = B_HEADS * HEAD_DIM

LANES = 128
VMEM_LIMIT_BYTES = 56 * 1024 * 1024

PROJ0_TOKEN_TILE = 512
PROJ1_TOKEN_TILE = 1024
MLP_TOKEN_TILE = 1024
ATTN_Q_TILE = 256
GQA_Q_ROWS = 512
MLA_HEADS_PER_STEP = 4
FF_CHUNK = 1024
WEIGHT_STAGE_CHUNK = (256, 1024)
WEIGHT_STAGE_SLOTS = 4

COL_QA = 0
COL_KA = COL_QA + A_Q_W
COL_VA = COL_KA + A_KV_W
COL_QB = COL_VA + A_KV_W
COL_KB = COL_QB + B_W
COL_VB = COL_KB + B_W
OUT_QA = 0
OUT_QB = OUT_QA + A_Q_W
OUT_KB = OUT_QB + B_W
OUT_KA = OUT_KB + B_W
QK0_W = OUT_KA + A_KV_W
ROW_VB = 0
ROW_VA = ROW_VB + B_W
VT0_ROWS = ROW_VA + A_KV_W
DIL_HEADS_PER_STEP = 4


def _rms(x):
    return x * lax.rsqrt(jnp.mean(x * x, axis=-1, keepdims=True) + EPS)


def _dot(a, b):
    return jnp.dot(a, b, preferred_element_type=F32)


def _dot_nt(a, b):
    return lax.dot_general(a, b, (((1,), (1,)), ((), ())), preferred_element_type=F32)


def _dot_tn(a, b):
    return lax.dot_general(a, b, (((0,), (0,)), ((), ())), preferred_element_type=F32)


def _with_ones(vt):
    return jnp.concatenate([vt, jnp.ones_like(vt)], axis=0)


def _softmax_pv(s, vt_ones):
    dv = vt_ones.shape[0] // 2
    m = jnp.max(s, axis=-1, keepdims=True)
    p = jnp.exp2(s - m).astype(BF16)
    acc = _dot(vt_ones, p.T)
    return acc[:dv] / acc[dv:dv + 1]


def _pipelined(blocks, scores, finish):
    s_next = scores(blocks[0])
    for n, blk in enumerate(blocks):
        s_cur = s_next
        if n + 1 < len(blocks):
            s_next = scores(blocks[n + 1])
        finish(blk, s_cur)


def _proj0_kernel(x_ref, g_ref, w_ref, gq_ref, gk_ref, c_ref, sa_ref, sb_ref, qk_ref, vt_ref):
    tm = x_ref.shape[0]
    xn = (_rms(x_ref[...]) * g_ref[...]).astype(BF16)
    low = lax.broadcasted_iota(jnp.int32, (tm, LANES), 1) < HEAD_DIM
    cos, sin_a, sin_b = c_ref[...], sa_ref[...], sb_ref[...]

    def norm_rope(y, gain):
        sq = y * y
        s_all = jnp.sum(sq, axis=-1, keepdims=True)
        s_low = jnp.sum(jnp.where(low, sq, 0.0), axis=-1, keepdims=True)
        ms = jnp.where(low, s_low, s_all - s_low) * (1.0 / HEAD_DIM)
        y = y * lax.rsqrt(ms + EPS) * gain
        return (y * cos + pltpu.roll(y, LANES - HEAD_DIM // 2, 1) * sin_a
                + pltpu.roll(y, HEAD_DIM // 2, 1) * sin_b)

    scale = HEAD_DIM ** -0.5 * LOG2E
    y = _dot(xn, w_ref[:, COL_QA:COL_QB])
    for c0 in range(0, A_Q_W, LANES):
        q = norm_rope(y[:, COL_QA + c0:COL_QA + c0 + LANES], gq_ref[...]) * scale
        qk_ref[:, OUT_QA + c0:OUT_QA + c0 + LANES] = q.astype(BF16)
    qk_ref[:, OUT_KA:] = norm_rope(y[:, COL_KA:COL_VA], gk_ref[...]).astype(BF16)
    vt_ref[ROW_VA:, :] = y[:, COL_VA:COL_QB].T.astype(BF16)
    qk_ref[:, OUT_QB:OUT_KB] = (_dot(xn, w_ref[:, COL_QB:COL_KB]) * scale).astype(BF16)
    qk_ref[:, OUT_KB:OUT_KA] = _dot(xn, w_ref[:, COL_KB:COL_VB]).astype(BF16)
    vt_ref[ROW_VB:ROW_VA, :] = _dot(xn, w_ref[:, COL_VB:]).T.astype(BF16)


def _gqa_kernel(q_ref, k_ref, vt_ref, o_ref):
    rep = A_HEADS // A_KV_HEADS
    head = lambda i: slice(i * HEAD_DIM, (i + 1) * HEAD_DIM)
    keys = [k_ref[:, head(g)] for g in range(A_KV_HEADS)]
    vt_ones = [_with_ones(vt_ref[head(g), :]) for g in range(A_KV_HEADS)]

    tq = ATTN_Q_TILE
    tile = lambda i: slice(i * tq, (i + 1) * tq)

    def scores(block):
        i, h = block
        return _dot_nt(q_ref[tile(i), head(h)], keys[h // rep])

    def finish(block, s):
        i, h = block
        o_ref[head(h), tile(i)] = _softmax_pv(s, vt_ones[h // rep]).astype(o_ref.dtype)

    _pipelined([(i, h) for i in range(q_ref.shape[0] // tq) for h in range(A_HEADS)],
               scores, finish)


def _dilated_kernel(q_ref, k_ref, vt_ref, rb_ref, onehot_ref, logm_ref, o_ref, e_ref):
    s_len = q_ref.shape[0]
    blk = ATTN_Q_TILE
    n_blk = s_len // blk
    reach = -(-max((w // (2 * d)) * d for w, d in B_BRANCHES) // blk)
    heads = q_ref.shape[1] // HEAD_DIM
    e_cols = (2 * n_blk - 1) * blk

    @pl.when(pl.program_id(1) == 0)
    def _():
        for hh in range(heads):
            per_delta = jnp.sum(onehot_ref[...] * rb_ref[:, hh:hh + 1], axis=0, keepdims=True)
            per_delta = (per_delta + logm_ref[...]) * LOG2E
            spread = jnp.broadcast_to(per_delta, (blk, 2 * s_len))
            toeplitz = pltpu.roll(spread, 0, 1, stride=1, stride_axis=0)
            e_ref[hh] = toeplitz[:, blk:blk + e_cols]

    head = lambda hh: slice(hh * HEAD_DIM, (hh + 1) * HEAD_DIM)
    vt_ones = [_with_ones(vt_ref[head(hh), :]) for hh in range(heads)]

    def key_range(i):
        return max(0, i - reach) * blk, min(n_blk, i + reach + 1) * blk

    def scores(block):
        hh, i = block
        k_lo, k_hi = key_range(i)
        off = (n_blk - 1 - i) * blk
        s = _dot_nt(q_ref[i * blk:(i + 1) * blk, head(hh)], k_ref[k_lo:k_hi, head(hh)])
        return s + e_ref[hh, :, off + k_lo:off + k_hi]

    def finish(block, s):
        hh, i = block
        k_lo, k_hi = key_range(i)
        o = _softmax_pv(s, vt_ones[hh][:, k_lo:k_hi])
        o_ref[head(hh), i * blk:(i + 1) * blk] = o.astype(o_ref.dtype)

    _pipelined([(hh, i) for hh in range(heads) for i in range(n_blk)], scores, finish)


def _stage_weights(chunks, stage_ref, sem_ref):
    slots = stage_ref.shape[0]

    def copy(c):
        return pltpu.make_async_copy(chunks[c][0], stage_ref.at[c % slots], sem_ref.at[c % slots])

    for c in range(min(slots - 1, len(chunks))):
        copy(c).start(priority=c % 2)
    for c, (_, dst) in enumerate(chunks):
        if c + slots - 1 < len(chunks):
            copy(c + slots - 1).start(priority=(c + slots - 1) % 2)
        copy(c).wait()
        dst[...] = stage_ref[c % slots].astype(BF16)


def _mlp_kernel(n_mix, final_norm, layer, *refs):
    h_ref = refs[0]
    mix_refs = refs[1:1 + n_mix]
    wo_hbm, g_ref, w1_hbm, w2_hbm = refs[1 + n_mix:5 + n_mix]
    gf_ref = refs[5 + n_mix] if final_norm else None
    out_ref, wo_ref, w1_ref, w2_ref, stage_ref, sem_ref = refs[-6:]

    @pl.when(pl.program_id(0) == 0)
    def _():
        rows, cols = stage_ref.shape[1:]
        chunks = []
        wo_layer, mlp_layer = layer
        for hbm, idx, vmem in ((wo_hbm, wo_layer, wo_ref), (w1_hbm, mlp_layer, w1_ref),
                               (w2_hbm, mlp_layer, w2_ref)):
            for r0 in range(0, vmem.shape[0], rows):
                for c0 in range(0, vmem.shape[1], cols):
                    chunks.append((hbm.at[idx, r0:r0 + rows, c0:c0 + cols],
                                   vmem.at[r0:r0 + rows, c0:c0 + cols]))
        _stage_weights(chunks, stage_ref, sem_ref)

    mix_t = jnp.concatenate([m_ref[...] for m_ref in mix_refs], axis=0)
    h = h_ref[...] + _dot_tn(mix_t, wo_ref[...])
    hn = (_rms(h) * g_ref[...]).astype(BF16)
    for c0 in range(0, D_FF, FF_CHUNK):
        a = jnp.maximum(_dot(hn, w1_ref[:, c0:c0 + FF_CHUNK]), 0.0)
        h = h + _dot((a * a).astype(BF16), w2_ref[c0:c0 + FF_CHUNK, :])
    if final_norm:
        h = _rms(h) * gf_ref[...]
    out_ref[...] = h


def _proj1_kernel(h_ref, g_ref, wd_ref, gq_ref, gkv_ref, wuq_ref, wuk_ref, wuvt_ref,
                  cq_ref, sq_ref, ck_ref, sk_ref, q_ref, k_ref, vt_ref):
    xn = (_rms(h_ref[...]) * g_ref[...]).astype(BF16)
    hd = _dot(xn, wd_ref[...])
    c_q = (_rms(hd[:, :C_Q_RANK]) * gq_ref[...]).astype(BF16)
    c_kv = (_rms(hd[:, C_Q_RANK:C_Q_RANK + C_KV_RANK]) * gkv_ref[...]).astype(BF16)

    def rope(y, cos, sin):
        return y * cos + pltpu.roll(y, LANES - C_ROPE_DIM, 1) * sin

    cq, sq = cq_ref[...], sq_ref[...]
    k_rope = rope(hd[:, C_Q_RANK + C_KV_RANK:], ck_ref[...], sk_ref[...])
    for c0 in range(0, C_HEADS * LANES, 2 * LANES):
        q2 = _dot(c_q, wuq_ref[:, c0:c0 + 2 * LANES])
        k2 = _dot(c_kv, wuk_ref[:, c0:c0 + 2 * LANES])
        for lo in (0, LANES):
            lanes = slice(c0 + lo, c0 + lo + LANES)
            q_ref[:, lanes] = rope(q2[:, lo:lo + LANES], cq, sq).astype(BF16)
            k_ref[:, lanes] = (k2[:, lo:lo + LANES] + k_rope).astype(BF16)
    vt_ref[...] = _dot_nt(wuvt_ref[...], c_kv).astype(BF16)


def _mla_kernel(q_ref, k_ref, vt_ref, o_ref):
    s_len = q_ref.shape[0]
    tq = ATTN_Q_TILE
    heads = MLA_HEADS_PER_STEP
    v_rows = lambda hh: slice(hh * C_V_DIM, (hh + 1) * C_V_DIM)
    vt_ones = [_with_ones(vt_ref[v_rows(hh), :]) for hh in range(heads)]

    def scores(block):
        hh, i = block
        lanes = slice(hh * LANES, (hh + 1) * LANES)
        return _dot_nt(q_ref[i * tq:(i + 1) * tq, lanes], k_ref[:, lanes])

    def finish(block, s):
        hh, i = block
        o_ref[v_rows(hh), i * tq:(i + 1) * tq] = _softmax_pv(s, vt_ones[hh]).astype(o_ref.dtype)

    _pipelined([(hh, i) for hh in range(heads) for i in range(s_len // tq)], scores, finish)


def _rope_angles(pos, dim):
    inv_freq = ROPE_THETA ** (-np.arange(0, dim, 2, dtype=np.float64) / dim)
    return pos.astype(np.float64)[:, None] * inv_freq[None, :]


def _axial_tables(s_len):
    t = np.arange(s_len)
    ang = np.concatenate([_rope_angles(t // GRID_W, HEAD_DIM // 2),
                          _rope_angles(t % GRID_W, HEAD_DIM // 2)], axis=-1)
    cos, sin, zero = np.cos(ang), np.sin(ang), np.zeros_like(ang)
    c = np.concatenate([cos, cos, cos, cos], axis=-1)
    sa = np.concatenate([-sin, zero, -sin, zero], axis=-1)
    sb = np.concatenate([zero, sin, zero, sin], axis=-1)
    return [jnp.asarray(a, F32) for a in (c, sa, sb)]


def _mla_tables(s_len, scale):
    ang = _rope_angles(np.arange(s_len), C_ROPE_DIM)
    cos, sin = np.cos(ang), np.sin(ang)
    ones = np.ones((s_len, C_NOPE_DIM))
    z_nope = np.zeros((s_len, C_NOPE_DIM))
    z_tail = np.zeros((s_len, LANES - C_QK_DIM))
    c = np.concatenate([ones, cos, cos, z_tail], axis=-1)
    sn = np.concatenate([z_nope, sin, sin, z_tail], axis=-1)
    return [jnp.asarray(a * scale, F32) for a in (c, sn)]


def _rotate_half_cols(w):
    half = w.shape[-1] // 2
    return jnp.concatenate([-w[..., half:], w[..., :half]], axis=-1)


def _t5_bucket(rel):
    nb = NUM_BUCKETS // 2
    max_exact = nb // 2
    base = np.where(rel > 0, nb, 0)
    n = np.abs(rel)
    nf = np.maximum(n, 1).astype(np.float32)
    large = max_exact + (np.log(nf / np.float32(max_exact)) / np.float32(math.log(REL_MAX_DISTANCE / max_exact))
                         * np.float32(nb - max_exact)).astype(np.int32)
    large = np.minimum(large, nb - 1)
    return base + np.where(n < max_exact, n, large)


def _dilated_structure(s_len):
    delta = np.arange(2 * s_len) - s_len
    mult = np.zeros(delta.shape, np.int32)
    for w, d in B_BRANCHES:
        n_side = w // (2 * d)
        mult += ((delta % d == 0) & (np.abs(delta) <= n_side * d)).astype(np.int32)
    log_mult = np.where(mult > 0, np.log(np.maximum(mult, 1).astype(np.float64)), NEG_INF)
    onehot = (_t5_bucket(delta)[None, :] == np.arange(NUM_BUCKETS)[:, None]) & (mult > 0)[None, :]
    return jnp.asarray(onehot, F32), jnp.asarray(log_mult[None, :], F32)


def _params(*sem):
    return pltpu.CompilerParams(dimension_semantics=sem, vmem_limit_bytes=VMEM_LIMIT_BYTES)


def _resident(shape):
    return pl.BlockSpec(shape, lambda *_: (0,) * len(shape), pipeline_mode=pl.Buffered(1))


def _row2(v):
    return v.reshape(1, -1).astype(F32)


def _token_specs(tm, s_len):
    seq_tiles = s_len // tm
    tok = lambda w: pl.BlockSpec((tm, w), lambda i: (i, 0))
    tok_t = lambda w: pl.BlockSpec((None, w, tm), lambda i: (i // seq_tiles, 0, i % seq_tiles))
    table = pl.BlockSpec((tm, LANES), lambda i: (i % seq_tiles, 0))
    return tok, tok_t, table


def _mlp_call(h, mixes_t, wo_all, g, w1_all, w2_all, layer, gf, name):
    n_tok = h.shape[0]
    tm = MLP_TOKEN_TILE
    tok, tok_t, _ = _token_specs(tm, mixes_t[0].shape[2])
    in_hbm = pl.BlockSpec(memory_space=pl.ANY)
    in_specs = [tok(D_MODEL)] + [tok_t(m.shape[1]) for m in mixes_t] + [
        in_hbm, _resident((1, D_MODEL)), in_hbm, in_hbm]
    args = [h, *mixes_t, wo_all, _row2(g), w1_all, w2_all]
    if gf is not None:
        in_specs.append(_resident((1, D_MODEL)))
        args.append(_row2(gf))
    return pl.pallas_call(
        functools.partial(_mlp_kernel, len(mixes_t), gf is not None, layer),
        out_shape=jax.ShapeDtypeStruct((n_tok, D_MODEL), F32),
        grid=(n_tok // tm,),
        in_specs=in_specs,
        out_specs=tok(D_MODEL),
        scratch_shapes=[pltpu.VMEM(wo_all.shape[1:], BF16), pltpu.VMEM(w1_all.shape[1:], BF16),
                        pltpu.VMEM(w2_all.shape[1:], BF16),
                        pltpu.VMEM((WEIGHT_STAGE_SLOTS,) + WEIGHT_STAGE_CHUNK, F32),
                        pltpu.SemaphoreType.DMA((WEIGHT_STAGE_SLOTS,))],
        compiler_params=_params("arbitrary"),
        name=name,
    )(*args)


def kernel(x, norm_mix_g, norm_mlp_g, ab_w_in, a_q_norm_g, a_k_norm_g, ab_w_out, rel_bias,
           c_w_down, c_q_norm_g, c_kv_norm_g, c_w_uq, c_w_ukv, c_w_out, mlp_w1, mlp_w2,
           final_norm_g):
    b, s, d = x.shape
    n_tok = b * s
    h0 = x.reshape(n_tok, d)
    tm = PROJ0_TOKEN_TILE
    tok, tok_t, tab_spec = _token_specs(tm, s)

    tables0 = _axial_tables(s)
    gq = _row2(jnp.tile(a_q_norm_g[0], LANES // HEAD_DIM))
    gk = _row2(jnp.tile(a_k_norm_g[0], LANES // HEAD_DIM))
    w_in = ab_w_in[0].astype(BF16)
    qk0, vt0 = pl.pallas_call(
        _proj0_kernel,
        out_shape=(jax.ShapeDtypeStruct((n_tok, QK0_W), BF16),
                   jax.ShapeDtypeStruct((b, VT0_ROWS, s), BF16)),
        grid=(n_tok // tm,),
        in_specs=[tok(d), _resident((1, d)), _resident(w_in.shape),
                  _resident((1, LANES)), _resident((1, LANES)), tab_spec, tab_spec, tab_spec],
        out_specs=(tok(QK0_W), tok_t(VT0_ROWS)),
        compiler_params=_params("parallel"),
        name="proj0",
    )(h0, _row2(norm_mix_g[0]), w_in, gq, gk, *tables0)

    tq = ATTN_Q_TILE
    q_tiles = s // tq
    gq_rows = GQA_Q_ROWS
    gq_steps = s // gq_rows
    o_a = pl.pallas_call(
        _gqa_kernel,
        out_shape=jax.ShapeDtypeStruct((b, A_Q_W, s), BF16),
        grid=(b, gq_steps),
        in_specs=[pl.BlockSpec((gq_rows, A_Q_W),
                               lambda bi, qi: (bi * gq_steps + qi, OUT_QA // A_Q_W)),
                  pl.BlockSpec((s, A_KV_W), lambda bi, qi: (bi, OUT_KA // A_KV_W)),
                  pl.BlockSpec((None, A_KV_W, s), lambda bi, qi: (bi, ROW_VA // A_KV_W, 0))],
        out_specs=pl.BlockSpec((None, A_Q_W, gq_rows), lambda bi, qi: (bi, 0, qi)),
        compiler_params=_params("parallel", "parallel"),
        name="gqa",
    )(qk0, qk0, vt0)

    onehot, log_mult = _dilated_structure(s)
    dil_heads = DIL_HEADS_PER_STEP
    dil_w = dil_heads * HEAD_DIM
    groups = B_HEADS // dil_heads
    rb = rel_bias.astype(F32).reshape(NUM_BUCKETS, groups, dil_heads).transpose(1, 0, 2)
    o_b = pl.pallas_call(
        _dilated_kernel,
        out_shape=jax.ShapeDtypeStruct((b, B_W, s), BF16),
        grid=(groups, b),
        in_specs=[pl.BlockSpec((s, dil_w), lambda p, bi: (bi, OUT_QB // dil_w + p)),
                  pl.BlockSpec((s, dil_w), lambda p, bi: (bi, OUT_KB // dil_w + p)),
                  pl.BlockSpec((None, dil_w, s), lambda p, bi: (bi, ROW_VB // dil_w + p, 0)),
                  pl.BlockSpec((None, NUM_BUCKETS, dil_heads), lambda p, bi: (p, 0, 0)),
                  _resident(onehot.shape), _resident(log_mult.shape)],
        out_specs=pl.BlockSpec((None, dil_w, s), lambda p, bi: (bi, p, 0)),
        scratch_shapes=[pltpu.VMEM((dil_heads, tq, (2 * q_tiles - 1) * tq), F32)],
        compiler_params=_params("parallel", "arbitrary"),
        name="dilated",
    )(qk0, qk0, vt0, rb, onehot, log_mult)

    h1 = _mlp_call(h0, [o_a, o_b], ab_w_out, norm_mlp_g[0], mlp_w1, mlp_w2, (0, 0), None,
                   "mix0_mlp0")

    scale_c = C_QK_DIM ** -0.5 * LOG2E
    tables_q = _mla_tables(s, scale_c)
    tables_k = _mla_tables(s, 1.0)
    zeros = lambda r, c: jnp.zeros((r, c), F32)
    wd = c_w_down[0]
    kv_end = C_Q_RANK + C_KV_RANK
    wd_p = jnp.concatenate([wd[:, :kv_end], zeros(d, C_NOPE_DIM), wd[:, kv_end:],
                            _rotate_half_cols(wd[:, kv_end:])], axis=1).astype(BF16)
    wuq = c_w_uq[0].reshape(C_Q_RANK, C_HEADS, C_QK_DIM)
    wuq_p = jnp.concatenate([wuq, _rotate_half_cols(wuq[:, :, C_NOPE_DIM:])], axis=-1).reshape(
        C_Q_RANK, C_HEADS * LANES).astype(BF16)
    wukv = c_w_ukv[0].reshape(C_KV_RANK, C_HEADS, C_NOPE_DIM + C_V_DIM)
    wuk_p = jnp.pad(wukv[:, :, :C_NOPE_DIM], ((0, 0), (0, 0), (0, LANES - C_NOPE_DIM))).reshape(
        C_KV_RANK, C_HEADS * LANES).astype(BF16)
    wuv_t = wukv[:, :, C_NOPE_DIM:].reshape(C_KV_RANK, C_HEADS * C_V_DIM).T.astype(BF16)

    qk_w = C_HEADS * LANES
    v_w = C_HEADS * C_V_DIM
    tm = PROJ1_TOKEN_TILE
    tok, tok_t, tab_spec = _token_specs(tm, s)
    q_c, k_c, vt_c = pl.pallas_call(
        _proj1_kernel,
        out_shape=(jax.ShapeDtypeStruct((n_tok, qk_w), BF16),
                   jax.ShapeDtypeStruct((n_tok, qk_w), BF16),
                   jax.ShapeDtypeStruct((b, v_w, s), BF16)),
        grid=(n_tok // tm,),
        in_specs=[tok(d), _resident((1, d)), _resident(wd_p.shape),
                  _resident((1, C_Q_RANK)), _resident((1, C_KV_RANK)),
                  _resident(wuq_p.shape), _resident(wuk_p.shape), _resident(wuv_t.shape)]
                 + [tab_spec] * 4,
        out_specs=(tok(qk_w), tok(qk_w), tok_t(v_w)),
        compiler_params=_params("parallel"),
        name="proj1",
    )(h1, _row2(norm_mix_g[1]), wd_p, _row2(c_q_norm_g[0]), _row2(c_kv_norm_g[0]),
      wuq_p, wuk_p, wuv_t, *tables_q, *tables_k)

    o_c = pl.pallas_call(
        _mla_kernel,
        out_shape=jax.ShapeDtypeStruct((b, v_w, s), BF16),
        grid=(b, C_HEADS // MLA_HEADS_PER_STEP),
        in_specs=[pl.BlockSpec((s, MLA_HEADS_PER_STEP * LANES), lambda bi, p: (bi, p)),
                  pl.BlockSpec((s, MLA_HEADS_PER_STEP * LANES), lambda bi, p: (bi, p)),
                  pl.BlockSpec((None, MLA_HEADS_PER_STEP * C_V_DIM, s), lambda bi, p: (bi, p, 0))],
        out_specs=pl.BlockSpec((None, MLA_HEADS_PER_STEP * C_V_DIM, s), lambda bi, p: (bi, p, 0)),
        compiler_params=_params("parallel", "parallel"),
        name="mla",
    )(q_c, k_c, vt_c)

    out = _mlp_call(h1, [o_c], c_w_out, norm_mlp_g[1], mlp_w1, mlp_w2, (0, 1), final_norm_g,
                    "mix1_mlp1")
    return out.reshape(b, s, d)
```
